```python
import jax, jax.numpy as jnp
from jax import lax
import numpy as np

D_MODEL = 1024
BATCH = 16
SEQ = 4096
DEPTH = 4

GRID_W = 64
CTX_LEN = 256
N_MIXERS = 2
N_MLSTM_LAYERS = (DEPTH + 1) // 2
N_ATT_LAYERS = DEPTH // 2
NORM_EPS = 1e-6

MLSTM_HEADS = 4
MLSTM_QK_DIM = D_MODEL // 2
MLSTM_V_DIM = D_MODEL
MLSTM_DK = MLSTM_QK_DIM // MLSTM_HEADS
MLSTM_DV = MLSTM_V_DIM // MLSTM_HEADS
MLSTM_CHUNK = 64
MLSTM_GATE_CAP = 15.0
MLSTM_N_GATES = 4
MLSTM_IN_DIM = 2 * MLSTM_QK_DIM + 2 * MLSTM_V_DIM + MLSTM_N_GATES * MLSTM_HEADS

ATT_HEADS = 16
ATT_KV_HEADS = 4
ATT_GROUP = ATT_HEADS // ATT_KV_HEADS
ATT_HEAD_DIM = D_MODEL // ATT_HEADS
ATT_Q_BLOCK = 128
ATT_IN_DIM = (ATT_HEADS + 2 * ATT_KV_HEADS) * ATT_HEAD_DIM
ROPE_THETA = 10000.0
ROPE_PAIRS_PER_AXIS = ATT_HEAD_DIM // 4

N_EXPERTS = 16
EC_CAPACITY = 2
D_EXPERT = D_MODEL

kernel_name = 'hybrid_mlstm_gqa_ec_moe_dit'


def _rmsnorm(x, g):
    xf = x.astype(jnp.float32)
    y = xf * lax.rsqrt(jnp.mean(xf * xf, axis=-1, keepdims=True) + NORM_EPS)
    return (y * g.astype(jnp.float32)).astype(x.dtype)


def _modulate(h, shift, scale):
    return h * (1 + scale) + shift


def _axial_rope(n_tok):
    rows = n_tok // GRID_W
    row = jnp.repeat(jnp.arange(rows, dtype=jnp.float32), GRID_W)
    col = jnp.tile(jnp.arange(GRID_W, dtype=jnp.float32), rows)
    inv = ROPE_THETA ** (-jnp.arange(ROPE_PAIRS_PER_AXIS, dtype=jnp.float32) / ROPE_PAIRS_PER_AXIS)
    ang = jnp.concatenate([row[:, None] * inv, col[:, None] * inv], axis=-1)
    return jnp.cos(ang), jnp.sin(ang)


def _apply_rope(x, cos, sin):
    shape = (x.shape[1],) + (1,) * (x.ndim - 3) + (cos.shape[-1],)
    cos = cos.reshape(shape).astype(x.dtype)
    sin = sin.reshape(shape).astype(x.dtype)
    x1, x2 = jnp.split(x, 2, axis=-1)
    return jnp.concatenate([x1 * cos - x2 * sin, x1 * sin + x2 * cos], axis=-1)


def _mlstm_zero_state(bsz):
    return (jnp.zeros((bsz, MLSTM_HEADS, MLSTM_DK, MLSTM_DV), jnp.float32),
            jnp.zeros((bsz, MLSTM_HEADS, MLSTM_DK), jnp.float32),
            jnp.zeros((bsz, MLSTM_HEADS), jnp.float32))


def _mlstm_scan(q, k, v, ig, lf, state):
    bsz, nh, n_tok, _ = q.shape
    n_chunks = n_tok // MLSTM_CHUNK
    causal = jnp.tril(jnp.ones((MLSTM_CHUNK, MLSTM_CHUNK), dtype=bool))

    def to_chunks(a):
        a = a.reshape(a.shape[:2] + (n_chunks, MLSTM_CHUNK) + a.shape[3:])
        return jnp.moveaxis(a, 2, 0)

    def step(carry, inp):
        c_mat, n_vec, m_st = carry
        qc, kc, vc, igc, lfc = inp
        b = jnp.cumsum(lfc, axis=-1)
        log_d = jnp.where(causal, b[..., :, None] - b[..., None, :] + igc[..., None, :], -jnp.inf)
        m_inter = b + m_st[..., None]
        m_q = jnp.maximum(m_inter, jnp.max(log_d, axis=-1))
        w_intra = jnp.exp(log_d - m_q[..., None])
        w_inter = jnp.exp(m_inter - m_q)
        s = jnp.einsum('bhjd,bhsd->bhjs', qc, kc) * w_intra
        num = jnp.einsum('bhjs,bhsv->bhjv', s, vc) + w_inter[..., None] * jnp.einsum('bhjd,bhdv->bhjv', qc, c_mat)
        den = jnp.sum(s, axis=-1) + w_inter * jnp.einsum('bhjd,bhd->bhj', qc, n_vec)
        h = num / jnp.maximum(jnp.abs(den), jnp.exp(-m_q))[..., None]
        b_last = b[..., -1]
        log_w = b_last[..., None] - b + igc
        m_new = jnp.maximum(b_last + m_st, jnp.max(log_w, axis=-1))
        w_key = jnp.exp(log_w - m_new[..., None])
        decay = jnp.exp(b_last + m_st - m_new)
        c_new = decay[..., None, None] * c_mat + jnp.einsum('bhsd,bhsv->bhdv', kc * w_key[..., None], vc)
        n_new = decay[..., None] * n_vec + jnp.einsum('bhs,bhsd->bhd', w_key, kc)
        return (c_new, n_new, m_new), h

    state, h = lax.scan(step, state, tuple(to_chunks(a) for a in (q, k, v, ig, lf)))
    h = jnp.moveaxis(h, 0, 2).reshape(bsz, nh, n_tok, MLSTM_DV)
    return h, state


def _bidir_mlstm(qkv, fwd, bwd, state_f, state_b):
    flip = lambda a: jnp.flip(a, axis=2)
    h_f, state_f = _mlstm_scan(*qkv, *fwd, state_f)
    h_b, state_b = _mlstm_scan(*(flip(a) for a in qkv), *(flip(a) for a in bwd), state_b)
    return h_f + flip(h_b), state_f, state_b


def _mlstm_mixer(h_ctx, h_lat, w_in, b_gate, w_norm, w_out, ctx_out):
    splits = [MLSTM_QK_DIM, 2 * MLSTM_QK_DIM, 2 * MLSTM_QK_DIM + MLSTM_V_DIM, 2 * MLSTM_QK_DIM + 2 * MLSTM_V_DIM]

    def project(h):
        bsz, n_tok, _ = h.shape
        q, k, v, o, g = jnp.split(h @ w_in, splits, axis=-1)

        def heads(a, d):
            return a.reshape(bsz, n_tok, MLSTM_HEADS, d).transpose(0, 2, 1, 3).astype(jnp.float32)

        q = heads(q, MLSTM_DK) * (MLSTM_DK ** -0.5)
        k = heads(k, MLSTM_DK)
        v = heads(v, MLSTM_DV)
        g = (g + b_gate).astype(jnp.float32)
        g = MLSTM_GATE_CAP * jnp.tanh(g / MLSTM_GATE_CAP)
        g = g.reshape(bsz, n_tok, MLSTM_N_GATES, MLSTM_HEADS).transpose(2, 0, 3, 1)
        fwd = (g[0], jax.nn.log_sigmoid(g[1]))
        bwd = (g[2], jax.nn.log_sigmoid(g[3]))
        return (q, k, v), o, fwd, bwd

    def readout(hh, o):
        bsz, _, n_tok, _ = hh.shape
        hh = _rmsnorm(hh, w_norm.reshape(MLSTM_HEADS, 1, MLSTM_DV))
        hh = hh.transpose(0, 2, 1, 3).reshape(bsz, n_tok, MLSTM_V_DIM).astype(o.dtype)
        return (hh * jax.nn.sigmoid(o)) @ w_out

    qkv_c, o_c, fwd_c, bwd_c = project(h_ctx)
    qkv_l, o_l, fwd_l, bwd_l = project(h_lat)
    zero = _mlstm_zero_state(h_ctx.shape[0])
    hh_c, st_f, st_b = _bidir_mlstm(qkv_c, fwd_c, bwd_c, zero, zero)
    hh_l, _, _ = _bidir_mlstm(qkv_l, fwd_l, bwd_l, st_f, st_b)
    y_lat = readout(hh_l, o_l)
    y_ctx = readout(hh_c, o_c) if ctx_out else None
    return y_ctx, y_lat


def _gqa_mixer(h_ctx, h_lat, w_in, q_norm, k_norm, w_out, cos, sin, ctx_out):
    scale = ATT_HEAD_DIM ** -0.5

    def project(h):
        bsz, n_tok, _ = h.shape
        q, k, v = jnp.split(h @ w_in, [ATT_HEADS * ATT_HEAD_DIM, (ATT_HEADS + ATT_KV_HEADS) * ATT_HEAD_DIM], axis=-1)
        q = _rmsnorm(q.reshape(bsz, n_tok, ATT_KV_HEADS, ATT_GROUP, ATT_HEAD_DIM), q_norm)
        k = _rmsnorm(k.reshape(bsz, n_tok, ATT_KV_HEADS, ATT_HEAD_DIM), k_norm)
        v = v.reshape(bsz, n_tok, ATT_KV_HEADS, ATT_HEAD_DIM)
        return q, k, v

    def attend(q, k, v):
        s = jnp.einsum('bqgrd,bkgd->bgrqk', q, k).astype(jnp.float32) * scale
        p = jax.nn.softmax(s, axis=-1).astype(v.dtype)
        return jnp.einsum('bgrqk,bkgd->bqgrd', p, v)

    bsz, n_lat, _ = h_lat.shape
    q_c, k_c, v_c = project(h_ctx)
    q_l, k_l, v_l = project(h_lat)
    q_l = _apply_rope(q_l, cos, sin)
    k_l = _apply_rope(k_l, cos, sin)
    k_all = jnp.concatenate([k_c, k_l], axis=1)
    v_all = jnp.concatenate([v_c, v_l], axis=1)
    n_blocks = n_lat // ATT_Q_BLOCK
    q_blocks = jnp.moveaxis(q_l.reshape(bsz, n_blocks, ATT_Q_BLOCK, ATT_KV_HEADS, ATT_GROUP, ATT_HEAD_DIM), 1, 0)
    o_l = lax.map(lambda qb: attend(qb, k_all, v_all), q_blocks)
    o_l = jnp.moveaxis(o_l, 0, 1).reshape(bsz, n_lat, ATT_HEADS * ATT_HEAD_DIM)
    y_lat = o_l @ w_out
    y_ctx = attend(q_c, k_c, v_c).reshape(bsz, h_ctx.shape[1], ATT_HEADS * ATT_HEAD_DIM) @ w_out if ctx_out else None
    return y_ctx, y_lat


def _ec_moe(h, w_router, w_gate, w_up, w_down):
    bsz, n_tok, d = h.shape
    cap = max(1, EC_CAPACITY * n_tok // N_EXPERTS)
    aff = jax.nn.softmax((h @ w_router).astype(jnp.float32), axis=-1)
    g, idx = lax.top_k(jnp.swapaxes(aff, 1, 2), cap)
    xg = jax.vmap(lambda hb, ib: hb[ib])(h, idx)
    a = jnp.einsum('becd,edf->becf', xg, w_gate)
    u = jnp.einsum('becd,edf->becf', xg, w_up)
    y = jnp.einsum('becf,efd->becd', jax.nn.silu(a) * u, w_down) * g[..., None].astype(h.dtype)
    return jax.vmap(lambda ib, yb: jnp.zeros((n_tok, d), yb.dtype).at[ib.reshape(-1)].add(yb.reshape(-1, d)))(idx, y)


def setup_inputs(seed: int = 0) -> dict:
    key = jax.random.key(seed)
    ks = iter(jax.random.split(key, 24))

    def nrm(shape, scale):
        return scale * jax.random.normal(next(ks), shape, jnp.float32)

    def gain(shape):
        return 1.0 + nrm(shape, 0.05)

    inv_d = D_MODEL ** -0.5
    att_w = ATT_HEADS * ATT_HEAD_DIM
    b_gate_base = jnp.tile(jnp.array([0.0, 3.0, 0.0, 3.0], jnp.float32)[:, None], (1, MLSTM_HEADS)).reshape(-1)
    return {
        'x': nrm((BATCH, SEQ, D_MODEL), 1.0),
        'c': nrm((BATCH, D_MODEL), 1.0),
        'ctx': nrm((BATCH, CTX_LEN, D_MODEL), 1.0),
        'c_ctx': nrm((D_MODEL,), 1.0),
        'w_mod': nrm((DEPTH, D_MODEL, 6 * D_MODEL), 0.5 * inv_d),
        'b_mod': nrm((DEPTH, 6 * D_MODEL), 0.02),
        'norm_mix': gain((DEPTH, D_MODEL)),
        'norm_ffn': gain((DEPTH, D_MODEL)),
        'mlstm_w_in': nrm((N_MLSTM_LAYERS, D_MODEL, MLSTM_IN_DIM), inv_d),
        'mlstm_b_gate': b_gate_base + nrm((N_MLSTM_LAYERS, MLSTM_N_GATES * MLSTM_HEADS), 0.1),
        'mlstm_norm': gain((N_MLSTM_LAYERS, MLSTM_V_DIM)),
        'mlstm_w_out': nrm((N_MLSTM_LAYERS, MLSTM_V_DIM, D_MODEL), MLSTM_V_DIM ** -0.5),
        'attn_w_in': nrm((N_ATT_LAYERS, D_MODEL, ATT_IN_DIM), inv_d),
        'attn_q_norm': gain((N_ATT_LAYERS, ATT_HEAD_DIM)),
        'attn_k_norm': gain((N_ATT_LAYERS, ATT_HEAD_DIM)),
        'attn_w_out': nrm((N_ATT_LAYERS, att_w, D_MODEL), att_w ** -0.5),
        'moe_router': nrm((DEPTH, D_MODEL, N_EXPERTS), inv_d),
        'moe_w_gate': nrm((DEPTH, N_EXPERTS, D_MODEL, D_EXPERT), inv_d),
        'moe_w_up': nrm((DEPTH, N_EXPERTS, D_MODEL, D_EXPERT), inv_d),
        'moe_w_down': nrm((DEPTH, N_EXPERTS, D_EXPERT, D_MODEL), D_EXPERT ** -0.5),
        'norm_final': gain((D_MODEL,)),
    }


def reference(x, c, ctx, c_ctx, w_mod, b_mod, norm_mix, norm_ffn, mlstm_w_in, mlstm_b_gate, mlstm_norm, mlstm_w_out,
              attn_w_in, attn_q_norm, attn_k_norm, attn_w_out, moe_router, moe_w_gate, moe_w_up, moe_w_down, norm_final):
    n_lat = x.shape[1]
    cos, sin = _axial_rope(n_lat)
    silu_c = jax.nn.silu(c)
    silu_cc = jax.nn.silu(c_ctx)
    xc = ctx
    for i in range(DEPTH):
        last = i == DEPTH - 1
        mod_l = (silu_c @ w_mod[i] + b_mod[i])[:, None, :]
        mod_c = silu_cc @ w_mod[i] + b_mod[i]
        sh1_l, sc1_l, g1_l, sh2_l, sc2_l, g2_l = jnp.split(mod_l, 6, axis=-1)
        sh1_c, sc1_c, g1_c, sh2_c, sc2_c, g2_c = jnp.split(mod_c, 6, axis=-1)
        h_l = _modulate(_rmsnorm(x, norm_mix[i]), sh1_l, sc1_l)
        h_c = _modulate(_rmsnorm(xc, norm_mix[i]), sh1_c, sc1_c)
        j = i // N_MIXERS
        if i % N_MIXERS == 0:
            y_c, y_l = _mlstm_mixer(h_c, h_l, mlstm_w_in[j], mlstm_b_gate[j], mlstm_norm[j], mlstm_w_out[j], not last)
        else:
            y_c, y_l = _gqa_mixer(h_c, h_l, attn_w_in[j], attn_q_norm[j], attn_k_norm[j], attn_w_out[j], cos, sin, not last)
        x = x + g1_l * y_l
        h_l = _modulate(_rmsnorm(x, norm_ffn[i]), sh2_l, sc2_l)
        x = x + g2_l * _ec_moe(h_l, moe_router[i], moe_w_gate[i], moe_w_up[i], moe_w_down[i])
        if not last:
            xc = xc + g1_c * y_c
            h_c = _modulate(_rmsnorm(xc, norm_ffn[i]), sh2_c, sc2_c)
            xc = xc + g2_c * _ec_moe(h_c, moe_router[i], moe_w_gate[i], moe_w_up[i], moe_w_down[i])
    return _rmsnorm(x, norm_final)
```

```python
import functools

import jax
import jax.numpy as jnp
from jax import lax
from jax.experimental import pallas as pl
from jax.experimental.pallas import tpu as pltpu

F32 = jnp.float32
BF16 = jnp.bfloat16
I32 = jnp.int32

D = 1024
TILE = 256
EPS = 1e-6
DEPTH = 4

M_HEADS = 4
M_DK = 128
M_DV = 256
M_QK = M_HEADS * M_DK
M_V = M_HEADS * M_DV
M_AUG = M_DV + 128
GATE_CAP = 15.0

A_HEADS = 16
A_KV = 4
A_GRP = 4
A_HD = 64
ROPE_THETA = 10000.0
GRID_W = 64

N_EXP = 16
EC_CAPACITY = 2
WIN = 64
GATE_LANES = 128
XW = D + GATE_LANES

LANE = 128
SUB = 8
VMEM_LIMIT = 56 * 1024 * 1024


def _cparams(sem):
    return pltpu.CompilerParams(dimension_semantics=sem, vmem_limit_bytes=VMEM_LIMIT)


def _dot(a, b):
    return jnp.dot(a, b, preferred_element_type=F32)


def _dot_nt(a, b):
    return lax.dot_general(a, b, (((1,), (1,)), ((), ())), preferred_element_type=F32)


def _split2(x):
    hi = x.astype(BF16)
    lo = (x - hi.astype(F32)).astype(BF16)
    return hi, lo


def _split3(x):
    hi = x.astype(BF16)
    r = x - hi.astype(F32)
    mid = r.astype(BF16)
    lo = (r - mid.astype(F32)).astype(BF16)
    return hi, mid, lo


def _dot3(a, b):
    ah, al = _split2(a)
    bh, bl = _split2(b)
    return _dot(ah, bh) + (_dot(ah, bl) + _dot(al, bh))


def _rms_mod(x, gain, shift, scale):
    ms = jnp.mean(x * x, axis=-1, keepdims=True)
    y = x * lax.rsqrt(ms + EPS) * gain
    return y * (1.0 + scale) + shift


def _iota(shape, dim):
    return lax.broadcasted_iota(I32, shape, dim)


def _mod_index(nb):
    return lambda b, t: (jnp.where(t == 0, nb, b), 0, 0)


def _mod_kernel(c_ref, w_ref, b_ref, o_ref):
    c = c_ref[...]
    s = c * jax.nn.sigmoid(c)
    o_ref[...] = _dot3(s, w_ref[...]) + b_ref[...]


def _modulation(cc, w_mod, b_mod):
    depth, _, n6 = w_mod.shape
    rb = cc.shape[0]
    nj = n6 // D
    out = pl.pallas_call(
        _mod_kernel,
        out_shape=jax.ShapeDtypeStruct((depth, rb, n6), F32),
        grid=(depth, nj),
        in_specs=[
            pl.BlockSpec((rb, D), lambda i, j: (0, 0)),
            pl.BlockSpec((None, D, D), lambda i, j: (i, 0, j)),
            pl.BlockSpec((None, 1, D), lambda i, j: (i, 0, j)),
        ],
        out_specs=pl.BlockSpec((None, rb, D), lambda i, j: (i, 0, j)),
        compiler_params=_cparams(("arbitrary", "arbitrary")),
        name="adaln_mod",
    )(cc, w_mod, b_mod.reshape(depth, 1, n6))
    return out.reshape(depth, rb, nj, D)


def _gate_act(g, idx):
    g = GATE_CAP * jnp.tanh(g * (1.0 / GATE_CAP))
    logsig = jnp.minimum(g, 0.0) - jnp.log(1.0 + jnp.exp(-jnp.abs(g)))
    is_forget = ((idx >> 2) & 1) == 1
    return jnp.where(is_forget, logsig, g)


def _proj_mlstm_kernel(x_ref, mod_ref, gain_ref, w_ref, wkt_ref, wg_ref, wgt_ref, bc_ref, br_ref,
                       q_ref, kt_ref, v_ref, o_ref, gc_ref, gr_ref):
    m = mod_ref[0]
    h = _rms_mod(x_ref[0], gain_ref[...], m[0:1], m[1:2])
    hb = h.astype(BF16)
    r = _dot(hb, w_ref[...])
    q_ref[0] = (r[:, :M_QK] * (M_DK ** -0.5)).astype(BF16)
    v_ref[0] = r[:, M_QK:M_QK + M_V].astype(BF16)
    o_ref[0] = r[:, M_QK + M_V:]
    kt_ref[0] = _dot_nt(wkt_ref[...], hb).astype(BF16)
    gc = _dot(hb, wg_ref[...]) + bc_ref[...]
    gc_ref[0] = _gate_act(gc, _iota(gc.shape, 1))
    gr = _dot_nt(wgt_ref[...], hb) + br_ref[...]
    gr_ref[0] = _gate_act(gr, _iota(gr.shape, 0))


def _proj_mlstm(xa, mod_i, gain, w_in, b_gate):
    nb, ntot, _ = xa.shape
    nt = ntot // TILE
    n_g = 4 * M_HEADS
    wq = w_in[:, :M_QK]
    wk = w_in[:, M_QK:2 * M_QK]
    wvo = w_in[:, 2 * M_QK:2 * M_QK + 2 * M_V]
    wg = w_in[:, 2 * M_QK + 2 * M_V:]
    w_main = jnp.concatenate([wq, wvo], axis=1).astype(BF16)
    wkt = wk.T.astype(BF16)
    wg_pad = jnp.pad(wg, ((0, 0), (0, LANE - n_g))).astype(BF16)
    wgt = wg.T.astype(BF16)
    bc = jnp.pad(b_gate, (0, LANE - n_g)).reshape(1, LANE)
    br = b_gate.reshape(n_g, 1)
    tok = lambda w: pl.BlockSpec((1, TILE, w), lambda b, t: (b, t, 0))
    full = lambda a: pl.BlockSpec(a.shape, lambda b, t: (0,) * a.ndim)
    return pl.pallas_call(
        _proj_mlstm_kernel,
        out_shape=(
            jax.ShapeDtypeStruct((nb, ntot, M_QK), BF16),
            jax.ShapeDtypeStruct((nb, M_QK, ntot), BF16),
            jax.ShapeDtypeStruct((nb, ntot, M_V), BF16),
            jax.ShapeDtypeStruct((nb, ntot, M_V), F32),
            jax.ShapeDtypeStruct((nb, ntot, LANE), F32),
            jax.ShapeDtypeStruct((nb, n_g, ntot), F32),
        ),
        grid=(nb, nt),
        in_specs=[tok(D), pl.BlockSpec((1, 6, D), _mod_index(nb)), full(gain), full(w_main), full(wkt),
                  full(wg_pad), full(wgt), full(bc), full(br)],
        out_specs=(tok(M_QK), pl.BlockSpec((1, M_QK, TILE), lambda b, t: (b, 0, t)), tok(M_V), tok(M_V),
                   tok(LANE), pl.BlockSpec((1, n_g, TILE), lambda b, t: (b, 0, t))),
        compiler_params=_cparams(("arbitrary", "arbitrary")),
        name="proj_mlstm",
    )(xa, mod_i, gain, w_main, wkt, wg_pad, wgt, bc, br)


def _mlstm_kernel(qf_ref, qb_ref, kf_ref, kb_ref, vf_ref, vb_ref, gcf_ref, gcb_ref, grf_ref, grb_ref,
                  hf_ref, hb_ref, c_ref, m_ref):
    t = pl.program_id(1)

    @pl.when(t == 0)
    def _():
        c_ref[...] = jnp.zeros(c_ref.shape, F32)
        m_ref[...] = jnp.zeros(m_ref.shape, F32)

    n = TILE
    row = _iota((n, n), 0)
    col = _iota((n, n), 1)
    lower = col <= row
    upper = col >= row
    lower_b = lower.astype(BF16)
    upper_b = upper.astype(BF16)
    ones_col = (_iota((n, M_AUG - M_DV), 1) == 0).astype(BF16)
    dirs = ((qf_ref, kf_ref, vf_ref, gcf_ref, grf_ref, hf_ref, lower, lower_b, upper_b, n - 1),
            (qb_ref, kb_ref, vb_ref, gcb_ref, grb_ref, hb_ref, upper, upper_b, lower_b, 0))
    for d, (q_ref, k_ref, v_ref, gc_ref, gr_ref, o_ref, mask, cum_l, cum_r, last) in enumerate(dirs):
        gc = gc_ref[0]
        gr = gr_ref[0]
        bcol = sum(_dot(cum_l, p) for p in _split3(gc))
        brow = sum(_dot(p, cum_r) for p in _split3(gr))
        for h in range(M_HEADS):
            gi = 8 * d + h
            gf = gi + 4
            sidx = 4 * d + h
            b_col = bcol[:, gf:gf + 1]
            b_row = brow[gf:gf + 1, :]
            ig_row = gr[gi:gi + 1, :]
            total = b_row[:, last:last + 1]
            m_st = m_ref[sidx][0:1, 0:1]
            log_d = jnp.where(mask, b_col - b_row + ig_row, -jnp.inf)
            m_inter = b_col + m_st
            m_q = jnp.maximum(m_inter, jnp.max(log_d, axis=1, keepdims=True))
            w_intra = jnp.exp(log_d - m_q)
            w_inter = jnp.exp(m_inter - m_q)
            qh = q_ref[0, :, M_DK * h:M_DK * (h + 1)]
            kth = k_ref[0, M_DK * h:M_DK * (h + 1), :]
            v_aug = jnp.concatenate([v_ref[0, :, M_DV * h:M_DV * (h + 1)], ones_col], axis=1)
            c_aug = c_ref[sidx]
            s = (_dot(qh, kth) * w_intra).astype(BF16)
            nd = _dot(s, v_aug) + w_inter * _dot(qh, c_aug.astype(BF16))
            den = nd[:, M_DV:M_DV + 1]
            o_ref[0, :, M_DV * h:M_DV * (h + 1)] = nd[:, :M_DV] / jnp.maximum(jnp.abs(den), jnp.exp(-m_q))
            log_w = total - b_row + ig_row
            m_new = jnp.maximum(total + m_st, jnp.max(log_w, axis=1, keepdims=True))
            w_key = jnp.exp(log_w - m_new)
            decay = jnp.exp(total + m_st - m_new)
            kw = (kth.astype(F32) * w_key).astype(BF16)
            c_ref[sidx] = decay * c_aug + _dot(kw, v_aug)
            m_ref[sidx] = jnp.broadcast_to(m_new, m_ref.shape[1:])


def _mlstm_scan(q, kt, v, gc, gr):
    nb, ntot, _ = q.shape
    nt = ntot // TILE
    fwd = lambda b, t: (b, t, 0)
    bwd = lambda b, t: (b, jnp.where(t == 0, 0, nt - t), 0)
    fwd_t = lambda b, t: (b, 0, t)
    bwd_t = lambda b, t: (b, 0, jnp.where(t == 0, 0, nt - t))
    n_g = gr.shape[1]
    return pl.pallas_call(
        _mlstm_kernel,
        out_shape=(jax.ShapeDtypeStruct((nb, ntot, M_V), F32), jax.ShapeDtypeStruct((nb, ntot, M_V), F32)),
        grid=(nb, nt),
        in_specs=[
            pl.BlockSpec((1, TILE, M_QK), fwd), pl.BlockSpec((1, TILE, M_QK), bwd),
            pl.BlockSpec((1, M_QK, TILE), fwd_t), pl.BlockSpec((1, M_QK, TILE), bwd_t),
            pl.BlockSpec((1, TILE, M_V), fwd), pl.BlockSpec((1, TILE, M_V), bwd),
            pl.BlockSpec((1, TILE, LANE), fwd), pl.BlockSpec((1, TILE, LANE), bwd),
            pl.BlockSpec((1, n_g, TILE), fwd_t), pl.BlockSpec((1, n_g, TILE), bwd_t),
        ],
        out_specs=(pl.BlockSpec((1, TILE, M_V), fwd), pl.BlockSpec((1, TILE, M_V), bwd)),
        scratch_shapes=[pltpu.VMEM((2 * M_HEADS, M_DK, M_AUG), F32), pltpu.VMEM((2 * M_HEADS, SUB, LANE), F32)],
        compiler_params=_cparams(("arbitrary", "arbitrary")),
        name="mlstm_scan",
    )(q, q, kt, kt, v, v, gc, gc, gr, gr)


def _head_norm(x, gain):
    w = x.shape[1]
    gsum = ((_iota((w, LANE), 0) >> 6) == _iota((w, LANE), 1)).astype(BF16)
    gexp = ((_iota((LANE, w), 1) >> 6) == _iota((LANE, w), 0)).astype(BF16)
    hi, lo = _split2(x * x)
    ssum = _dot(hi, gsum) + _dot(lo, gsum)
    rh, rl = _split2(lax.rsqrt(ssum * (1.0 / A_HD) + EPS))
    return x * (_dot(rh, gexp) + _dot(rl, gexp)) * gain


def _rope(x, cos, sin_signed):
    first = (_iota((x.shape[0], LANE), 1) & (A_HD - 1)) < A_HD // 2
    tiles = []
    for i in range(x.shape[1] // LANE):
        xt = x[:, LANE * i:LANE * (i + 1)]
        partner = jnp.where(first, pltpu.roll(xt, LANE - A_HD // 2, 1), pltpu.roll(xt, A_HD // 2, 1))
        tiles.append(xt * cos + partner * sin_signed)
    return jnp.concatenate(tiles, axis=1)


def _proj_attn_kernel(x_ref, mod_ref, gain_ref, w_ref, qg_ref, kg_ref, cos_ref, sin_ref,
                      q_ref, kt_ref, vd_ref):
    m = mod_ref[0]
    h = _rms_mod(x_ref[0], gain_ref[...], m[0:1], m[1:2])
    r = _dot(h.astype(BF16), w_ref[...])
    nq = A_HEADS * A_HD
    nk = A_KV * A_HD
    cos = cos_ref[...]
    sin = sin_ref[...]
    qn = _rope(_head_norm(r[:, :nq], qg_ref[...]), cos, sin)
    kn = _rope(_head_norm(r[:, nq:nq + nk], kg_ref[...]), cos, sin)
    v = r[:, nq + nk:]
    for hd in range(A_HEADS):
        q_ref[0, hd] = qn[:, A_HD * hd:A_HD * (hd + 1)].astype(BF16)
    kt_ref[0] = jnp.transpose(kn).astype(BF16)
    low = _iota((TILE, LANE), 1) < A_HD
    for g in range(A_KV):
        vt = v[:, LANE * (g // 2):LANE * (g // 2 + 1)]
        rolled = pltpu.roll(vt, A_HD, 1)
        keep = low if g % 2 == 0 else jnp.logical_not(low)
        vd_ref[0, g] = jnp.where(keep, vt, rolled).astype(BF16)


def _rope_tables(n_lat, n_ctx):
    rows = n_lat // GRID_W
    row = jnp.repeat(jnp.arange(rows, dtype=F32), GRID_W)
    col = jnp.tile(jnp.arange(GRID_W, dtype=F32), rows)
    pairs = A_HD // 4
    inv = ROPE_THETA ** (-jnp.arange(pairs, dtype=F32) / pairs)
    ang = jnp.concatenate([row[:, None] * inv, col[:, None] * inv], axis=-1)
    c = jnp.cos(ang)
    s = jnp.sin(ang)
    cos = jnp.concatenate([c, c, c, c], axis=-1)
    sin = jnp.concatenate([-s, s, -s, s], axis=-1)
    cos = jnp.concatenate([jnp.ones((n_ctx, LANE), F32), cos], axis=0)
    sin = jnp.concatenate([jnp.zeros((n_ctx, LANE), F32), sin], axis=0)
    return cos, sin


def _proj_attn(xa, mod_i, gain, w_in, q_norm, k_norm, cos, sin):
    nb, ntot, _ = xa.shape
    nt = ntot // TILE
    nq = A_HEADS * A_HD
    nk = A_KV * A_HD
    w = w_in.astype(BF16)
    qg = (jnp.tile(q_norm, A_HEADS) * (A_HD ** -0.5)).reshape(1, nq)
    kg = jnp.tile(k_norm, A_KV).reshape(1, nk)
    full = lambda a: pl.BlockSpec(a.shape, lambda b, t: (0,) * a.ndim)
    tab = pl.BlockSpec((TILE, LANE), lambda b, t: (t, 0))
    return pl.pallas_call(
        _proj_attn_kernel,
        out_shape=(
            jax.ShapeDtypeStruct((nb, A_HEADS, ntot, A_HD), BF16),
            jax.ShapeDtypeStruct((nb, nk, ntot), BF16),
            jax.ShapeDtypeStruct((nb, A_KV, ntot, LANE), BF16),
        ),
        grid=(nb, nt),
        in_specs=[pl.BlockSpec((1, TILE, D), lambda b, t: (b, t, 0)), pl.BlockSpec((1, 6, D), _mod_index(nb)),
                  full(gain), full(w), full(qg), full(kg), tab, tab],
        out_specs=(pl.BlockSpec((1, A_HEADS, TILE, A_HD), lambda b, t: (b, 0, t, 0)),
                   pl.BlockSpec((1, nk, TILE), lambda b, t: (b, 0, t)),
                   pl.BlockSpec((1, A_KV, TILE, LANE), lambda b, t: (b, 0, t, 0))),
        compiler_params=_cparams(("arbitrary", "arbitrary")),
        name="proj_attn",
    )(xa, mod_i, gain, w, qg, kg, cos, sin)


def _attn_kernel(q_ref, kt_ref, vd_ref, o_ref, *, ntot):
    qi = pl.program_id(2)
    tq = q_ref.shape[2]
    q = q_ref[0].reshape(A_GRP * tq, A_HD)

    def attend(nk):
        s = _dot(q, kt_ref[0, :, :nk])
        m = jnp.max(s, axis=1, keepdims=True)
        p = jnp.exp(s - m)
        l = jnp.sum(p, axis=1, keepdims=True)
        r = _dot(p.astype(BF16), vd_ref[0, 0, :nk, :]) / l
        low = _iota((tq, LANE), 1) < A_HD
        t0 = jnp.where(low, r[0:tq], r[tq:2 * tq])
        t1 = jnp.where(low, r[2 * tq:3 * tq], r[3 * tq:4 * tq])
        o_ref[0, 0] = jnp.concatenate([t0, t1], axis=1).astype(BF16)

    @pl.when(qi == 0)
    def _():
        attend(TILE)

    @pl.when(qi > 0)
    def _():
        attend(ntot)


def _attention(q, kt, vd):
    nb, _, ntot, _ = q.shape
    nq = ntot // TILE
    return pl.pallas_call(
        functools.partial(_attn_kernel, ntot=ntot),
        out_shape=jax.ShapeDtypeStruct((nb, A_KV, ntot, A_GRP * A_HD), BF16),
        grid=(nb, A_KV, nq),
        in_specs=[pl.BlockSpec((1, A_GRP, TILE, A_HD), lambda b, g, i: (b, g, i, 0)),
                  pl.BlockSpec((1, A_HD, ntot), lambda b, g, i: (b, g, 0)),
                  pl.BlockSpec((1, 1, ntot, LANE), lambda b, g, i: (b, g, 0, 0))],
        out_specs=pl.BlockSpec((1, 1, TILE, A_GRP * A_HD), lambda b, g, i: (b, g, i, 0)),
        compiler_params=_cparams(("arbitrary", "arbitrary", "arbitrary")),
        name="attention",
    )(q, kt, vd)


def _route_tail(x, y, m, gain_ref, wrt_ref, x1_ref, h2_ref, aff_ref):
    x1 = x + m[2:3] * y
    x1_ref[0] = x1
    h2 = _rms_mod(x1, gain_ref[...], m[3:4], m[4:5])
    h2_ref[0] = h2.astype(BF16)
    logits = _dot3_nt(wrt_ref[...], h2)
    e = jnp.exp(logits - jnp.max(logits, axis=0, keepdims=True))
    aff_ref[0] = e / jnp.sum(e, axis=0, keepdims=True)


def _dot3_nt(a, b):
    ah, al = _split2(a)
    bh, bl = _split2(b)
    return _dot_nt(ah, bh) + (_dot_nt(ah, bl) + _dot_nt(al, bh))


def _readout_mlstm_kernel(x_ref, mod_ref, hf_ref, hb_ref, o_ref, mn_ref, wo_ref, gain_ref, wrt_ref,
                          x1_ref, h2_ref, aff_ref):
    hh = hf_ref[0] + hb_ref[0]
    mn = mn_ref[...]
    parts = []
    for h in range(M_HEADS):
        seg = hh[:, M_DV * h:M_DV * (h + 1)]
        ms = jnp.mean(seg * seg, axis=-1, keepdims=True)
        parts.append(seg * lax.rsqrt(ms + EPS) * mn[:, M_DV * h:M_DV * (h + 1)])
    z = jnp.concatenate(parts, axis=1) * jax.nn.sigmoid(o_ref[0])
    y = _dot(z.astype(BF16), wo_ref[...])
    _route_tail(x_ref[0], y, mod_ref[0], gain_ref, wrt_ref, x1_ref, h2_ref, aff_ref)


def _readout_attn_kernel(x_ref, mod_ref, oa_ref, wo_ref, gain_ref, wrt_ref, x1_ref, h2_ref, aff_ref):
    gw = A_GRP * A_HD
    y = _dot(oa_ref[0, 0], wo_ref[0:gw, :])
    for g in range(1, A_KV):
        y = y + _dot(oa_ref[0, g], wo_ref[gw * g:gw * (g + 1), :])
    _route_tail(x_ref[0], y, mod_ref[0], gain_ref, wrt_ref, x1_ref, h2_ref, aff_ref)


def _readout(xa, mod_i, mixer_out, w_out, gain_ffn, w_router, mlstm_norm=None):
    nb, ntot, _ = xa.shape
    nt = ntot // TILE
    tok = lambda w: pl.BlockSpec((1, TILE, w), lambda b, t: (b, t, 0))
    full = lambda a: pl.BlockSpec(a.shape, lambda b, t: (0,) * a.ndim)
    wo = w_out.astype(BF16)
    wrt = w_router.T
    common_in = [tok(D), pl.BlockSpec((1, 6, D), _mod_index(nb))]
    tail_in = [full(wo), full(gain_ffn), full(wrt)]
    if mlstm_norm is not None:
        hf, hb, og = mixer_out
        kern = _readout_mlstm_kernel
        mid_in = [tok(M_V), tok(M_V), tok(M_V), full(mlstm_norm)]
        args = (xa, mod_i, hf, hb, og, mlstm_norm, wo, gain_ffn, wrt)
    else:
        kern = _readout_attn_kernel
        mid_in = [pl.BlockSpec((1, A_KV, TILE, A_GRP * A_HD), lambda b, t: (b, 0, t, 0))]
        args = (xa, mod_i, mixer_out, wo, gain_ffn, wrt)
    return pl.pallas_call(
        kern,
        out_shape=(jax.ShapeDtypeStruct((nb, ntot, D), F32), jax.ShapeDtypeStruct((nb, ntot, D), BF16),
                   jax.ShapeDtypeStruct((nb, N_EXP, ntot), F32)),
        grid=(nb, nt),
        in_specs=common_in + mid_in + tail_in,
        out_specs=(tok(D), tok(D), pl.BlockSpec((1, N_EXP, TILE), lambda b, t: (b, 0, t))),
        compiler_params=_cparams(("arbitrary", "arbitrary")),
        name="readout_route",
    )(*args)


def _slot_geometry(n_ctx, n_lat):
    cap_ctx = max(1, EC_CAPACITY * n_ctx // N_EXP)
    cap_lat = max(1, EC_CAPACITY * n_lat // N_EXP)
    lat_base = -(-cap_ctx // SUB) * SUB
    slots_max = lat_base + cap_lat + (SUB - 1) * (n_lat // TILE)
    rows_ffn = -(-(slots_max + WIN - SUB) // 16) * 16
    return cap_ctx, cap_lat, lat_base, rows_ffn


def _topk_kernel(a_ref, slot_ref, off_ref, cnt_ref, *, n_ctx, cap_ctx, cap_lat, lat_base):
    bits = lax.bitcast_convert_type(a_ref[0], I32)
    prefix = (_iota((LANE, LANE), 0) <= _iota((LANE, LANE), 1)).astype(BF16)

    def count(mask):
        return jnp.sum(mask.astype(F32), axis=1, keepdims=True)

    def select(x, cap):
        thr = jnp.zeros((N_EXP, 1), I32)
        for bit in range(30, -1, -1):
            cand = thr | (1 << bit)
            thr = jnp.where(count(x >= cand) >= cap, cand, thr)
        gt = x > thr
        eq = x == thr
        need = cap - count(gt)
        run = jnp.zeros((N_EXP, 1), F32)
        blocks = []
        for j in range(x.shape[1] // LANE):
            sl = slice(LANE * j, LANE * (j + 1))
            eqf = eq[:, sl].astype(F32)
            inc = _dot(eqf.astype(BF16), prefix)
            rank = run + inc - eqf
            blocks.append(jnp.logical_or(gt[:, sl], jnp.logical_and(eq[:, sl], rank < need)))
            run = run + inc[:, LANE - 1:LANE]
        return blocks

    blocks = select(bits[:, :n_ctx], cap_ctx) + select(bits[:, n_ctx:], cap_lat)
    per_tile = TILE // LANE
    lane = _iota((N_EXP, LANE), 1)
    off_acc = jnp.zeros((N_EXP, LANE), I32)
    cnt_acc = jnp.zeros((N_EXP, LANE), I32)
    base = jnp.zeros((N_EXP, 1), F32)
    for t in range(len(blocks) // per_tile):
        if t == n_ctx // TILE:
            base = jnp.full((N_EXP, 1), float(lat_base), F32)
        run = jnp.zeros((N_EXP, 1), F32)
        for j in range(per_tile):
            blk = blocks[per_tile * t + j]
            sf = blk.astype(F32)
            inc = _dot(sf.astype(BF16), prefix)
            pos = base + run + inc - sf
            c0 = LANE * (per_tile * t + j)
            slot_ref[0, :, c0:c0 + LANE] = jnp.where(blk, pos.astype(I32), -1)
            run = run + inc[:, LANE - 1:LANE]
        n8 = jnp.floor((run + (SUB - 1)) * (1.0 / SUB)) * SUB
        off_acc = jnp.where(lane == t, base.astype(I32), off_acc)
        cnt_acc = jnp.where(lane == t, n8.astype(I32), cnt_acc)
        base = base + n8
    off_ref[0] = off_acc
    cnt_ref[0] = cnt_acc


def _topk(aff_t, n_ctx):
    nb, _, ntot = aff_t.shape
    cap_ctx, cap_lat, lat_base, _ = _slot_geometry(n_ctx, ntot - n_ctx)
    kern = functools.partial(_topk_kernel, n_ctx=n_ctx, cap_ctx=cap_ctx, cap_lat=cap_lat, lat_base=lat_base)
    return pl.pallas_call(
        kern,
        out_shape=(jax.ShapeDtypeStruct((nb, N_EXP, ntot), I32), jax.ShapeDtypeStruct((nb, N_EXP, LANE), I32),
                   jax.ShapeDtypeStruct((nb, N_EXP, LANE), I32)),
        grid=(nb,),
        in_specs=[pl.BlockSpec((1, N_EXP, ntot), lambda b: (b, 0, 0))],
        out_specs=(pl.BlockSpec((1, N_EXP, ntot), lambda b: (b, 0, 0)),
                   pl.BlockSpec((1, N_EXP, LANE), lambda b: (b, 0, 0)),
                   pl.BlockSpec((1, N_EXP, LANE), lambda b: (b, 0, 0))),
        compiler_params=_cparams(("arbitrary",)),
        name="ec_topk",
    )(aff_t)


def _dispatch_kernel(off_s, rounds_s, h_ref, slot_ref, aff_ref, xg_ref, stage_ref, sem):
    b = pl.program_id(0)
    t = pl.program_id(1)
    nt = pl.num_programs(1)
    base = (b * nt + t) * N_EXP
    rows = N_EXP * WIN
    a = aff_ref[0]
    src = _iota((LANE, LANE), 0)
    dst = _iota((LANE, LANE), 1)
    a3 = None
    for p, piece in enumerate(_split3(a)):
        sel = jnp.logical_and(dst == 3 * src + p, src < N_EXP).astype(BF16)
        term = _dot(piece, sel)
        a3 = term if a3 is None else a3 + term
    rhs = jnp.concatenate([h_ref[0], a3.astype(BF16)], axis=1)
    expand = ((_iota((rows, LANE), 0) >> 6) == _iota((rows, LANE), 1)).astype(BF16)
    j_row = _iota((rows, 1), 0) & (WIN - 1)
    sl = slot_ref[0]
    e_row = _iota((LANE, 1), 0)
    off_v = jnp.zeros((LANE, 1), I32)
    for e in range(N_EXP):
        off_v = jnp.where(e_row == e, off_s[base + e], off_v)

    def round_body(r, carry):
        rel = jnp.clip(sl - (off_v + WIN * r), -1, WIN).astype(F32).astype(BF16)
        relx = _dot(expand, rel)
        onehot = (relx == j_row.astype(F32)).astype(BF16)
        stage_ref[...] = _dot(onehot, rhs)
        copies = []
        last_start = xg_ref.shape[2] - WIN
        for e in range(N_EXP):
            start = pl.multiple_of(jnp.minimum(off_s[base + e] + WIN * r, last_start), SUB)
            cp = pltpu.make_async_copy(stage_ref.at[pl.ds(WIN * e, WIN), :],
                                       xg_ref.at[b, e, pl.ds(start, WIN), :], sem)
            cp.start()
            copies.append(cp)
        for cp in copies:
            cp.wait()
        return carry

    lax.fori_loop(0, rounds_s[b * nt + t], round_body, 0)


def _dispatch(h2, slot_pad, aff_pad, off_flat, rounds_flat, rows_alloc):
    nb, ntot, _ = h2.shape
    nt = ntot // TILE
    grid_spec = pltpu.PrefetchScalarGridSpec(
        num_scalar_prefetch=2,
        grid=(nb, nt),
        in_specs=[pl.BlockSpec((1, TILE, D), lambda b, t, o, r: (b, t, 0)),
                  pl.BlockSpec((1, LANE, TILE), lambda b, t, o, r: (b, 0, t)),
                  pl.BlockSpec((1, TILE, LANE), lambda b, t, o, r: (b, t, 0))],
        out_specs=pl.BlockSpec(memory_space=pl.ANY),
        scratch_shapes=[pltpu.VMEM((N_EXP * WIN, XW), F32), pltpu.SemaphoreType.DMA(())],
    )
    return pl.pallas_call(
        _dispatch_kernel,
        out_shape=jax.ShapeDtypeStruct((nb, N_EXP, rows_alloc, XW), F32),
        grid_spec=grid_spec,
        compiler_params=_cparams(("arbitrary", "arbitrary")),
        name="ec_dispatch",
    )(off_flat, rounds_flat, h2, slot_pad, aff_pad)


def _ffn_kernel(xg_ref, wg_ref, wu_ref, wd_ref, y_ref):
    e = pl.program_id(0)
    x = xg_ref[0, 0, :, :D].astype(BF16)
    gl = xg_ref[0, 0, :, D:]
    lane = _iota(gl.shape, 1)
    mine = jnp.logical_and(lane >= 3 * e, lane < 3 * e + 3)
    gate = jnp.sum(jnp.where(mine, gl, 0.0), axis=1, keepdims=True)
    a = _dot(x, wg_ref[0])
    u = _dot(x, wu_ref[0])
    hm = (a * jax.nn.sigmoid(a) * u).astype(BF16)
    y_ref[0, 0] = _dot(hm, wd_ref[0]) * gate


def _expert_ffn(xg, wg, wu, wd, rows_ffn):
    nb = xg.shape[0]
    wspec = pl.BlockSpec((1, D, D), lambda e, b: (e, 0, 0))
    return pl.pallas_call(
        _ffn_kernel,
        out_shape=jax.ShapeDtypeStruct((nb, N_EXP, rows_ffn, D), F32),
        grid=(N_EXP, nb),
        in_specs=[pl.BlockSpec((1, 1, rows_ffn, XW), lambda e, b: (b, e, 0, 0)), wspec, wspec, wspec],
        out_specs=pl.BlockSpec((1, 1, rows_ffn, D), lambda e, b: (b, e, 0, 0)),
        compiler_params=_cparams(("arbitrary", "arbitrary")),
        name="ec_ffn",
    )(xg, wg, wu, wd)


def _combine_kernel(off_s, rounds_s, x_ref, mod_ref, slot_ref, y_ref, o_ref, strip_ref, sem):
    b = pl.program_id(0)
    t = pl.program_id(1)
    nt = pl.num_programs(1)
    base = (b * nt + t) * N_EXP
    cols = N_EXP * WIN
    expand = ((_iota((LANE, cols), 1) >> 6) == _iota((LANE, cols), 0)).astype(BF16)
    j_lane = (_iota((1, cols), 1) & (WIN - 1)).astype(F32)
    sl = slot_ref[0]
    e_lane = _iota((1, LANE), 1)
    off_v = jnp.zeros((1, LANE), I32)
    for e in range(N_EXP):
        off_v = jnp.where(e_lane == e, off_s[base + e], off_v)

    def round_body(r, acc):
        copies = []
        for e in range(N_EXP):
            start = pl.multiple_of(off_s[base + e] + WIN * r, SUB)
            cp = pltpu.make_async_copy(y_ref.at[b, e, pl.ds(start, WIN), :],
                                       strip_ref.at[pl.ds(WIN * e, WIN), :], sem)
            cp.start()
            copies.append(cp)
        rel = jnp.clip(sl - (off_v + WIN * r), -1, WIN).astype(F32).astype(BF16)
        relx = _dot(rel, expand)
        onehot = (relx == j_lane).astype(BF16)
        for cp in copies:
            cp.wait()
        hi, lo = _split2(strip_ref[...])
        return acc + (_dot(onehot, hi) + _dot(onehot, lo))

    acc = lax.fori_loop(0, rounds_s[b * nt + t], round_body, jnp.zeros((TILE, D), F32))
    o_ref[0] = x_ref[0] + mod_ref[0][5:6] * acc


def _combine(x1, mod_i, slot_t_pad, y, off_flat, rounds_flat):
    nb, ntot, _ = x1.shape
    nt = ntot // TILE
    grid_spec = pltpu.PrefetchScalarGridSpec(
        num_scalar_prefetch=2,
        grid=(nb, nt),
        in_specs=[pl.BlockSpec((1, TILE, D), lambda b, t, o, r: (b, t, 0)),
                  pl.BlockSpec((1, 6, D), lambda b, t, o, r: (jnp.where(t == 0, nb, b), 0, 0)),
                  pl.BlockSpec((1, TILE, LANE), lambda b, t, o, r: (b, t, 0)),
                  pl.BlockSpec(memory_space=pl.ANY)],
        out_specs=pl.BlockSpec((1, TILE, D), lambda b, t, o, r: (b, t, 0)),
        scratch_shapes=[pltpu.VMEM((N_EXP * WIN, D), F32), pltpu.SemaphoreType.DMA(())],
    )
    return pl.pallas_call(
        _combine_kernel,
        out_shape=jax.ShapeDtypeStruct((nb, ntot, D), F32),
        grid_spec=grid_spec,
        compiler_params=_cparams(("arbitrary", "arbitrary")),
        name="ec_combine",
    )(off_flat, rounds_flat, x1, mod_i, slot_t_pad, y)


def _final_kernel(x_ref, g_ref, o_ref):
    x = x_ref[0]
    ms = jnp.mean(x * x, axis=-1, keepdims=True)
    o_ref[0] = x * lax.rsqrt(ms + EPS) * g_ref[...]


def _final_norm(xa, gain, n_ctx):
    nb, ntot, _ = xa.shape
    n_lat = ntot - n_ctx
    skip = n_ctx // TILE
    return pl.pallas_call(
        _final_kernel,
        out_shape=jax.ShapeDtypeStruct((nb, n_lat, D), F32),
        grid=(nb, n_lat // TILE),
        in_specs=[pl.BlockSpec((1, TILE, D), lambda b, t: (b, t + skip, 0)),
                  pl.BlockSpec((1, D), lambda b, t: (0, 0))],
        out_specs=pl.BlockSpec((1, TILE, D), lambda b, t: (b, t, 0)),
        compiler_params=_cparams(("arbitrary", "arbitrary")),
        name="final_norm",
    )(xa, gain)


def _moe(x1, h2, aff_t, mod_i, wg, wu, wd, n_ctx):
    nb, ntot, _ = x1.shape
    nt = ntot // TILE
    _, _, _, rows_ffn = _slot_geometry(n_ctx, ntot - n_ctx)
    rows_alloc = rows_ffn + WIN
    slot, off, cnt = _topk(aff_t, n_ctx)
    off_t = jnp.swapaxes(off[:, :, :nt], 1, 2)
    cnt_t = jnp.swapaxes(cnt[:, :, :nt], 1, 2)
    rounds_c = jnp.maximum(1, jnp.max((cnt_t + WIN - 1) // WIN, axis=2))
    fill = jnp.max((rows_alloc - off_t[:, nt - 1, :] + WIN - 1) // WIN, axis=1)
    rounds_d = rounds_c.at[:, nt - 1].max(fill)
    off_flat = off_t.reshape(-1)
    slot_pad = jnp.pad(slot, ((0, 0), (0, LANE - N_EXP), (0, 0)), constant_values=-1)
    slot_t_pad = jnp.swapaxes(slot_pad, 1, 2)
    aff_pad = jnp.pad(jnp.swapaxes(aff_t, 1, 2), ((0, 0), (0, 0), (0, LANE - N_EXP)))
    xg = _dispatch(h2, slot_pad, aff_pad, off_flat, rounds_d.reshape(-1), rows_alloc)
    y = _expert_ffn(xg, wg, wu, wd, rows_ffn)
    return _combine(x1, mod_i, slot_t_pad, y, off_flat, rounds_c.reshape(-1))


def kernel(x, c, ctx, c_ctx, w_mod, b_mod, norm_mix, norm_ffn, mlstm_w_in, mlstm_b_gate, mlstm_norm, mlstm_w_out,
           attn_w_in, attn_q_norm, attn_k_norm, attn_w_out, moe_router, moe_w_gate, moe_w_up, moe_w_down,
           norm_final):
    nb, n_lat, _ = x.shape
    n_ctx = ctx.shape[1]
    depth = w_mod.shape[0]
    assert n_ctx == TILE and n_lat % TILE == 0 and x.shape[2] == D
    xa = jnp.concatenate([ctx, x], axis=1)
    rb = -(-(nb + 1) // SUB) * SUB
    cc = jnp.concatenate([c, c_ctx[None, :], jnp.zeros((rb - nb - 1, D), F32)], axis=0)
    mod = _modulation(cc, w_mod, b_mod)
    cos, sin = _rope_tables(n_lat, n_ctx)
    for i in range(depth):
        j = i // 2
        mod_i = mod[i]
        gain_mix = norm_mix[i].reshape(1, D)
        gain_ffn = norm_ffn[i].reshape(1, D)
        if i % 2 == 0:
            q, kt, v, og, gc, gr = _proj_mlstm(xa, mod_i, gain_mix, mlstm_w_in[j], mlstm_b_gate[j])
            hf, hb = _mlstm_scan(q, kt, v, gc, gr)
            x1, h2, aff_t = _readout(xa, mod_i, (hf, hb, og), mlstm_w_out[j], gain_ffn, moe_router[i],
                                     mlstm_norm=mlstm_norm[j].reshape(1, M_V))
        else:
            q, kt, vd = _proj_attn(xa, mod_i, gain_mix, attn_w_in[j], attn_q_norm[j], attn_k_norm[j], cos, sin)
            oa = _attention(q, kt, vd)
            x1, h2, aff_t = _readout(xa, mod_i, oa, attn_w_out[j], gain_ffn, moe_router[i])
        xa = _moe(x1, h2, aff_t, mod_i, moe_w_gate[i].astype(BF16), moe_w_up[i].astype(BF16),
                  moe_w_down[i].astype(BF16), n_ctx)
    return _final_norm(xa, norm_final.reshape(1, D), n_ctx)
```

```python
import functools

import jax
import jax.numpy as jnp
from jax import lax
from jax.experimental import pallas as pl
from jax.experimental.pallas import tpu as pltpu

F32 = jnp.float32
BF16 = jnp.bfloat16
I32 = jnp.int32

D = 1024
TILE = 256
EPS = 1e-6
DEPTH = 4

M_HEADS = 4
M_DK = 128
M_DV = 256
M_QK = M_HEADS * M_DK
M_V = M_HEADS * M_DV
M_AUG = M_DV + 128
GATE_CAP = 15.0

A_HEADS = 16
A_KV = 4
A_GRP = 4
A_HD = 64
ROPE_THETA = 10000.0
GRID_W = 64
LOG2E = 1.4426950408889634

N_EXP = 16
EC_CAPACITY = 2
WIN = 64
GATE_LANES = 128
XW = D + GATE_LANES

LANE = 128
SUB = 8
VMEM_LIMIT = 56 * 1024 * 1024


def _cparams(sem):
    return pltpu.CompilerParams(dimension_semantics=sem, vmem_limit_bytes=VMEM_LIMIT)


def _dot(a, b):
    return jnp.dot(a, b, preferred_element_type=F32)


def _dot_nt(a, b):
    return lax.dot_general(a, b, (((1,), (1,)), ((), ())), preferred_element_type=F32)


def _split2(x):
    hi = x.astype(BF16)
    lo = (x - hi.astype(F32)).astype(BF16)
    return hi, lo


def _split3(x):
    hi = x.astype(BF16)
    r = x - hi.astype(F32)
    mid = r.astype(BF16)
    lo = (r - mid.astype(F32)).astype(BF16)
    return hi, mid, lo


def _dot3(a, b):
    ah, al = _split2(a)
    bh, bl = _split2(b)
    return _dot(ah, bh) + (_dot(ah, bl) + _dot(al, bh))


def _rms_mod(x, gain, shift, scale):
    ms = jnp.mean(x * x, axis=-1, keepdims=True)
    y = x * lax.rsqrt(ms + EPS) * gain
    return y * (1.0 + scale) + shift


def _iota(shape, dim):
    return lax.broadcasted_iota(I32, shape, dim)


def _mod_index(nb):
    return lambda b, t: (jnp.where(t == 0, nb, b), 0, 0)


def _mod_kernel(c_ref, w_ref, b_ref, o_ref):
    c = c_ref[...]
    s = c * jax.nn.sigmoid(c)
    o_ref[...] = _dot3(s, w_ref[...]) + b_ref[...]


def _modulation(cc, w_mod, b_mod):
    depth, _, n6 = w_mod.shape
    rb = cc.shape[0]
    nj = n6 // D
    out = pl.pallas_call(
        _mod_kernel,
        out_shape=jax.ShapeDtypeStruct((depth, rb, n6), F32),
        grid=(depth, nj),
        in_specs=[
            pl.BlockSpec((rb, D), lambda i, j: (0, 0)),
            pl.BlockSpec((None, D, D), lambda i, j: (i, 0, j)),
            pl.BlockSpec((None, 1, D), lambda i, j: (i, 0, j)),
        ],
        out_specs=pl.BlockSpec((None, rb, D), lambda i, j: (i, 0, j)),
        compiler_params=_cparams(("arbitrary", "arbitrary")),
        name="adaln_mod",
    )(cc, w_mod, b_mod.reshape(depth, 1, n6))
    return out.reshape(depth, rb, nj, D)


def _gate_act(g, idx):
    g = GATE_CAP * jnp.tanh(g * (1.0 / GATE_CAP))
    logsig = jnp.minimum(g, 0.0) - jnp.log(1.0 + jnp.exp(-jnp.abs(g)))
    is_forget = ((idx >> 2) & 1) == 1
    return jnp.where(is_forget, logsig, g)


def _proj_mlstm_kernel(x_ref, mod_ref, gain_ref, w_ref, wkt_ref, wg_ref, wgt_ref, bc_ref, br_ref,
                       q_ref, kt_ref, v_ref, o_ref, gc_ref, gr_ref):
    m = mod_ref[0]
    h = _rms_mod(x_ref[0], gain_ref[...], m[0:1], m[1:2])
    hb = h.astype(BF16)
    r = _dot(hb, w_ref[...])
    q_ref[0] = (r[:, :M_QK] * (M_DK ** -0.5)).astype(BF16)
    v_ref[0] = r[:, M_QK:M_QK + M_V].astype(BF16)
    o_ref[0] = r[:, M_QK + M_V:]
    kt_ref[0] = _dot_nt(wkt_ref[...], hb).astype(BF16)
    gc = _dot(hb, wg_ref[...]) + bc_ref[...]
    gc_ref[0] = _gate_act(gc, _iota(gc.shape, 1))
    gr = _dot_nt(wgt_ref[...], hb) + br_ref[...]
    gr_ref[0] = _gate_act(gr, _iota(gr.shape, 0))


def _proj_mlstm(xa, mod_i, gain, w_in, b_gate):
    nb, ntot, _ = xa.shape
    nt = ntot // TILE
    n_g = 4 * M_HEADS
    wq = w_in[:, :M_QK]
    wk = w_in[:, M_QK:2 * M_QK]
    wvo = w_in[:, 2 * M_QK:2 * M_QK + 2 * M_V]
    wg = w_in[:, 2 * M_QK + 2 * M_V:]
    w_main = jnp.concatenate([wq, wvo], axis=1).astype(BF16)
    wkt = wk.T.astype(BF16)
    wg_pad = jnp.pad(wg, ((0, 0), (0, LANE - n_g))).astype(BF16)
    wgt = wg.T.astype(BF16)
    bc = jnp.pad(b_gate, (0, LANE - n_g)).reshape(1, LANE)
    br = b_gate.reshape(n_g, 1)
    tok = lambda w: pl.BlockSpec((1, TILE, w), lambda b, t: (b, t, 0))
    full = lambda a: pl.BlockSpec(a.shape, lambda b, t: (0,) * a.ndim)
    return pl.pallas_call(
        _proj_mlstm_kernel,
        out_shape=(
            jax.ShapeDtypeStruct((nb, ntot, M_QK), BF16),
            jax.ShapeDtypeStruct((nb, M_QK, ntot), BF16),
            jax.ShapeDtypeStruct((nb, ntot, M_V), BF16),
            jax.ShapeDtypeStruct((nb, ntot, M_V), F32),
            jax.ShapeDtypeStruct((nb, ntot, LANE), F32),
            jax.ShapeDtypeStruct((nb, n_g, ntot), F32),
        ),
        grid=(nb, nt),
        in_specs=[tok(D), pl.BlockSpec((1, 6, D), _mod_index(nb)), full(gain), full(w_main), full(wkt),
                  full(wg_pad), full(wgt), full(bc), full(br)],
        out_specs=(tok(M_QK), pl.BlockSpec((1, M_QK, TILE), lambda b, t: (b, 0, t)), tok(M_V), tok(M_V),
                   tok(LANE), pl.BlockSpec((1, n_g, TILE), lambda b, t: (b, 0, t))),
        compiler_params=_cparams(("arbitrary", "arbitrary")),
        name="proj_mlstm",
    )(xa, mod_i, gain, w_main, wkt, wg_pad, wgt, bc, br)


def _mlstm_kernel(qf_ref, qb_ref, kf_ref, kb_ref, vf_ref, vb_ref, gcf_ref, gcb_ref, grf_ref, grb_ref,
                  hf_ref, hb_ref, c_ref, m_ref):
    t = pl.program_id(1)

    @pl.when(t == 0)
    def _():
        c_ref[...] = jnp.zeros(c_ref.shape, F32)
        m_ref[...] = jnp.zeros(m_ref.shape, F32)

    n = TILE
    row = _iota((n, n), 0)
    col = _iota((n, n), 1)
    lower = col <= row
    upper = col >= row
    lower_b = lower.astype(BF16)
    upper_b = upper.astype(BF16)
    ones_col = (_iota((n, M_AUG - M_DV), 1) == 0).astype(BF16)
    dirs = ((qf_ref, kf_ref, vf_ref, gcf_ref, grf_ref, hf_ref, lower, lower_b, upper_b, n - 1),
            (qb_ref, kb_ref, vb_ref, gcb_ref, grb_ref, hb_ref, upper, upper_b, lower_b, 0))
    for d, (q_ref, k_ref, v_ref, gc_ref, gr_ref, o_ref, mask, cum_l, cum_r, last) in enumerate(dirs):
        gc = gc_ref[0]
        gr = gr_ref[0]
        bcol = sum(_dot(cum_l, p) for p in _split3(gc))
        brow = sum(_dot(p, cum_r) for p in _split3(gr))
        for h in range(M_HEADS):
            gi = 8 * d + h
            gf = gi + 4
            sidx = 4 * d + h
            b_col = bcol[:, gf:gf + 1]
            b_row = brow[gf:gf + 1, :]
            ig_row = gr[gi:gi + 1, :]
            total = b_row[:, last:last + 1]
            m_st = m_ref[sidx][0:1, 0:1]
            log_d = jnp.where(mask, b_col - b_row + ig_row, -jnp.inf)
            m_inter = b_col + m_st
            m_q = jnp.maximum(m_inter, jnp.max(log_d, axis=1, keepdims=True))
            w_intra = jnp.exp(log_d - m_q)
            w_inter = jnp.exp(m_inter - m_q)
            qh = q_ref[0, :, M_DK * h:M_DK * (h + 1)]
            kth = k_ref[0, M_DK * h:M_DK * (h + 1), :]
            v_aug = jnp.concatenate([v_ref[0, :, M_DV * h:M_DV * (h + 1)], ones_col], axis=1)
            c_aug = c_ref[sidx]
            s = (_dot(qh, kth) * w_intra).astype(BF16)
            nd = _dot(s, v_aug) + w_inter * _dot(qh, c_aug.astype(BF16))
            den = nd[:, M_DV:M_DV + 1]
            o_ref[0, :, M_DV * h:M_DV * (h + 1)] = nd[:, :M_DV] / jnp.maximum(jnp.abs(den), jnp.exp(-m_q))
            log_w = total - b_row + ig_row
            m_new = jnp.maximum(total + m_st, jnp.max(log_w, axis=1, keepdims=True))
            w_key = jnp.exp(log_w - m_new)
            decay = jnp.exp(total + m_st - m_new)
            kw = (kth.astype(F32) * w_key).astype(BF16)
            c_ref[sidx] = decay * c_aug + _dot(kw, v_aug)
            m_ref[sidx] = jnp.broadcast_to(m_new, m_ref.shape[1:])


def _mlstm_scan(q, kt, v, gc, gr):
    nb, ntot, _ = q.shape
    nt = ntot // TILE
    fwd = lambda b, t: (b, t, 0)
    bwd = lambda b, t: (b, jnp.where(t == 0, 0, nt - t), 0)
    fwd_t = lambda b, t: (b, 0, t)
    bwd_t = lambda b, t: (b, 0, jnp.where(t == 0, 0, nt - t))
    n_g = gr.shape[1]
    return pl.pallas_call(
        _mlstm_kernel,
        out_shape=(jax.ShapeDtypeStruct((nb, ntot, M_V), F32), jax.ShapeDtypeStruct((nb, ntot, M_V), F32)),
        grid=(nb, nt),
        in_specs=[
            pl.BlockSpec((1, TILE, M_QK), fwd), pl.BlockSpec((1, TILE, M_QK), bwd),
            pl.BlockSpec((1, M_QK, TILE), fwd_t), pl.BlockSpec((1, M_QK, TILE), bwd_t),
            pl.BlockSpec((1, TILE, M_V), fwd), pl.BlockSpec((1, TILE, M_V), bwd),
            pl.BlockSpec((1, TILE, LANE), fwd), pl.BlockSpec((1, TILE, LANE), bwd),
            pl.BlockSpec((1, n_g, TILE), fwd_t), pl.BlockSpec((1, n_g, TILE), bwd_t),
        ],
        out_specs=(pl.BlockSpec((1, TILE, M_V), fwd), pl.BlockSpec((1, TILE, M_V), bwd)),
        scratch_shapes=[pltpu.VMEM((2 * M_HEADS, M_DK, M_AUG), F32), pltpu.VMEM((2 * M_HEADS, SUB, LANE), F32)],
        compiler_params=_cparams(("arbitrary", "arbitrary")),
        name="mlstm_scan",
    )(q, q, kt, kt, v, v, gc, gc, gr, gr)


def _head_norm(x, gain):
    w = x.shape[1]
    gsum = ((_iota((w, LANE), 0) >> 6) == _iota((w, LANE), 1)).astype(BF16)
    gexp = ((_iota((LANE, w), 1) >> 6) == _iota((LANE, w), 0)).astype(BF16)
    hi, lo = _split2(x * x)
    ssum = _dot(hi, gsum) + _dot(lo, gsum)
    rh, rl = _split2(lax.rsqrt(ssum * (1.0 / A_HD) + EPS))
    return x * (_dot(rh, gexp) + _dot(rl, gexp)) * gain


def _rope(x, cos, sin_signed):
    first = (_iota((x.shape[0], LANE), 1) & (A_HD - 1)) < A_HD // 2
    tiles = []
    for i in range(x.shape[1] // LANE):
        xt = x[:, LANE * i:LANE * (i + 1)]
        partner = jnp.where(first, pltpu.roll(xt, LANE - A_HD // 2, 1), pltpu.roll(xt, A_HD // 2, 1))
        tiles.append(xt * cos + partner * sin_signed)
    return jnp.concatenate(tiles, axis=1)


def _proj_attn_kernel(x_ref, mod_ref, gain_ref, w_ref, qg_ref, kg_ref, cos_ref, sin_ref,
                      q_ref, kt_ref, vd_ref):
    m = mod_ref[0]
    h = _rms_mod(x_ref[0], gain_ref[...], m[0:1], m[1:2])
    r = _dot(h.astype(BF16), w_ref[...])
    nq = A_HEADS * A_HD
    nk = A_KV * A_HD
    cos = cos_ref[...]
    sin = sin_ref[...]
    qn = _rope(_head_norm(r[:, :nq], qg_ref[...]), cos, sin)
    kn = _rope(_head_norm(r[:, nq:nq + nk], kg_ref[...]), cos, sin)
    v = r[:, nq + nk:]
    for hd in range(A_HEADS):
        q_ref[0, hd] = qn[:, A_HD * hd:A_HD * (hd + 1)].astype(BF16)
    kt_ref[0] = jnp.transpose(kn).astype(BF16)
    low = _iota((TILE, LANE), 1) < A_HD
    for g in range(A_KV):
        vt = v[:, LANE * (g // 2):LANE * (g // 2 + 1)]
        rolled = pltpu.roll(vt, A_HD, 1)
        keep = low if g % 2 == 0 else jnp.logical_not(low)
        vd_ref[0, g] = jnp.where(keep, vt, rolled).astype(BF16)


def _rope_tables(n_lat, n_ctx):
    rows = n_lat // GRID_W
    row = jnp.repeat(jnp.arange(rows, dtype=F32), GRID_W)
    col = jnp.tile(jnp.arange(GRID_W, dtype=F32), rows)
    pairs = A_HD // 4
    inv = ROPE_THETA ** (-jnp.arange(pairs, dtype=F32) / pairs)
    ang = jnp.concatenate([row[:, None] * inv, col[:, None] * inv], axis=-1)
    c = jnp.cos(ang)
    s = jnp.sin(ang)
    cos = jnp.concatenate([c, c, c, c], axis=-1)
    sin = jnp.concatenate([-s, s, -s, s], axis=-1)
    cos = jnp.concatenate([jnp.ones((n_ctx, LANE), F32), cos], axis=0)
    sin = jnp.concatenate([jnp.zeros((n_ctx, LANE), F32), sin], axis=0)
    return cos, sin


def _proj_attn(xa, mod_i, gain, w_in, q_norm, k_norm, cos, sin):
    nb, ntot, _ = xa.shape
    nt = ntot // TILE
    nq = A_HEADS * A_HD
    nk = A_KV * A_HD
    w = w_in.astype(BF16)
    qg = (jnp.tile(q_norm, A_HEADS) * (A_HD ** -0.5 * LOG2E)).reshape(1, nq)
    kg = jnp.tile(k_norm, A_KV).reshape(1, nk)
    full = lambda a: pl.BlockSpec(a.shape, lambda b, t: (0,) * a.ndim)
    tab = pl.BlockSpec((TILE, LANE), lambda b, t: (t, 0))
    return pl.pallas_call(
        _proj_attn_kernel,
        out_shape=(
            jax.ShapeDtypeStruct((nb, A_HEADS, ntot, A_HD), BF16),
            jax.ShapeDtypeStruct((nb, nk, ntot), BF16),
            jax.ShapeDtypeStruct((nb, A_KV, ntot, LANE), BF16),
        ),
        grid=(nb, nt),
        in_specs=[pl.BlockSpec((1, TILE, D), lambda b, t: (b, t, 0)), pl.BlockSpec((1, 6, D), _mod_index(nb)),
                  full(gain), full(w), full(qg), full(kg), tab, tab],
        out_specs=(pl.BlockSpec((1, A_HEADS, TILE, A_HD), lambda b, t: (b, 0, t, 0)),
                   pl.BlockSpec((1, nk, TILE), lambda b, t: (b, 0, t)),
                   pl.BlockSpec((1, A_KV, TILE, LANE), lambda b, t: (b, 0, t, 0))),
        compiler_params=_cparams(("arbitrary", "arbitrary")),
        name="proj_attn",
    )(xa, mod_i, gain, w, qg, kg, cos, sin)


def _attn_kernel(q_ref, kt_ref, vd_ref, o_ref, *, ntot):
    qi = pl.program_id(2)
    tq = q_ref.shape[2]

    def attend(nk):
        r = []
        for hd in range(A_GRP):
            s = _dot(q_ref[0, hd], kt_ref[0, :, :nk])
            m = jnp.max(s, axis=1, keepdims=True)
            p = jnp.exp2(s - m)
            l = jnp.sum(p, axis=1, keepdims=True)
            r.append(_dot(p.astype(BF16), vd_ref[0, 0, :nk, :]) / l)
        low = _iota((tq, LANE), 1) < A_HD
        t0 = jnp.where(low, r[0], r[1])
        t1 = jnp.where(low, r[2], r[3])
        o_ref[0, 0] = jnp.concatenate([t0, t1], axis=1).astype(BF16)

    @pl.when(qi == 0)
    def _():
        attend(TILE)

    @pl.when(qi > 0)
    def _():
        attend(ntot)


def _attention(q, kt, vd):
    nb, _, ntot, _ = q.shape
    nq = ntot // TILE
    return pl.pallas_call(
        functools.partial(_attn_kernel, ntot=ntot),
        out_shape=jax.ShapeDtypeStruct((nb, A_KV, ntot, A_GRP * A_HD), BF16),
        grid=(nb, A_KV, nq),
        in_specs=[pl.BlockSpec((1, A_GRP, TILE, A_HD), lambda b, g, i: (b, g, i, 0)),
                  pl.BlockSpec((1, A_HD, ntot), lambda b, g, i: (b, g, 0)),
                  pl.BlockSpec((1, 1, ntot, LANE), lambda b, g, i: (b, g, 0, 0))],
        out_specs=pl.BlockSpec((1, 1, TILE, A_GRP * A_HD), lambda b, g, i: (b, g, i, 0)),
        compiler_params=_cparams(("arbitrary", "arbitrary", "arbitrary")),
        name="attention",
    )(q, kt, vd)


def _route_tail(x, y, m, gain_ref, wrt_ref, x1_ref, h2_ref, aff_ref):
    x1 = x + m[2:3] * y
    x1_ref[0] = x1
    h2 = _rms_mod(x1, gain_ref[...], m[3:4], m[4:5])
    h2_ref[0] = h2.astype(BF16)
    logits = _dot3_nt(wrt_ref[...], h2)
    e = jnp.exp(logits - jnp.max(logits, axis=0, keepdims=True))
    aff_ref[0] = e / jnp.sum(e, axis=0, keepdims=True)


def _dot3_nt(a, b):
    ah, al = _split2(a)
    bh, bl = _split2(b)
    return _dot_nt(ah, bh) + (_dot_nt(ah, bl) + _dot_nt(al, bh))


def _readout_mlstm_kernel(x_ref, mod_ref, hf_ref, hb_ref, o_ref, mn_ref, wo_ref, gain_ref, wrt_ref,
                          x1_ref, h2_ref, aff_ref):
    hh = hf_ref[0] + hb_ref[0]
    mn = mn_ref[...]
    parts = []
    for h in range(M_HEADS):
        seg = hh[:, M_DV * h:M_DV * (h + 1)]
        ms = jnp.mean(seg * seg, axis=-1, keepdims=True)
        parts.append(seg * lax.rsqrt(ms + EPS) * mn[:, M_DV * h:M_DV * (h + 1)])
    z = jnp.concatenate(parts, axis=1) * jax.nn.sigmoid(o_ref[0])
    y = _dot(z.astype(BF16), wo_ref[...])
    _route_tail(x_ref[0], y, mod_ref[0], gain_ref, wrt_ref, x1_ref, h2_ref, aff_ref)


def _readout_attn_kernel(x_ref, mod_ref, oa_ref, wo_ref, gain_ref, wrt_ref, x1_ref, h2_ref, aff_ref):
    gw = A_GRP * A_HD
    y = _dot(oa_ref[0, 0], wo_ref[0:gw, :])
    for g in range(1, A_KV):
        y = y + _dot(oa_ref[0, g], wo_ref[gw * g:gw * (g + 1), :])
    _route_tail(x_ref[0], y, mod_ref[0], gain_ref, wrt_ref, x1_ref, h2_ref, aff_ref)


def _readout(xa, mod_i, mixer_out, w_out, gain_ffn, w_router, mlstm_norm=None):
    nb, ntot, _ = xa.shape
    nt = ntot // TILE
    tok = lambda w: pl.BlockSpec((1, TILE, w), lambda b, t: (b, t, 0))
    full = lambda a: pl.BlockSpec(a.shape, lambda b, t: (0,) * a.ndim)
    wo = w_out.astype(BF16)
    wrt = w_router.T
    common_in = [tok(D), pl.BlockSpec((1, 6, D), _mod_index(nb))]
    tail_in = [full(wo), full(gain_ffn), full(wrt)]
    if mlstm_norm is not None:
        hf, hb, og = mixer_out
        kern = _readout_mlstm_kernel
        mid_in = [tok(M_V), tok(M_V), tok(M_V), full(mlstm_norm)]
        args = (xa, mod_i, hf, hb, og, mlstm_norm, wo, gain_ffn, wrt)
    else:
        kern = _readout_attn_kernel
        mid_in = [pl.BlockSpec((1, A_KV, TILE, A_GRP * A_HD), lambda b, t: (b, 0, t, 0))]
        args = (xa, mod_i, mixer_out, wo, gain_ffn, wrt)
    return pl.pallas_call(
        kern,
        out_shape=(jax.ShapeDtypeStruct((nb, ntot, D), F32), jax.ShapeDtypeStruct((nb, ntot, D), BF16),
                   jax.ShapeDtypeStruct((nb, N_EXP, ntot), F32)),
        grid=(nb, nt),
        in_specs=common_in + mid_in + tail_in,
        out_specs=(tok(D), tok(D), pl.BlockSpec((1, N_EXP, TILE), lambda b, t: (b, 0, t))),
        compiler_params=_cparams(("arbitrary", "arbitrary")),
        name="readout_route",
    )(*args)


def _slot_geometry(n_ctx, n_lat):
    cap_ctx = max(1, EC_CAPACITY * n_ctx // N_EXP)
    cap_lat = max(1, EC_CAPACITY * n_lat // N_EXP)
    lat_base = -(-cap_ctx // SUB) * SUB
    slots_max = lat_base + cap_lat + (SUB - 1) * (n_lat // TILE)
    rows_ffn = -(-(slots_max + WIN - SUB) // 16) * 16
    return cap_ctx, cap_lat, lat_base, rows_ffn


def _topk_kernel(a_ref, slot_ref, off_ref, cnt_ref, *, n_ctx, cap_ctx, cap_lat, lat_base):
    bits = lax.bitcast_convert_type(a_ref[0], I32)
    prefix = (_iota((LANE, LANE), 0) <= _iota((LANE, LANE), 1)).astype(BF16)

    def count(mask):
        return jnp.sum(mask.astype(F32), axis=1, keepdims=True)

    def select(x, cap):
        thr = jnp.zeros((N_EXP, 1), I32)
        for bit in range(30, -1, -1):
            cand = thr | (1 << bit)
            thr = jnp.where(count(x >= cand) >= cap, cand, thr)
        gt = x > thr
        eq = x == thr
        need = cap - count(gt)
        run = jnp.zeros((N_EXP, 1), F32)
        blocks = []
        for j in range(x.shape[1] // LANE):
            sl = slice(LANE * j, LANE * (j + 1))
            eqf = eq[:, sl].astype(F32)
            inc = _dot(eqf.astype(BF16), prefix)
            rank = run + inc - eqf
            blocks.append(jnp.logical_or(gt[:, sl], jnp.logical_and(eq[:, sl], rank < need)))
            run = run + inc[:, LANE - 1:LANE]
        return blocks

    blocks = select(bits[:, :n_ctx], cap_ctx) + select(bits[:, n_ctx:], cap_lat)
    per_tile = TILE // LANE
    lane = _iota((N_EXP, LANE), 1)
    off_acc = jnp.zeros((N_EXP, LANE), I32)
    cnt_acc = jnp.zeros((N_EXP, LANE), I32)
    base = jnp.zeros((N_EXP, 1), F32)
    for t in range(len(blocks) // per_tile):
        if t == n_ctx // TILE:
            base = jnp.full((N_EXP, 1), float(lat_base), F32)
        run = jnp.zeros((N_EXP, 1), F32)
        for j in range(per_tile):
            blk = blocks[per_tile * t + j]
            sf = blk.astype(F32)
            inc = _dot(sf.astype(BF16), prefix)
            pos = base + run + inc - sf
            c0 = LANE * (per_tile * t + j)
            slot_ref[0, :, c0:c0 + LANE] = jnp.where(blk, pos.astype(I32), -1)
            run = run + inc[:, LANE - 1:LANE]
        n8 = jnp.floor((run + (SUB - 1)) * (1.0 / SUB)) * SUB
        off_acc = jnp.where(lane == t, base.astype(I32), off_acc)
        cnt_acc = jnp.where(lane == t, n8.astype(I32), cnt_acc)
        base = base + n8
    off_ref[0] = off_acc
    cnt_ref[0] = cnt_acc


def _topk(aff_t, n_ctx):
    nb, _, ntot = aff_t.shape
    cap_ctx, cap_lat, lat_base, _ = _slot_geometry(n_ctx, ntot - n_ctx)
    kern = functools.partial(_topk_kernel, n_ctx=n_ctx, cap_ctx=cap_ctx, cap_lat=cap_lat, lat_base=lat_base)
    return pl.pallas_call(
        kern,
        out_shape=(jax.ShapeDtypeStruct((nb, N_EXP, ntot), I32), jax.ShapeDtypeStruct((nb, N_EXP, LANE), I32),
                   jax.ShapeDtypeStruct((nb, N_EXP, LANE), I32)),
        grid=(nb,),
        in_specs=[pl.BlockSpec((1, N_EXP, ntot), lambda b: (b, 0, 0))],
        out_specs=(pl.BlockSpec((1, N_EXP, ntot), lambda b: (b, 0, 0)),
                   pl.BlockSpec((1, N_EXP, LANE), lambda b: (b, 0, 0)),
                   pl.BlockSpec((1, N_EXP, LANE), lambda b: (b, 0, 0))),
        compiler_params=_cparams(("arbitrary",)),
        name="ec_topk",
    )(aff_t)


def _dispatch_kernel(off_s, rounds_s, h_ref, slot_ref, aff_ref, xg_ref, stage_ref, sem, cnt_ref):
    b = pl.program_id(0)
    t = pl.program_id(1)
    nt = pl.num_programs(1)
    base = (b * nt + t) * N_EXP
    rows = N_EXP * WIN
    a = aff_ref[0]
    src = _iota((LANE, LANE), 0)
    dst = _iota((LANE, LANE), 1)
    a3 = None
    for p, piece in enumerate(_split3(a)):
        sel = jnp.logical_and(dst == 3 * src + p, src < N_EXP).astype(BF16)
        term = _dot(piece, sel)
        a3 = term if a3 is None else a3 + term
    rhs = jnp.concatenate([h_ref[0], a3.astype(BF16)], axis=1)
    expand = ((_iota((rows, LANE), 0) >> 6) == _iota((rows, LANE), 1)).astype(BF16)
    j_row = _iota((rows, 1), 0) & (WIN - 1)
    sl = slot_ref[0]
    e_row = _iota((LANE, 1), 0)
    off_v = jnp.zeros((LANE, 1), I32)
    for e in range(N_EXP):
        off_v = jnp.where(e_row == e, off_s[base + e], off_v)

    @pl.when(jnp.logical_and(b == 0, t == 0))
    def _():
        cnt_ref[0] = 0

    def strip_copies(buf, sample, starts):
        return [pltpu.make_async_copy(stage_ref.at[buf, pl.ds(WIN * e, WIN), :],
                                      xg_ref.at[sample, e, pl.ds(starts[e], WIN), :], sem.at[buf])
                for e in range(N_EXP)]

    def wait_strips(buf):
        for cp in strip_copies(buf, 0, [0] * N_EXP):
            cp.wait()

    def round_body(r, carry):
        n = cnt_ref[0]
        buf = n & 1
        rel = jnp.clip(sl - (off_v + WIN * r), -1, WIN).astype(F32).astype(BF16)
        relx = _dot(expand, rel)
        onehot = (relx == j_row.astype(F32)).astype(BF16)
        stage_ref[buf] = _dot(onehot, rhs)

        @pl.when(n > 0)
        def _():
            wait_strips(1 - buf)

        last_start = xg_ref.shape[2] - WIN
        starts = [pl.multiple_of(jnp.minimum(off_s[base + e] + WIN * r, last_start), SUB) for e in range(N_EXP)]
        for cp in strip_copies(buf, b, starts):
            cp.start()
        cnt_ref[0] = n + 1
        return carry

    lax.fori_loop(0, rounds_s[b * nt + t], round_body, 0)

    @pl.when(jnp.logical_and(b == pl.num_programs(0) - 1, t == nt - 1))
    def _():
        wait_strips((cnt_ref[0] - 1) & 1)


def _dispatch(h2, slot_pad, aff_pad, off_flat, rounds_flat, rows_alloc):
    nb, ntot, _ = h2.shape
    nt = ntot // TILE
    grid_spec = pltpu.PrefetchScalarGridSpec(
        num_scalar_prefetch=2,
        grid=(nb, nt),
        in_specs=[pl.BlockSpec((1, TILE, D), lambda b, t, o, r: (b, t, 0)),
                  pl.BlockSpec((1, LANE, TILE), lambda b, t, o, r: (b, 0, t)),
                  pl.BlockSpec((1, TILE, LANE), lambda b, t, o, r: (b, t, 0))],
        out_specs=pl.BlockSpec(memory_space=pl.ANY),
        scratch_shapes=[pltpu.VMEM((2, N_EXP * WIN, XW), F32), pltpu.SemaphoreType.DMA((2,)),
                        pltpu.SMEM((1,), I32)],
    )
    return pl.pallas_call(
        _dispatch_kernel,
        out_shape=jax.ShapeDtypeStruct((nb, N_EXP, rows_alloc, XW), F32),
        grid_spec=grid_spec,
        compiler_params=_cparams(("arbitrary", "arbitrary")),
        name="ec_dispatch",
    )(off_flat, rounds_flat, h2, slot_pad, aff_pad)


def _ffn_kernel(xg_ref, wg_ref, wu_ref, wd_ref, y_ref):
    e = pl.program_id(0)
    x = xg_ref[0, 0, :, :D].astype(BF16)
    gl = xg_ref[0, 0, :, D:]
    lane = _iota(gl.shape, 1)
    mine = jnp.logical_and(lane >= 3 * e, lane < 3 * e + 3)
    gate = jnp.sum(jnp.where(mine, gl, 0.0), axis=1, keepdims=True)
    a = _dot(x, wg_ref[0])
    u = _dot(x, wu_ref[0])
    hm = (a * jax.nn.sigmoid(a) * u).astype(BF16)
    y_ref[0, 0] = _dot(hm, wd_ref[0]) * gate


def _expert_ffn(xg, wg, wu, wd, rows_ffn):
    nb = xg.shape[0]
    wspec = pl.BlockSpec((1, D, D), lambda e, b: (e, 0, 0))
    return pl.pallas_call(
        _ffn_kernel,
        out_shape=jax.ShapeDtypeStruct((nb, N_EXP, rows_ffn, D), F32),
        grid=(N_EXP, nb),
        in_specs=[pl.BlockSpec((1, 1, rows_ffn, XW), lambda e, b: (b, e, 0, 0)), wspec, wspec, wspec],
        out_specs=pl.BlockSpec((1, 1, rows_ffn, D), lambda e, b: (b, e, 0, 0)),
        compiler_params=_cparams(("arbitrary", "arbitrary")),
        name="ec_ffn",
    )(xg, wg, wu, wd)


def _combine_kernel(off_s, rounds_s, x_ref, mod_ref, slot_ref, y_ref, o_ref, strip_ref, sem):
    b = pl.program_id(0)
    t = pl.program_id(1)
    nt = pl.num_programs(1)
    base = (b * nt + t) * N_EXP
    cols = N_EXP * WIN
    expand = ((_iota((LANE, cols), 1) >> 6) == _iota((LANE, cols), 0)).astype(BF16)
    j_lane = (_iota((1, cols), 1) & (WIN - 1)).astype(F32)
    sl = slot_ref[0]
    e_lane = _iota((1, LANE), 1)
    off_v = jnp.zeros((1, LANE), I32)
    for e in range(N_EXP):
        off_v = jnp.where(e_lane == e, off_s[base + e], off_v)

    step = b * nt + t
    buf = step & 1

    def strip_copies(step_, sample, r, dst):
        return [pltpu.make_async_copy(
            y_ref.at[sample, e, pl.ds(pl.multiple_of(off_s[step_ * N_EXP + e] + WIN * r, SUB), WIN), :],
            strip_ref.at[dst, pl.ds(WIN * e, WIN), :], sem.at[dst]) for e in range(N_EXP)]

    @pl.when(step == 0)
    def _():
        for cp in strip_copies(step, b, 0, buf):
            cp.start()

    @pl.when(step + 1 < pl.num_programs(0) * nt)
    def _():
        for cp in strip_copies(step + 1, jnp.where(t == nt - 1, b + 1, b), 0, 1 - buf):
            cp.start()

    def expand_round(r, acc):
        rel = jnp.clip(sl - (off_v + WIN * r), -1, WIN).astype(F32).astype(BF16)
        relx = _dot(rel, expand)
        onehot = (relx == j_lane).astype(BF16)
        for cp in strip_copies(step, b, r, buf):
            cp.wait()
        hi, lo = _split2(strip_ref[buf])
        return acc + (_dot(onehot, hi) + _dot(onehot, lo))

    def extra_round(r, acc):
        for cp in strip_copies(step, b, r, buf):
            cp.start()
        return expand_round(r, acc)

    acc = expand_round(0, jnp.zeros((TILE, D), F32))
    acc = lax.fori_loop(1, rounds_s[step], extra_round, acc)
    o_ref[0] = x_ref[0] + mod_ref[0][5:6] * acc


def _combine(x1, mod_i, slot_t_pad, y, off_flat, rounds_flat):
    nb, ntot, _ = x1.shape
    nt = ntot // TILE
    grid_spec = pltpu.PrefetchScalarGridSpec(
        num_scalar_prefetch=2,
        grid=(nb, nt),
        in_specs=[pl.BlockSpec((1, TILE, D), lambda b, t, o, r: (b, t, 0)),
                  pl.BlockSpec((1, 6, D), lambda b, t, o, r: (jnp.where(t == 0, nb, b), 0, 0)),
                  pl.BlockSpec((1, TILE, LANE), lambda b, t, o, r: (b, t, 0)),
                  pl.BlockSpec(memory_space=pl.ANY)],
        out_specs=pl.BlockSpec((1, TILE, D), lambda b, t, o, r: (b, t, 0)),
        scratch_shapes=[pltpu.VMEM((2, N_EXP * WIN, D), F32), pltpu.SemaphoreType.DMA((2,))],
    )
    return pl.pallas_call(
        _combine_kernel,
        out_shape=jax.ShapeDtypeStruct((nb, ntot, D), F32),
        grid_spec=grid_spec,
        compiler_params=_cparams(("arbitrary", "arbitrary")),
        name="ec_combine",
    )(off_flat, rounds_flat, x1, mod_i, slot_t_pad, y)


def _final_kernel(x_ref, g_ref, o_ref):
    x = x_ref[0]
    ms = jnp.mean(x * x, axis=-1, keepdims=True)
    o_ref[0] = x * lax.rsqrt(ms + EPS) * g_ref[...]


def _final_norm(xa, gain, n_ctx):
    nb, ntot, _ = xa.shape
    n_lat = ntot - n_ctx
    skip = n_ctx // TILE
    return pl.pallas_call(
        _final_kernel,
        out_shape=jax.ShapeDtypeStruct((nb, n_lat, D), F32),
        grid=(nb, n_lat // TILE),
        in_specs=[pl.BlockSpec((1, TILE, D), lambda b, t: (b, t + skip, 0)),
                  pl.BlockSpec((1, D), lambda b, t: (0, 0))],
        out_specs=pl.BlockSpec((1, TILE, D), lambda b, t: (b, t, 0)),
        compiler_params=_cparams(("arbitrary", "arbitrary")),
        name="final_norm",
    )(xa, gain)


def _moe(x1, h2, aff_t, mod_i, wg, wu, wd, n_ctx):
    nb, ntot, _ = x1.shape
    nt = ntot // TILE
    _, _, _, rows_ffn = _slot_geometry(n_ctx, ntot - n_ctx)
    rows_alloc = rows_ffn + WIN
    slot, off, cnt = _topk(aff_t, n_ctx)
    off_t = jnp.swapaxes(off[:, :, :nt], 1, 2)
    cnt_t = jnp.swapaxes(cnt[:, :, :nt], 1, 2)
    rounds_c = jnp.maximum(1, jnp.max((cnt_t + WIN - 1) // WIN, axis=2))
    fill = jnp.max((rows_alloc - off_t[:, nt - 1, :] + WIN - 1) // WIN, axis=1)
    rounds_d = rounds_c.at[:, nt - 1].max(fill)
    off_flat = off_t.reshape(-1)
    slot_pad = jnp.pad(slot, ((0, 0), (0, LANE - N_EXP), (0, 0)), constant_values=-1)
    slot_t_pad = jnp.swapaxes(slot_pad, 1, 2)
    aff_pad = jnp.pad(jnp.swapaxes(aff_t, 1, 2), ((0, 0), (0, 0), (0, LANE - N_EXP)))
    xg = _dispatch(h2, slot_pad, aff_pad, off_flat, rounds_d.reshape(-1), rows_alloc)
    y = _expert_ffn(xg, wg, wu, wd, rows_ffn)
    return _combine(x1, mod_i, slot_t_pad, y, off_flat, rounds_c.reshape(-1))


def kernel(x, c, ctx, c_ctx, w_mod, b_mod, norm_mix, norm_ffn, mlstm_w_in, mlstm_b_gate, mlstm_norm, mlstm_w_out,
           attn_w_in, attn_q_norm, attn_k_norm, attn_w_out, moe_router, moe_w_gate, moe_w_up, moe_w_down,
           norm_final):
    nb, n_lat, _ = x.shape
    n_ctx = ctx.shape[1]
    depth = w_mod.shape[0]
    assert n_ctx == TILE and n_lat % TILE == 0 and x.shape[2] == D
    xa = jnp.concatenate([ctx, x], axis=1)
    rb = -(-(nb + 1) // SUB) * SUB
    cc = jnp.concatenate([c, c_ctx[None, :], jnp.zeros((rb - nb - 1, D), F32)], axis=0)
    mod = _modulation(cc, w_mod, b_mod)
    cos, sin = _rope_tables(n_lat, n_ctx)
    for i in range(depth):
        j = i // 2
        mod_i = mod[i]
        gain_mix = norm_mix[i].reshape(1, D)
        gain_ffn = norm_ffn[i].reshape(1, D)
        if i % 2 == 0:
            q, kt, v, og, gc, gr = _proj_mlstm(xa, mod_i, gain_mix, mlstm_w_in[j], mlstm_b_gate[j])
            hf, hb = _mlstm_scan(q, kt, v, gc, gr)
            x1, h2, aff_t = _readout(xa, mod_i, (hf, hb, og), mlstm_w_out[j], gain_ffn, moe_router[i],
                                     mlstm_norm=mlstm_norm[j].reshape(1, M_V))
        else:
            q, kt, vd = _proj_attn(xa, mod_i, gain_mix, attn_w_in[j], attn_q_norm[j], attn_k_norm[j], cos, sin)
            oa = _attention(q, kt, vd)
            x1, h2, aff_t = _readout(xa, mod_i, oa, attn_w_out[j], gain_ffn, moe_router[i])
        xa = _moe(x1, h2, aff_t, mod_i, moe_w_gate[i].astype(BF16), moe_w_up[i].astype(BF16),
                  moe_w_down[i].astype(BF16), n_ctx)
    return _final_norm(xa, norm_final.reshape(1, D), n_ctx)
```

```python
import functools

import jax
import jax.numpy as jnp
from jax import lax
from jax.experimental import pallas as pl
from jax.experimental.pallas import tpu as pltpu

F32 = jnp.float32
BF16 = jnp.bfloat16
I32 = jnp.int32

D = 1024
TILE = 256
EPS = 1e-6
DEPTH = 4

M_HEADS = 4
M_DK = 128
M_DV = 256
M_QK = M_HEADS * M_DK
M_V = M_HEADS * M_DV
M_AUG = M_DV + 128
GATE_CAP = 15.0

A_HEADS = 16
A_KV = 4
A_GRP = 4
A_HD = 64
ROPE_THETA = 10000.0
GRID_W = 64
LOG2E = 1.4426950408889634

N_EXP = 16
EC_CAPACITY = 2
WIN = 48
GATE_LANES = 128
XW = D + GATE_LANES

LANE = 128
SUB = 8
VMEM_LIMIT = 56 * 1024 * 1024


def _cparams(sem):
    return pltpu.CompilerParams(dimension_semantics=sem, vmem_limit_bytes=VMEM_LIMIT)


def _dot(a, b):
    return jnp.dot(a, b, preferred_element_type=F32)


def _dot_nt(a, b):
    return lax.dot_general(a, b, (((1,), (1,)), ((), ())), preferred_element_type=F32)


def _split2(x):
    hi = x.astype(BF16)
    lo = (x - hi.astype(F32)).astype(BF16)
    return hi, lo


def _split3(x):
    hi = x.astype(BF16)
    r = x - hi.astype(F32)
    mid = r.astype(BF16)
    lo = (r - mid.astype(F32)).astype(BF16)
    return hi, mid, lo


def _dot3(a, b):
    ah, al = _split2(a)
    bh, bl = _split2(b)
    return _dot(ah, bh) + (_dot(ah, bl) + _dot(al, bh))


def _rms_mod(x, gain, shift, scale):
    ms = jnp.mean(x * x, axis=-1, keepdims=True)
    y = x * lax.rsqrt(ms + EPS) * gain
    return y * (1.0 + scale) + shift


def _iota(shape, dim):
    return lax.broadcasted_iota(I32, shape, dim)


def _mod_index(nb):
    return lambda b, t: (jnp.where(t == 0, nb, b), 0, 0)


def _mod_kernel(c_ref, w_ref, b_ref, o_ref):
    c = c_ref[...]
    s = c * jax.nn.sigmoid(c)
    o_ref[...] = _dot3(s, w_ref[...]) + b_ref[...]


def _modulation(cc, w_mod, b_mod):
    depth, _, n6 = w_mod.shape
    rb = cc.shape[0]
    nj = n6 // D
    out = pl.pallas_call(
        _mod_kernel,
        out_shape=jax.ShapeDtypeStruct((depth, rb, n6), F32),
        grid=(depth, nj),
        in_specs=[
            pl.BlockSpec((rb, D), lambda i, j: (0, 0)),
            pl.BlockSpec((None, D, D), lambda i, j: (i, 0, j)),
            pl.BlockSpec((None, 1, D), lambda i, j: (i, 0, j)),
        ],
        out_specs=pl.BlockSpec((None, rb, D), lambda i, j: (i, 0, j)),
        compiler_params=_cparams(("arbitrary", "arbitrary")),
        name="adaln_mod",
    )(cc, w_mod, b_mod.reshape(depth, 1, n6))
    return out.reshape(depth, rb, nj, D)


def _gate_act(g, idx):
    g = GATE_CAP * jnp.tanh(g * (1.0 / GATE_CAP))
    logsig = jnp.minimum(g, 0.0) - jnp.log(1.0 + jnp.exp(-jnp.abs(g)))
    is_forget = ((idx >> 2) & 1) == 1
    return jnp.where(is_forget, logsig, g)


def _proj_mlstm_kernel(x_ref, mod_ref, gain_ref, w_ref, wkt_ref, wg_ref, wgt_ref, bc_ref, br_ref,
                       q_ref, kt_ref, v_ref, o_ref, gc_ref, gr_ref):
    m = mod_ref[0]
    h = _rms_mod(x_ref[0], gain_ref[...], m[0:1], m[1:2])
    hb = h.astype(BF16)
    r = _dot(hb, w_ref[...])
    q_ref[0] = (r[:, :M_QK] * (M_DK ** -0.5)).astype(BF16)
    v_ref[0] = r[:, M_QK:M_QK + M_V].astype(BF16)
    o_ref[0] = r[:, M_QK + M_V:]
    kt_ref[0] = _dot_nt(wkt_ref[...], hb).astype(BF16)
    gc = _dot(hb, wg_ref[...]) + bc_ref[...]
    gc_ref[0] = _gate_act(gc, _iota(gc.shape, 1))
    gr = _dot_nt(wgt_ref[...], hb) + br_ref[...]
    gr_ref[0] = _gate_act(gr, _iota(gr.shape, 0))


def _proj_mlstm(xa, mod_i, gain, w_in, b_gate):
    nb, ntot, _ = xa.shape
    nt = ntot // TILE
    n_g = 4 * M_HEADS
    wq = w_in[:, :M_QK]
    wk = w_in[:, M_QK:2 * M_QK]
    wvo = w_in[:, 2 * M_QK:2 * M_QK + 2 * M_V]
    wg = w_in[:, 2 * M_QK + 2 * M_V:]
    w_main = jnp.concatenate([wq, wvo], axis=1).astype(BF16)
    wkt = wk.T.astype(BF16)
    wg_pad = jnp.pad(wg, ((0, 0), (0, LANE - n_g))).astype(BF16)
    wgt = wg.T.astype(BF16)
    bc = jnp.pad(b_gate, (0, LANE - n_g)).reshape(1, LANE)
    br = b_gate.reshape(n_g, 1)
    tok = lambda w: pl.BlockSpec((1, TILE, w), lambda b, t: (b, t, 0))
    full = lambda a: pl.BlockSpec(a.shape, lambda b, t: (0,) * a.ndim)
    return pl.pallas_call(
        _proj_mlstm_kernel,
        out_shape=(
            jax.ShapeDtypeStruct((nb, ntot, M_QK), BF16),
            jax.ShapeDtypeStruct((nb, M_QK, ntot), BF16),
            jax.ShapeDtypeStruct((nb, ntot, M_V), BF16),
            jax.ShapeDtypeStruct((nb, ntot, M_V), F32),
            jax.ShapeDtypeStruct((nb, ntot, LANE), F32),
            jax.ShapeDtypeStruct((nb, n_g, ntot), F32),
        ),
        grid=(nb, nt),
        in_specs=[tok(D), pl.BlockSpec((1, 6, D), _mod_index(nb)), full(gain), full(w_main), full(wkt),
                  full(wg_pad), full(wgt), full(bc), full(br)],
        out_specs=(tok(M_QK), pl.BlockSpec((1, M_QK, TILE), lambda b, t: (b, 0, t)), tok(M_V), tok(M_V),
                   tok(LANE), pl.BlockSpec((1, n_g, TILE), lambda b, t: (b, 0, t))),
        compiler_params=_cparams(("arbitrary", "arbitrary")),
        name="proj_mlstm",
    )(xa, mod_i, gain, w_main, wkt, wg_pad, wgt, bc, br)


def _mlstm_kernel(qf_ref, qb_ref, kf_ref, kb_ref, vf_ref, vb_ref, gcf_ref, gcb_ref, grf_ref, grb_ref,
                  hf_ref, hb_ref, c_ref, m_ref):
    t = pl.program_id(1)

    @pl.when(t == 0)
    def _():
        c_ref[...] = jnp.zeros(c_ref.shape, F32)
        m_ref[...] = jnp.zeros(m_ref.shape, F32)

    n = TILE
    row = _iota((n, n), 0)
    col = _iota((n, n), 1)
    lower = col <= row
    upper = col >= row
    lower_b = lower.astype(BF16)
    upper_b = upper.astype(BF16)
    ones_col = (_iota((n, M_AUG - M_DV), 1) == 0).astype(BF16)
    dirs = ((qf_ref, kf_ref, vf_ref, gcf_ref, grf_ref, hf_ref, lower, lower_b, upper_b, n - 1),
            (qb_ref, kb_ref, vb_ref, gcb_ref, grb_ref, hb_ref, upper, upper_b, lower_b, 0))
    for d, (q_ref, k_ref, v_ref, gc_ref, gr_ref, o_ref, mask, cum_l, cum_r, last) in enumerate(dirs):
        gc = gc_ref[0]
        gr = gr_ref[0]
        bcol = sum(_dot(cum_l, p) for p in _split3(gc))
        brow = sum(_dot(p, cum_r) for p in _split3(gr))
        for h in range(M_HEADS):
            gi = 8 * d + h
            gf = gi + 4
            sidx = 4 * d + h
            b_col = bcol[:, gf:gf + 1]
            b_row = brow[gf:gf + 1, :]
            ig_row = gr[gi:gi + 1, :]
            total = b_row[:, last:last + 1]
            m_st = m_ref[sidx][0:1, 0:1]
            log_d = jnp.where(mask, b_col - b_row + ig_row, -jnp.inf)
            m_inter = b_col + m_st
            m_q = jnp.maximum(m_inter, jnp.max(log_d, axis=1, keepdims=True))
            w_intra = jnp.exp(log_d - m_q)
            w_inter = jnp.exp(m_inter - m_q)
            qh = q_ref[0, :, M_DK * h:M_DK * (h + 1)]
            kth = k_ref[0, M_DK * h:M_DK * (h + 1), :]
            v_aug = jnp.concatenate([v_ref[0, :, M_DV * h:M_DV * (h + 1)], ones_col], axis=1)
            c_aug = c_ref[sidx]
            s = (_dot(qh, kth) * w_intra).astype(BF16)
            nd = _dot(s, v_aug) + w_inter * _dot(qh, c_aug.astype(BF16))
            den = nd[:, M_DV:M_DV + 1]
            o_ref[0, :, M_DV * h:M_DV * (h + 1)] = nd[:, :M_DV] / jnp.maximum(jnp.abs(den), jnp.exp(-m_q))
            log_w = total - b_row + ig_row
            m_new = jnp.maximum(total + m_st, jnp.max(log_w, axis=1, keepdims=True))
            w_key = jnp.exp(log_w - m_new)
            decay = jnp.exp(total + m_st - m_new)
            kw = (kth.astype(F32) * w_key).astype(BF16)
            c_ref[sidx] = decay * c_aug + _dot(kw, v_aug)
            m_ref[sidx] = jnp.broadcast_to(m_new, m_ref.shape[1:])


def _mlstm_scan(q, kt, v, gc, gr):
    nb, ntot, _ = q.shape
    nt = ntot // TILE
    fwd = lambda b, t: (b, t, 0)
    bwd = lambda b, t: (b, jnp.where(t == 0, 0, nt - t), 0)
    fwd_t = lambda b, t: (b, 0, t)
    bwd_t = lambda b, t: (b, 0, jnp.where(t == 0, 0, nt - t))
    n_g = gr.shape[1]
    return pl.pallas_call(
        _mlstm_kernel,
        out_shape=(jax.ShapeDtypeStruct((nb, ntot, M_V), F32), jax.ShapeDtypeStruct((nb, ntot, M_V), F32)),
        grid=(nb, nt),
        in_specs=[
            pl.BlockSpec((1, TILE, M_QK), fwd), pl.BlockSpec((1, TILE, M_QK), bwd),
            pl.BlockSpec((1, M_QK, TILE), fwd_t), pl.BlockSpec((1, M_QK, TILE), bwd_t),
            pl.BlockSpec((1, TILE, M_V), fwd), pl.BlockSpec((1, TILE, M_V), bwd),
            pl.BlockSpec((1, TILE, LANE), fwd), pl.BlockSpec((1, TILE, LANE), bwd),
            pl.BlockSpec((1, n_g, TILE), fwd_t), pl.BlockSpec((1, n_g, TILE), bwd_t),
        ],
        out_specs=(pl.BlockSpec((1, TILE, M_V), fwd), pl.BlockSpec((1, TILE, M_V), bwd)),
        scratch_shapes=[pltpu.VMEM((2 * M_HEADS, M_DK, M_AUG), F32), pltpu.VMEM((2 * M_HEADS, SUB, LANE), F32)],
        compiler_params=_cparams(("arbitrary", "arbitrary")),
        name="mlstm_scan",
    )(q, q, kt, kt, v, v, gc, gc, gr, gr)


def _head_norm(x, gain):
    w = x.shape[1]
    gsum = ((_iota((w, LANE), 0) >> 6) == _iota((w, LANE), 1)).astype(BF16)
    gexp = ((_iota((LANE, w), 1) >> 6) == _iota((LANE, w), 0)).astype(BF16)
    hi, lo = _split2(x * x)
    ssum = _dot(hi, gsum) + _dot(lo, gsum)
    rh, rl = _split2(lax.rsqrt(ssum * (1.0 / A_HD) + EPS))
    return x * (_dot(rh, gexp) + _dot(rl, gexp)) * gain


def _rope(x, cos, sin_signed):
    first = (_iota((x.shape[0], LANE), 1) & (A_HD - 1)) < A_HD // 2
    tiles = []
    for i in range(x.shape[1] // LANE):
        xt = x[:, LANE * i:LANE * (i + 1)]
        partner = jnp.where(first, pltpu.roll(xt, LANE - A_HD // 2, 1), pltpu.roll(xt, A_HD // 2, 1))
        tiles.append(xt * cos + partner * sin_signed)
    return jnp.concatenate(tiles, axis=1)


def _proj_attn_kernel(x_ref, mod_ref, gain_ref, w_ref, qg_ref, kg_ref, cos_ref, sin_ref, shift_ref,
                      q_ref, kt_ref, vd_ref):
    m = mod_ref[0]
    h = _rms_mod(x_ref[0], gain_ref[...], m[0:1], m[1:2])
    r = _dot(h.astype(BF16), w_ref[...])
    nq = A_HEADS * A_HD
    nk = A_KV * A_HD
    cos = cos_ref[...]
    sin = sin_ref[...]
    qn = _rope(_head_norm(r[:, :nq], qg_ref[...]), cos, sin)
    kn = _rope(_head_norm(r[:, nq:nq + nk], kg_ref[...]), cos, sin)
    v = r[:, nq + nk:]
    lane = _iota((TILE, LANE), 1)
    low = lane < A_HD
    one_hot = jnp.where(lane == A_HD, 1.0, 0.0)

    def head_tile(x, idx):
        tile = x[:, LANE * (idx // 2):LANE * (idx // 2 + 1)]
        if idx % 2 == 1:
            tile = pltpu.roll(tile, A_HD, 1)
        return jnp.where(low, tile, one_hot).astype(BF16)

    for hd in range(A_HEADS):
        q_ref[0, hd] = head_tile(qn, hd)
    for g in range(A_KV):
        vd_ref[0, g] = head_tile(v, g)
    kt = jnp.transpose(kn)
    extra = jnp.where(_iota((LANE - A_HD, TILE), 0) == 0, shift_ref[...], 0.0).astype(BF16)
    for g in range(A_KV):
        kt_ref[0, g, 0:A_HD, :] = kt[A_HD * g:A_HD * (g + 1), :].astype(BF16)
        kt_ref[0, g, A_HD:LANE, :] = extra


def _rope_tables(n_lat, n_ctx):
    rows = n_lat // GRID_W
    row = jnp.repeat(jnp.arange(rows, dtype=F32), GRID_W)
    col = jnp.tile(jnp.arange(GRID_W, dtype=F32), rows)
    pairs = A_HD // 4
    inv = ROPE_THETA ** (-jnp.arange(pairs, dtype=F32) / pairs)
    ang = jnp.concatenate([row[:, None] * inv, col[:, None] * inv], axis=-1)
    c = jnp.cos(ang)
    s = jnp.sin(ang)
    cos = jnp.concatenate([c, c, c, c], axis=-1)
    sin = jnp.concatenate([-s, s, -s, s], axis=-1)
    cos = jnp.concatenate([jnp.ones((n_ctx, LANE), F32), cos], axis=0)
    sin = jnp.concatenate([jnp.zeros((n_ctx, LANE), F32), sin], axis=0)
    return cos, sin


SHIFT_LIMIT = 60.0


def _softmax_shift(qg, kg):
    bound = A_HD * jnp.max(jnp.abs(qg)) * jnp.max(jnp.abs(kg))
    fast = bound <= SHIFT_LIMIT
    shift = jnp.where(fast, jnp.ceil(bound), 0.0)
    return shift, jnp.logical_not(fast).astype(I32)


def _proj_attn(xa, mod_i, gain, w_in, q_norm, k_norm, cos, sin):
    nb, ntot, _ = xa.shape
    nt = ntot // TILE
    nq = A_HEADS * A_HD
    nk = A_KV * A_HD
    w = w_in.astype(BF16)
    qg = (jnp.tile(q_norm, A_HEADS) * (A_HD ** -0.5 * LOG2E)).reshape(1, nq)
    kg = jnp.tile(k_norm, A_KV).reshape(1, nk)
    shift, flag = _softmax_shift(qg, kg)
    neg_shift = (-shift).reshape(1, 1).astype(F32)
    full = lambda a: pl.BlockSpec(a.shape, lambda b, t: (0,) * a.ndim)
    tab = pl.BlockSpec((TILE, LANE), lambda b, t: (t, 0))
    q, kt, vd = pl.pallas_call(
        _proj_attn_kernel,
        out_shape=(
            jax.ShapeDtypeStruct((nb, A_HEADS, ntot, LANE), BF16),
            jax.ShapeDtypeStruct((nb, A_KV, LANE, ntot), BF16),
            jax.ShapeDtypeStruct((nb, A_KV, ntot, LANE), BF16),
        ),
        grid=(nb, nt),
        in_specs=[pl.BlockSpec((1, TILE, D), lambda b, t: (b, t, 0)), pl.BlockSpec((1, 6, D), _mod_index(nb)),
                  full(gain), full(w), full(qg), full(kg), tab, tab, full(neg_shift)],
        out_specs=(pl.BlockSpec((1, A_HEADS, TILE, LANE), lambda b, t: (b, 0, t, 0)),
                   pl.BlockSpec((1, A_KV, LANE, TILE), lambda b, t: (b, 0, 0, t)),
                   pl.BlockSpec((1, A_KV, TILE, LANE), lambda b, t: (b, 0, t, 0))),
        compiler_params=_cparams(("arbitrary", "arbitrary")),
        name="proj_attn",
    )(xa, mod_i, gain, w, qg, kg, cos, sin, neg_shift)
    return q, kt, vd, flag.reshape(1)


def _attn_kernel(flag_ref, q_ref, kt_ref, vd_ref, o_ref, *, ntot):
    qi = pl.program_id(2)
    tq = q_ref.shape[2]
    low = _iota((tq, LANE), 1) < A_HD

    def attend(nk, row_max):
        outs = []
        for hd in range(A_GRP):
            s = _dot(q_ref[0, hd], kt_ref[0, 0, :, :nk])
            if row_max:
                s = s - jnp.max(s, axis=1, keepdims=True)
            r = _dot(jnp.exp2(s).astype(BF16), vd_ref[0, 0, :nk, :])
            outs.append(r / r[:, A_HD:A_HD + 1])
        t0 = jnp.where(low, outs[0], pltpu.roll(outs[1], A_HD, 1))
        t1 = jnp.where(low, outs[2], pltpu.roll(outs[3], A_HD, 1))
        o_ref[0, 0] = jnp.concatenate([t0, t1], axis=1).astype(BF16)

    @pl.when(qi == 0)
    def _():
        attend(TILE, True)

    @pl.when(jnp.logical_and(qi > 0, flag_ref[0] == 0))
    def _():
        attend(ntot, False)

    @pl.when(jnp.logical_and(qi > 0, flag_ref[0] != 0))
    def _():
        attend(ntot, True)


def _attention(q, kt, vd, flag):
    nb, _, ntot, _ = q.shape
    nq = ntot // TILE
    grid_spec = pltpu.PrefetchScalarGridSpec(
        num_scalar_prefetch=1,
        grid=(nb, A_KV, nq),
        in_specs=[pl.BlockSpec((1, A_GRP, TILE, LANE), lambda b, g, i, f: (b, g, i, 0)),
                  pl.BlockSpec((1, 1, LANE, ntot), lambda b, g, i, f: (b, g, 0, 0)),
                  pl.BlockSpec((1, 1, ntot, LANE), lambda b, g, i, f: (b, g, 0, 0))],
        out_specs=pl.BlockSpec((1, 1, TILE, A_GRP * A_HD), lambda b, g, i, f: (b, g, i, 0)),
    )
    return pl.pallas_call(
        functools.partial(_attn_kernel, ntot=ntot),
        out_shape=jax.ShapeDtypeStruct((nb, A_KV, ntot, A_GRP * A_HD), BF16),
        grid_spec=grid_spec,
        compiler_params=_cparams(("arbitrary", "arbitrary", "arbitrary")),
        name="attention",
    )(flag, q, kt, vd)


def _route_tail(x, y, m, gain_ref, wrt_ref, x1_ref, h2_ref, aff_ref):
    x1 = x + m[2:3] * y
    x1_ref[0] = x1
    h2 = _rms_mod(x1, gain_ref[...], m[3:4], m[4:5])
    h2_ref[0] = h2.astype(BF16)
    logits = _dot3_nt(wrt_ref[...], h2)
    e = jnp.exp(logits - jnp.max(logits, axis=0, keepdims=True))
    aff_ref[0] = e / jnp.sum(e, axis=0, keepdims=True)


def _dot3_nt(a, b):
    ah, al = _split2(a)
    bh, bl = _split2(b)
    return _dot_nt(ah, bh) + (_dot_nt(ah, bl) + _dot_nt(al, bh))


def _readout_mlstm_kernel(x_ref, mod_ref, hf_ref, hb_ref, o_ref, mn_ref, wo_ref, gain_ref, wrt_ref,
                          x1_ref, h2_ref, aff_ref):
    hh = hf_ref[0] + hb_ref[0]
    mn = mn_ref[...]
    parts = []
    for h in range(M_HEADS):
        seg = hh[:, M_DV * h:M_DV * (h + 1)]
        ms = jnp.mean(seg * seg, axis=-1, keepdims=True)
        parts.append(seg * lax.rsqrt(ms + EPS) * mn[:, M_DV * h:M_DV * (h + 1)])
    z = jnp.concatenate(parts, axis=1) * jax.nn.sigmoid(o_ref[0])
    y = _dot(z.astype(BF16), wo_ref[...])
    _route_tail(x_ref[0], y, mod_ref[0], gain_ref, wrt_ref, x1_ref, h2_ref, aff_ref)


def _readout_attn_kernel(x_ref, mod_ref, oa_ref, wo_ref, gain_ref, wrt_ref, x1_ref, h2_ref, aff_ref):
    gw = A_GRP * A_HD
    y = _dot(oa_ref[0, 0], wo_ref[0:gw, :])
    for g in range(1, A_KV):
        y = y + _dot(oa_ref[0, g], wo_ref[gw * g:gw * (g + 1), :])
    _route_tail(x_ref[0], y, mod_ref[0], gain_ref, wrt_ref, x1_ref, h2_ref, aff_ref)


def _readout(xa, mod_i, mixer_out, w_out, gain_ffn, w_router, mlstm_norm=None):
    nb, ntot, _ = xa.shape
    nt = ntot // TILE
    tok = lambda w: pl.BlockSpec((1, TILE, w), lambda b, t: (b, t, 0))
    full = lambda a: pl.BlockSpec(a.shape, lambda b, t: (0,) * a.ndim)
    wo = w_out.astype(BF16)
    wrt = w_router.T
    common_in = [tok(D), pl.BlockSpec((1, 6, D), _mod_index(nb))]
    tail_in = [full(wo), full(gain_ffn), full(wrt)]
    if mlstm_norm is not None:
        hf, hb, og = mixer_out
        kern = _readout_mlstm_kernel
        mid_in = [tok(M_V), tok(M_V), tok(M_V), full(mlstm_norm)]
        args = (xa, mod_i, hf, hb, og, mlstm_norm, wo, gain_ffn, wrt)
    else:
        kern = _readout_attn_kernel
        mid_in = [pl.BlockSpec((1, A_KV, TILE, A_GRP * A_HD), lambda b, t: (b, 0, t, 0))]
        args = (xa, mod_i, mixer_out, wo, gain_ffn, wrt)
    return pl.pallas_call(
        kern,
        out_shape=(jax.ShapeDtypeStruct((nb, ntot, D), F32), jax.ShapeDtypeStruct((nb, ntot, D), BF16),
                   jax.ShapeDtypeStruct((nb, N_EXP, ntot), F32)),
        grid=(nb, nt),
        in_specs=common_in + mid_in + tail_in,
        out_specs=(tok(D), tok(D), pl.BlockSpec((1, N_EXP, TILE), lambda b, t: (b, 0, t))),
        compiler_params=_cparams(("arbitrary", "arbitrary")),
        name="readout_route",
    )(*args)


def _slot_geometry(n_ctx, n_lat):
    cap_ctx = max(1, EC_CAPACITY * n_ctx // N_EXP)
    cap_lat = max(1, EC_CAPACITY * n_lat // N_EXP)
    lat_base = -(-cap_ctx // SUB) * SUB
    slots_max = lat_base + cap_lat + (SUB - 1) * (n_lat // TILE)
    rows_ffn = -(-slots_max // 16) * 16
    return cap_ctx, cap_lat, lat_base, rows_ffn


def _topk_kernel(a_ref, slot_ref, off_ref, cnt_ref, *, n_ctx, cap_ctx, cap_lat, lat_base):
    bits = lax.bitcast_convert_type(a_ref[0], I32)
    prefix = (_iota((LANE, LANE), 0) <= _iota((LANE, LANE), 1)).astype(BF16)

    def count(mask):
        return jnp.sum(mask.astype(F32), axis=1, keepdims=True)

    def select(x, cap):
        thr = jnp.zeros((N_EXP, 1), I32)
        for bit in range(30, -1, -1):
            cand = thr | (1 << bit)
            thr = jnp.where(count(x >= cand) >= cap, cand, thr)
        gt = x > thr
        eq = x == thr
        need = cap - count(gt)
        run = jnp.zeros((N_EXP, 1), F32)
        blocks = []
        for j in range(x.shape[1] // LANE):
            sl = slice(LANE * j, LANE * (j + 1))
            eqf = eq[:, sl].astype(F32)
            inc = _dot(eqf.astype(BF16), prefix)
            rank = run + inc - eqf
            blocks.append(jnp.logical_or(gt[:, sl], jnp.logical_and(eq[:, sl], rank < need)))
            run = run + inc[:, LANE - 1:LANE]
        return blocks

    blocks = select(bits[:, :n_ctx], cap_ctx) + select(bits[:, n_ctx:], cap_lat)
    per_tile = TILE // LANE
    lane = _iota((N_EXP, LANE), 1)
    off_acc = jnp.zeros((N_EXP, LANE), I32)
    cnt_acc = jnp.zeros((N_EXP, LANE), I32)
    base = jnp.zeros((N_EXP, 1), F32)
    for t in range(len(blocks) // per_tile):
        if t == n_ctx // TILE:
            base = jnp.full((N_EXP, 1), float(lat_base), F32)
        run = jnp.zeros((N_EXP, 1), F32)
        for j in range(per_tile):
            blk = blocks[per_tile * t + j]
            sf = blk.astype(F32)
            inc = _dot(sf.astype(BF16), prefix)
            pos = base + run + inc - sf
            c0 = LANE * (per_tile * t + j)
            slot_ref[0, :, c0:c0 + LANE] = jnp.where(blk, pos.astype(I32), -1)
            run = run + inc[:, LANE - 1:LANE]
        n8 = jnp.floor((run + (SUB - 1)) * (1.0 / SUB)) * SUB
        off_acc = jnp.where(lane == t, base.astype(I32), off_acc)
        cnt_acc = jnp.where(lane == t, n8.astype(I32), cnt_acc)
        base = base + n8
    off_ref[0] = off_acc
    cnt_ref[0] = cnt_acc


def _topk(aff_t, n_ctx):
    nb, _, ntot = aff_t.shape
    cap_ctx, cap_lat, lat_base, _ = _slot_geometry(n_ctx, ntot - n_ctx)
    kern = functools.partial(_topk_kernel, n_ctx=n_ctx, cap_ctx=cap_ctx, cap_lat=cap_lat, lat_base=lat_base)
    return pl.pallas_call(
        kern,
        out_shape=(jax.ShapeDtypeStruct((nb, N_EXP, ntot), I32), jax.ShapeDtypeStruct((nb, N_EXP, LANE), I32),
                   jax.ShapeDtypeStruct((nb, N_EXP, LANE), I32)),
        grid=(nb,),
        in_specs=[pl.BlockSpec((1, N_EXP, ntot), lambda b: (b, 0, 0))],
        out_specs=(pl.BlockSpec((1, N_EXP, ntot), lambda b: (b, 0, 0)),
                   pl.BlockSpec((1, N_EXP, LANE), lambda b: (b, 0, 0)),
                   pl.BlockSpec((1, N_EXP, LANE), lambda b: (b, 0, 0))),
        compiler_params=_cparams(("arbitrary",)),
        name="ec_topk",
    )(aff_t)


def _strip_index(shape, dim):
    i = _iota(shape, dim)
    e = jnp.floor((i.astype(F32) + 0.5) * (1.0 / WIN)).astype(I32)
    return e, i - WIN * e


def _dispatch_kernel(off_s, rounds_s, h_ref, slot_ref, aff_ref, xg_ref, stage_ref, sem, cnt_ref):
    b = pl.program_id(0)
    t = pl.program_id(1)
    nt = pl.num_programs(1)
    base = (b * nt + t) * N_EXP
    rows = N_EXP * WIN
    a = aff_ref[0]
    src = _iota((LANE, LANE), 0)
    dst = _iota((LANE, LANE), 1)
    a3 = None
    for p, piece in enumerate(_split3(a)):
        sel = jnp.logical_and(dst == 3 * src + p, src < N_EXP).astype(BF16)
        term = _dot(piece, sel)
        a3 = term if a3 is None else a3 + term
    rhs = jnp.concatenate([h_ref[0], a3.astype(BF16)], axis=1)
    e_of_row, _ = _strip_index((rows, LANE), 0)
    expand = (e_of_row == _iota((rows, LANE), 1)).astype(BF16)
    _, j_row = _strip_index((rows, 1), 0)
    sl = slot_ref[0]
    e_row = _iota((LANE, 1), 0)
    off_v = jnp.zeros((LANE, 1), I32)
    for e in range(N_EXP):
        off_v = jnp.where(e_row == e, off_s[base + e], off_v)

    @pl.when(jnp.logical_and(b == 0, t == 0))
    def _():
        cnt_ref[0] = 0

    def strip_copies(buf, sample, starts):
        return [pltpu.make_async_copy(stage_ref.at[buf, pl.ds(WIN * e, WIN), :],
                                      xg_ref.at[sample, e, pl.ds(starts[e], WIN), :], sem.at[buf])
                for e in range(N_EXP)]

    def wait_strips(buf):
        for cp in strip_copies(buf, 0, [0] * N_EXP):
            cp.wait()

    def round_body(r, carry):
        n = cnt_ref[0]
        buf = n & 1
        rel = jnp.clip(sl - (off_v + WIN * r), -1, WIN).astype(F32).astype(BF16)
        relx = _dot(expand, rel)
        onehot = (relx == j_row.astype(F32)).astype(BF16)
        stage_ref[buf] = _dot(onehot, rhs)

        @pl.when(n > 0)
        def _():
            wait_strips(1 - buf)

        last_start = xg_ref.shape[2] - WIN
        starts = [pl.multiple_of(jnp.minimum(off_s[base + e] + WIN * r, last_start), SUB) for e in range(N_EXP)]
        for cp in strip_copies(buf, b, starts):
            cp.start()
        cnt_ref[0] = n + 1
        return carry

    lax.fori_loop(0, rounds_s[b * nt + t], round_body, 0)

    @pl.when(jnp.logical_and(b == pl.num_programs(0) - 1, t == nt - 1))
    def _():
        wait_strips((cnt_ref[0] - 1) & 1)


def _dispatch(h2, slot_pad, aff_pad, off_flat, rounds_flat, rows_alloc):
    nb, ntot, _ = h2.shape
    nt = ntot // TILE
    grid_spec = pltpu.PrefetchScalarGridSpec(
        num_scalar_prefetch=2,
        grid=(nb, nt),
        in_specs=[pl.BlockSpec((1, TILE, D), lambda b, t, o, r: (b, t, 0)),
                  pl.BlockSpec((1, LANE, TILE), lambda b, t, o, r: (b, 0, t)),
                  pl.BlockSpec((1, TILE, LANE), lambda b, t, o, r: (b, t, 0))],
        out_specs=pl.BlockSpec(memory_space=pl.ANY),
        scratch_shapes=[pltpu.VMEM((2, N_EXP * WIN, XW), F32), pltpu.SemaphoreType.DMA((2,)),
                        pltpu.SMEM((1,), I32)],
    )
    return pl.pallas_call(
        _dispatch_kernel,
        out_shape=jax.ShapeDtypeStruct((nb, N_EXP, rows_alloc, XW), F32),
        grid_spec=grid_spec,
        compiler_params=_cparams(("arbitrary", "arbitrary")),
        name="ec_dispatch",
    )(off_flat, rounds_flat, h2, slot_pad, aff_pad)


def _ffn_kernel(xg_ref, wg_ref, wu_ref, wd_ref, y_ref):
    e = pl.program_id(0)
    x = xg_ref[0, 0, :, :D].astype(BF16)
    gl = xg_ref[0, 0, :, D:]
    lane = _iota(gl.shape, 1)
    mine = jnp.logical_and(lane >= 3 * e, lane < 3 * e + 3)
    gate = jnp.sum(jnp.where(mine, gl, 0.0), axis=1, keepdims=True)
    a = _dot(x, wg_ref[0])
    u = _dot(x, wu_ref[0])
    hm = (a * jax.nn.sigmoid(a) * u).astype(BF16)
    y_ref[0, 0] = _dot(hm, wd_ref[0]) * gate


def _expert_ffn(xg, wg, wu, wd, rows_ffn):
    nb = xg.shape[0]
    wspec = pl.BlockSpec((1, D, D), lambda e, b: (e, 0, 0))
    return pl.pallas_call(
        _ffn_kernel,
        out_shape=jax.ShapeDtypeStruct((nb, N_EXP, rows_ffn, D), F32),
        grid=(N_EXP, nb),
        in_specs=[pl.BlockSpec((1, 1, rows_ffn, XW), lambda e, b: (b, e, 0, 0)), wspec, wspec, wspec],
        out_specs=pl.BlockSpec((1, 1, rows_ffn, D), lambda e, b: (b, e, 0, 0)),
        compiler_params=_cparams(("arbitrary", "arbitrary")),
        name="ec_ffn",
    )(xg, wg, wu, wd)


def _combine_kernel(off_s, rounds_s, x_ref, mod_ref, slot_ref, y_ref, o_ref, strip_ref, sem):
    b = pl.program_id(0)
    t = pl.program_id(1)
    nt = pl.num_programs(1)
    base = (b * nt + t) * N_EXP
    cols = N_EXP * WIN
    e_of_col, _ = _strip_index((LANE, cols), 1)
    expand = (e_of_col == _iota((LANE, cols), 0)).astype(BF16)
    _, j_lane = _strip_index((1, cols), 1)
    j_lane = j_lane.astype(F32)
    sl = slot_ref[0]
    e_lane = _iota((1, LANE), 1)
    last_start = y_ref.shape[2] - WIN

    def strip_start(step_, e, r):
        return jnp.minimum(off_s[step_ * N_EXP + e] + WIN * r, last_start)

    step = b * nt + t
    buf = step & 1

    def strip_copies(step_, sample, r, dst):
        return [pltpu.make_async_copy(
            y_ref.at[sample, e, pl.ds(pl.multiple_of(strip_start(step_, e, r), SUB), WIN), :],
            strip_ref.at[dst, pl.ds(WIN * e, WIN), :], sem.at[dst]) for e in range(N_EXP)]

    @pl.when(step == 0)
    def _():
        for cp in strip_copies(step, b, 0, buf):
            cp.start()

    @pl.when(step + 1 < pl.num_programs(0) * nt)
    def _():
        for cp in strip_copies(step + 1, jnp.where(t == nt - 1, b + 1, b), 0, 1 - buf):
            cp.start()

    def expand_round(r, acc):
        first_v = jnp.zeros((1, LANE), I32)
        start_v = jnp.zeros((1, LANE), I32)
        for e in range(N_EXP):
            first_v = jnp.where(e_lane == e, off_s[base + e] + WIN * r, first_v)
            start_v = jnp.where(e_lane == e, strip_start(step, e, r), start_v)
        nominal = sl - first_v
        in_round = jnp.logical_and(nominal >= 0, nominal < WIN)
        rel = jnp.where(in_round, sl - start_v, -1).astype(F32).astype(BF16)
        relx = _dot(rel, expand)
        onehot = (relx == j_lane).astype(BF16)
        for cp in strip_copies(step, b, r, buf):
            cp.wait()
        hi, lo = _split2(strip_ref[buf])
        return acc + (_dot(onehot, hi) + _dot(onehot, lo))

    def extra_round(r, acc):
        for cp in strip_copies(step, b, r, buf):
            cp.start()
        return expand_round(r, acc)

    acc = expand_round(0, jnp.zeros((TILE, D), F32))
    acc = lax.fori_loop(1, rounds_s[step], extra_round, acc)
    o_ref[0] = x_ref[0] + mod_ref[0][5:6] * acc


def _combine(x1, mod_i, slot_t_pad, y, off_flat, rounds_flat):
    nb, ntot, _ = x1.shape
    nt = ntot // TILE
    grid_spec = pltpu.PrefetchScalarGridSpec(
        num_scalar_prefetch=2,
        grid=(nb, nt),
        in_specs=[pl.BlockSpec((1, TILE, D), lambda b, t, o, r: (b, t, 0)),
                  pl.BlockSpec((1, 6, D), lambda b, t, o, r: (jnp.where(t == 0, nb, b), 0, 0)),
                  pl.BlockSpec((1, TILE, LANE), lambda b, t, o, r: (b, t, 0)),
                  pl.BlockSpec(memory_space=pl.ANY)],
        out_specs=pl.BlockSpec((1, TILE, D), lambda b, t, o, r: (b, t, 0)),
        scratch_shapes=[pltpu.VMEM((2, N_EXP * WIN, D), F32), pltpu.SemaphoreType.DMA((2,))],
    )
    return pl.pallas_call(
        _combine_kernel,
        out_shape=jax.ShapeDtypeStruct((nb, ntot, D), F32),
        grid_spec=grid_spec,
        compiler_params=_cparams(("arbitrary", "arbitrary")),
        name="ec_combine",
    )(off_flat, rounds_flat, x1, mod_i, slot_t_pad, y)


def _final_kernel(x_ref, g_ref, o_ref):
    x = x_ref[0]
    ms = jnp.mean(x * x, axis=-1, keepdims=True)
    o_ref[0] = x * lax.rsqrt(ms + EPS) * g_ref[...]


def _final_norm(xa, gain, n_ctx):
    nb, ntot, _ = xa.shape
    n_lat = ntot - n_ctx
    skip = n_ctx // TILE
    return pl.pallas_call(
        _final_kernel,
        out_shape=jax.ShapeDtypeStruct((nb, n_lat, D), F32),
        grid=(nb, n_lat // TILE),
        in_specs=[pl.BlockSpec((1, TILE, D), lambda b, t: (b, t + skip, 0)),
                  pl.BlockSpec((1, D), lambda b, t: (0, 0))],
        out_specs=pl.BlockSpec((1, TILE, D), lambda b, t: (b, t, 0)),
        compiler_params=_cparams(("arbitrary", "arbitrary")),
        name="final_norm",
    )(xa, gain)


def _moe(x1, h2, aff_t, mod_i, wg, wu, wd, n_ctx):
    nb, ntot, _ = x1.shape
    nt = ntot // TILE
    _, _, _, rows_ffn = _slot_geometry(n_ctx, ntot - n_ctx)
    rows_alloc = rows_ffn + WIN
    slot, off, cnt = _topk(aff_t, n_ctx)
    off_t = jnp.swapaxes(off[:, :, :nt], 1, 2)
    cnt_t = jnp.swapaxes(cnt[:, :, :nt], 1, 2)
    rounds_c = jnp.maximum(1, jnp.max((cnt_t + WIN - 1) // WIN, axis=2))
    fill = jnp.max((rows_alloc - off_t[:, nt - 1, :] + WIN - 1) // WIN, axis=1)
    rounds_d = rounds_c.at[:, nt - 1].max(fill)
    off_flat = off_t.reshape(-1)
    slot_pad = jnp.pad(slot, ((0, 0), (0, LANE - N_EXP), (0, 0)), constant_values=-1)
    slot_t_pad = jnp.swapaxes(slot_pad, 1, 2)
    aff_pad = jnp.pad(jnp.swapaxes(aff_t, 1, 2), ((0, 0), (0, 0), (0, LANE - N_EXP)))
    xg = _dispatch(h2, slot_pad, aff_pad, off_flat, rounds_d.reshape(-1), rows_alloc)
    y = _expert_ffn(xg, wg, wu, wd, rows_ffn)
    return _combine(x1, mod_i, slot_t_pad, y, off_flat, rounds_c.reshape(-1))


def kernel(x, c, ctx, c_ctx, w_mod, b_mod, norm_mix, norm_ffn, mlstm_w_in, mlstm_b_gate, mlstm_norm, mlstm_w_out,
           attn_w_in, attn_q_norm, attn_k_norm, attn_w_out, moe_router, moe_w_gate, moe_w_up, moe_w_down,
           norm_final):
    nb, n_lat, _ = x.shape
    n_ctx = ctx.shape[1]
    depth = w_mod.shape[0]
    assert n_ctx == TILE and n_lat % TILE == 0 and x.shape[2] == D
    xa = jnp.concatenate([ctx, x], axis=1)
    rb = -(-(nb + 1) // SUB) * SUB
    cc = jnp.concatenate([c, c_ctx[None, :], jnp.zeros((rb - nb - 1, D), F32)], axis=0)
    mod = _modulation(cc, w_mod, b_mod)
    cos, sin = _rope_tables(n_lat, n_ctx)
    for i in range(depth):
        j = i // 2
        mod_i = mod[i]
        gain_mix = norm_mix[i].reshape(1, D)
        gain_ffn = norm_ffn[i].reshape(1, D)
        if i % 2 == 0:
            q, kt, v, og, gc, gr = _proj_mlstm(xa, mod_i, gain_mix, mlstm_w_in[j], mlstm_b_gate[j])
            hf, hb = _mlstm_scan(q, kt, v, gc, gr)
            x1, h2, aff_t = _readout(xa, mod_i, (hf, hb, og), mlstm_w_out[j], gain_ffn, moe_router[i],
                                     mlstm_norm=mlstm_norm[j].reshape(1, M_V))
        else:
            q, kt, vd, flag = _proj_attn(xa, mod_i, gain_mix, attn_w_in[j], attn_q_norm[j], attn_k_norm[j], cos, sin)
            oa = _attention(q, kt, vd, flag)
            x1, h2, aff_t = _readout(xa, mod_i, oa, attn_w_out[j], gain_ffn, moe_router[i])
        xa = _moe(x1, h2, aff_t, mod_i, moe_w_gate[i].astype(BF16), moe_w_up[i].astype(BF16),
                  moe_w_down[i].astype(BF16), n_ctx)
    return _final_norm(xa, norm_final.reshape(1, D), n_ctx)
```

```python
import functools

import jax
import jax.numpy as jnp
from jax import lax
from jax.experimental import pallas as pl
from jax.experimental.pallas import tpu as pltpu

F32 = jnp.float32
BF16 = jnp.bfloat16
I32 = jnp.int32

D = 1024
TILE = 256
EPS = 1e-6
DEPTH = 4

M_HEADS = 4
M_DK = 128
M_DV = 256
M_QK = M_HEADS * M_DK
M_V = M_HEADS * M_DV
M_AUG = M_DV + 128
GATE_CAP = 15.0

A_HEADS = 16
A_KV = 4
A_GRP = 4
A_HD = 64
ROPE_THETA = 10000.0
GRID_W = 64
LOG2E = 1.4426950408889634

N_EXP = 16
EC_CAPACITY = 2
WIN = 48
GATE_LANES = 128
XW = D + GATE_LANES
HALF = D // 2
XP = HALF + GATE_LANES
HIGH16 = -65536

LANE = 128
SUB = 8
VMEM_LIMIT = 56 * 1024 * 1024


def _cparams(sem):
    return pltpu.CompilerParams(dimension_semantics=sem, vmem_limit_bytes=VMEM_LIMIT)


def _dot(a, b):
    return jnp.dot(a, b, preferred_element_type=F32)


def _dot_nt(a, b):
    return lax.dot_general(a, b, (((1,), (1,)), ((), ())), preferred_element_type=F32)


def _split2(x):
    hi = x.astype(BF16)
    lo = (x - hi.astype(F32)).astype(BF16)
    return hi, lo


def _split3(x):
    hi = x.astype(BF16)
    r = x - hi.astype(F32)
    mid = r.astype(BF16)
    lo = (r - mid.astype(F32)).astype(BF16)
    return hi, mid, lo


def _dot3(a, b):
    ah, al = _split2(a)
    bh, bl = _split2(b)
    return _dot(ah, bh) + (_dot(ah, bl) + _dot(al, bh))


def _rms_mod(x, gain, shift, scale):
    ms = jnp.mean(x * x, axis=-1, keepdims=True)
    y = x * lax.rsqrt(ms + EPS) * gain
    return y * (1.0 + scale) + shift


def _iota(shape, dim):
    return lax.broadcasted_iota(I32, shape, dim)


def _mod_index(nb):
    return lambda b, t: (jnp.where(t == 0, nb, b), 0, 0)


def _mod_kernel(c_ref, w_ref, b_ref, o_ref):
    c = c_ref[...]
    s = c * jax.nn.sigmoid(c)
    o_ref[...] = _dot3(s, w_ref[...]) + b_ref[...]


def _modulation(cc, w_mod, b_mod):
    depth, _, n6 = w_mod.shape
    rb = cc.shape[0]
    nj = n6 // D
    out = pl.pallas_call(
        _mod_kernel,
        out_shape=jax.ShapeDtypeStruct((depth, rb, n6), F32),
        grid=(depth, nj),
        in_specs=[
            pl.BlockSpec((rb, D), lambda i, j: (0, 0)),
            pl.BlockSpec((None, D, D), lambda i, j: (i, 0, j)),
            pl.BlockSpec((None, 1, D), lambda i, j: (i, 0, j)),
        ],
        out_specs=pl.BlockSpec((None, rb, D), lambda i, j: (i, 0, j)),
        compiler_params=_cparams(("arbitrary", "arbitrary")),
        name="adaln_mod",
    )(cc, w_mod, b_mod.reshape(depth, 1, n6))
    return out.reshape(depth, rb, nj, D)


def _gate_act(g, idx):
    g = GATE_CAP * jnp.tanh(g * (1.0 / GATE_CAP))
    logsig = jnp.minimum(g, 0.0) - jnp.log(1.0 + jnp.exp(-jnp.abs(g)))
    is_forget = ((idx >> 2) & 1) == 1
    return jnp.where(is_forget, logsig, g)


def _proj_mlstm_kernel(x_ref, mod_ref, gain_ref, w_ref, wkt_ref, wg_ref, wgt_ref, bc_ref, br_ref,
                       q_ref, kt_ref, v_ref, o_ref, gc_ref, gr_ref):
    m = mod_ref[0]
    h = _rms_mod(x_ref[0], gain_ref[...], m[0:1], m[1:2])
    hb = h.astype(BF16)
    r = _dot(hb, w_ref[...])
    q_ref[0] = (r[:, :M_QK] * (M_DK ** -0.5)).astype(BF16)
    v_ref[0] = r[:, M_QK:M_QK + M_V].astype(BF16)
    o_ref[0] = r[:, M_QK + M_V:].astype(BF16)
    kt_ref[0] = _dot_nt(wkt_ref[...], hb).astype(BF16)
    gc = _dot(hb, wg_ref[...]) + bc_ref[...]
    gc_ref[0] = _gate_act(gc, _iota(gc.shape, 1))
    gr = _dot_nt(wgt_ref[...], hb) + br_ref[...]
    gr_ref[0] = _gate_act(gr, _iota(gr.shape, 0))


def _proj_mlstm(xa, mod_i, gain, w_in, b_gate):
    nb, ntot, _ = xa.shape
    nt = ntot // TILE
    n_g = 4 * M_HEADS
    wq = w_in[:, :M_QK]
    wk = w_in[:, M_QK:2 * M_QK]
    wvo = w_in[:, 2 * M_QK:2 * M_QK + 2 * M_V]
    wg = w_in[:, 2 * M_QK + 2 * M_V:]
    w_main = jnp.concatenate([wq, wvo], axis=1).astype(BF16)
    wkt = wk.T.astype(BF16)
    wg_pad = jnp.pad(wg, ((0, 0), (0, LANE - n_g))).astype(BF16)
    wgt = wg.T.astype(BF16)
    bc = jnp.pad(b_gate, (0, LANE - n_g)).reshape(1, LANE)
    br = b_gate.reshape(n_g, 1)
    tok = lambda w: pl.BlockSpec((1, TILE, w), lambda b, t: (b, t, 0))
    full = lambda a: pl.BlockSpec(a.shape, lambda b, t: (0,) * a.ndim)
    return pl.pallas_call(
        _proj_mlstm_kernel,
        out_shape=(
            jax.ShapeDtypeStruct((nb, ntot, M_QK), BF16),
            jax.ShapeDtypeStruct((nb, M_QK, ntot), BF16),
            jax.ShapeDtypeStruct((nb, ntot, M_V), BF16),
            jax.ShapeDtypeStruct((nb, ntot, M_V), BF16),
            jax.ShapeDtypeStruct((nb, ntot, LANE), F32),
            jax.ShapeDtypeStruct((nb, n_g, ntot), F32),
        ),
        grid=(nb, nt),
        in_specs=[tok(D), pl.BlockSpec((1, 6, D), _mod_index(nb)), full(gain), full(w_main), full(wkt),
                  full(wg_pad), full(wgt), full(bc), full(br)],
        out_specs=(tok(M_QK), pl.BlockSpec((1, M_QK, TILE), lambda b, t: (b, 0, t)), tok(M_V), tok(M_V),
                   tok(LANE), pl.BlockSpec((1, n_g, TILE), lambda b, t: (b, 0, t))),
        compiler_params=_cparams(("arbitrary", "arbitrary")),
        name="proj_mlstm",
    )(xa, mod_i, gain, w_main, wkt, wg_pad, wgt, bc, br)


def _mlstm_kernel(qf_ref, qb_ref, kf_ref, kb_ref, vf_ref, vb_ref, gcf_ref, gcb_ref, grf_ref, grb_ref,
                  hf_ref, hb_ref, c_ref, m_ref):
    t = pl.program_id(1)

    @pl.when(t == 0)
    def _():
        c_ref[...] = jnp.zeros(c_ref.shape, F32)
        m_ref[...] = jnp.zeros(m_ref.shape, F32)

    n = TILE
    row = _iota((n, n), 0)
    col = _iota((n, n), 1)
    lower = col <= row
    upper = col >= row
    lower_b = lower.astype(BF16)
    upper_b = upper.astype(BF16)
    ones_col = (_iota((n, M_AUG - M_DV), 1) == 0).astype(BF16)
    dirs = ((qf_ref, kf_ref, vf_ref, gcf_ref, grf_ref, hf_ref, lower, lower_b, upper_b, n - 1),
            (qb_ref, kb_ref, vb_ref, gcb_ref, grb_ref, hb_ref, upper, upper_b, lower_b, 0))
    for d, (q_ref, k_ref, v_ref, gc_ref, gr_ref, o_ref, mask, cum_l, cum_r, last) in enumerate(dirs):
        gc = gc_ref[0]
        gr = gr_ref[0]
        bcol = sum(_dot(cum_l, p) for p in _split3(gc))
        brow = sum(_dot(p, cum_r) for p in _split3(gr))
        for h in range(M_HEADS):
            gi = 8 * d + h
            gf = gi + 4
            sidx = 4 * d + h
            b_col = bcol[:, gf:gf + 1]
            b_row = brow[gf:gf + 1, :]
            ig_row = gr[gi:gi + 1, :]
            total = b_row[:, last:last + 1]
            m_st = m_ref[sidx][0:1, 0:1]
            key_row = ig_row - b_row
            log_d = jnp.where(mask, b_col + key_row, -jnp.inf)
            m_inter = b_col + m_st
            m_q = jnp.maximum(m_inter, jnp.max(log_d, axis=1, keepdims=True))
            w_intra = jnp.exp(log_d - m_q)
            w_inter = jnp.exp(m_inter - m_q)
            qh = q_ref[0, :, M_DK * h:M_DK * (h + 1)]
            kth = k_ref[0, M_DK * h:M_DK * (h + 1), :]
            v_aug = jnp.concatenate([v_ref[0, :, M_DV * h:M_DV * (h + 1)], ones_col], axis=1)
            c_aug = c_ref[sidx]
            s = (_dot(qh, kth) * w_intra).astype(BF16)
            nd = _dot(s, v_aug) + w_inter * _dot(qh, c_aug.astype(BF16))
            den = nd[:, M_DV:M_DV + 1]
            inv = 1.0 / jnp.maximum(jnp.abs(den), jnp.exp(-m_q))
            o_ref[0, :, M_DV * h:M_DV * (h + 1)] = (nd[:, :M_DV] * inv).astype(o_ref.dtype)
            log_w = total + key_row
            m_new = jnp.maximum(total + m_st, jnp.max(log_w, axis=1, keepdims=True))
            w_key = jnp.exp(log_w - m_new)
            decay = jnp.exp(total + m_st - m_new)
            kw = (kth.astype(F32) * w_key).astype(BF16)
            c_ref[sidx] = decay * c_aug + _dot(kw, v_aug)
            m_ref[sidx] = jnp.broadcast_to(m_new, m_ref.shape[1:])


def _mlstm_scan(q, kt, v, gc, gr):
    nb, ntot, _ = q.shape
    nt = ntot // TILE
    fwd = lambda b, t: (b, t, 0)
    bwd = lambda b, t: (b, jnp.where(t == 0, 0, nt - t), 0)
    fwd_t = lambda b, t: (b, 0, t)
    bwd_t = lambda b, t: (b, 0, jnp.where(t == 0, 0, nt - t))
    n_g = gr.shape[1]
    return pl.pallas_call(
        _mlstm_kernel,
        out_shape=(jax.ShapeDtypeStruct((nb, ntot, M_V), BF16), jax.ShapeDtypeStruct((nb, ntot, M_V), BF16)),
        grid=(nb, nt),
        in_specs=[
            pl.BlockSpec((1, TILE, M_QK), fwd), pl.BlockSpec((1, TILE, M_QK), bwd),
            pl.BlockSpec((1, M_QK, TILE), fwd_t), pl.BlockSpec((1, M_QK, TILE), bwd_t),
            pl.BlockSpec((1, TILE, M_V), fwd), pl.BlockSpec((1, TILE, M_V), bwd),
            pl.BlockSpec((1, TILE, LANE), fwd), pl.BlockSpec((1, TILE, LANE), bwd),
            pl.BlockSpec((1, n_g, TILE), fwd_t), pl.BlockSpec((1, n_g, TILE), bwd_t),
        ],
        out_specs=(pl.BlockSpec((1, TILE, M_V), fwd), pl.BlockSpec((1, TILE, M_V), bwd)),
        scratch_shapes=[pltpu.VMEM((2 * M_HEADS, M_DK, M_AUG), F32), pltpu.VMEM((2 * M_HEADS, SUB, LANE), F32)],
        compiler_params=_cparams(("arbitrary", "arbitrary")),
        name="mlstm_scan",
    )(q, q, kt, kt, v, v, gc, gc, gr, gr)


def _head_norm(x, gain):
    w = x.shape[1]
    gsum = ((_iota((w, LANE), 0) >> 6) == _iota((w, LANE), 1)).astype(BF16)
    gexp = ((_iota((LANE, w), 1) >> 6) == _iota((LANE, w), 0)).astype(BF16)
    hi, lo = _split2(x * x)
    ssum = _dot(hi, gsum) + _dot(lo, gsum)
    rh, rl = _split2(lax.rsqrt(ssum * (1.0 / A_HD) + EPS))
    return x * (_dot(rh, gexp) + _dot(rl, gexp)) * gain


def _rope(x, cos, sin_signed):
    first = (_iota((x.shape[0], LANE), 1) & (A_HD - 1)) < A_HD // 2
    tiles = []
    for i in range(x.shape[1] // LANE):
        xt = x[:, LANE * i:LANE * (i + 1)]
        partner = jnp.where(first, pltpu.roll(xt, LANE - A_HD // 2, 1), pltpu.roll(xt, A_HD // 2, 1))
        tiles.append(xt * cos + partner * sin_signed)
    return jnp.concatenate(tiles, axis=1)


def _proj_attn_kernel(x_ref, mod_ref, gain_ref, w_ref, qg_ref, kg_ref, cos_ref, sin_ref, shift_ref,
                      q_ref, kt_ref, vd_ref):
    m = mod_ref[0]
    h = _rms_mod(x_ref[0], gain_ref[...], m[0:1], m[1:2])
    r = _dot(h.astype(BF16), w_ref[...])
    nq = A_HEADS * A_HD
    nk = A_KV * A_HD
    cos = cos_ref[...]
    sin = sin_ref[...]
    qn = _rope(_head_norm(r[:, :nq], qg_ref[...]), cos, sin)
    kn = _rope(_head_norm(r[:, nq:nq + nk], kg_ref[...]), cos, sin)
    v = r[:, nq + nk:]
    lane = _iota((TILE, LANE), 1)
    low = lane < A_HD
    one_hot = jnp.where(lane == A_HD, 1.0, 0.0)

    def head_tile(x, idx):
        tile = x[:, LANE * (idx // 2):LANE * (idx // 2 + 1)]
        if idx % 2 == 1:
            tile = pltpu.roll(tile, A_HD, 1)
        return jnp.where(low, tile, one_hot).astype(BF16)

    for hd in range(A_HEADS):
        q_ref[0, hd] = head_tile(qn, hd)
    for g in range(A_KV):
        vd_ref[0, g] = head_tile(v, g)
    kt = jnp.transpose(kn)
    extra = jnp.where(_iota((LANE - A_HD, TILE), 0) == 0, shift_ref[...], 0.0).astype(BF16)
    for g in range(A_KV):
        kt_ref[0, g, 0:A_HD, :] = kt[A_HD * g:A_HD * (g + 1), :].astype(BF16)
        kt_ref[0, g, A_HD:LANE, :] = extra


def _rope_tables(n_lat, n_ctx):
    rows = n_lat // GRID_W
    row = jnp.repeat(jnp.arange(rows, dtype=F32), GRID_W)
    col = jnp.tile(jnp.arange(GRID_W, dtype=F32), rows)
    pairs = A_HD // 4
    inv = ROPE_THETA ** (-jnp.arange(pairs, dtype=F32) / pairs)
    ang = jnp.concatenate([row[:, None] * inv, col[:, None] * inv], axis=-1)
    c = jnp.cos(ang)
    s = jnp.sin(ang)
    cos = jnp.concatenate([c, c, c, c], axis=-1)
    sin = jnp.concatenate([-s, s, -s, s], axis=-1)
    cos = jnp.concatenate([jnp.ones((n_ctx, LANE), F32), cos], axis=0)
    sin = jnp.concatenate([jnp.zeros((n_ctx, LANE), F32), sin], axis=0)
    return cos, sin


SHIFT_LIMIT = 60.0


def _softmax_shift(qg, kg):
    bound = A_HD * jnp.max(jnp.abs(qg)) * jnp.max(jnp.abs(kg))
    fast = bound <= SHIFT_LIMIT
    shift = jnp.where(fast, jnp.ceil(bound), 0.0)
    return shift, jnp.logical_not(fast).astype(I32)


def _proj_attn(xa, mod_i, gain, w_in, q_norm, k_norm, cos, sin):
    nb, ntot, _ = xa.shape
    nt = ntot // TILE
    nq = A_HEADS * A_HD
    nk = A_KV * A_HD
    w = w_in.astype(BF16)
    qg = (jnp.tile(q_norm, A_HEADS) * (A_HD ** -0.5 * LOG2E)).reshape(1, nq)
    kg = jnp.tile(k_norm, A_KV).reshape(1, nk)
    shift, flag = _softmax_shift(qg, kg)
    neg_shift = (-shift).reshape(1, 1).astype(F32)
    full = lambda a: pl.BlockSpec(a.shape, lambda b, t: (0,) * a.ndim)
    tab = pl.BlockSpec((TILE, LANE), lambda b, t: (t, 0))
    q, kt, vd = pl.pallas_call(
        _proj_attn_kernel,
        out_shape=(
            jax.ShapeDtypeStruct((nb, A_HEADS, ntot, LANE), BF16),
            jax.ShapeDtypeStruct((nb, A_KV, LANE, ntot), BF16),
            jax.ShapeDtypeStruct((nb, A_KV, ntot, LANE), BF16),
        ),
        grid=(nb, nt),
        in_specs=[pl.BlockSpec((1, TILE, D), lambda b, t: (b, t, 0)), pl.BlockSpec((1, 6, D), _mod_index(nb)),
                  full(gain), full(w), full(qg), full(kg), tab, tab, full(neg_shift)],
        out_specs=(pl.BlockSpec((1, A_HEADS, TILE, LANE), lambda b, t: (b, 0, t, 0)),
                   pl.BlockSpec((1, A_KV, LANE, TILE), lambda b, t: (b, 0, 0, t)),
                   pl.BlockSpec((1, A_KV, TILE, LANE), lambda b, t: (b, 0, t, 0))),
        compiler_params=_cparams(("arbitrary", "arbitrary")),
        name="proj_attn",
    )(xa, mod_i, gain, w, qg, kg, cos, sin, neg_shift)
    return q, kt, vd, flag.reshape(1)


def _attn_kernel(flag_ref, q_ref, kt_ref, vd_ref, o_ref, *, ntot):
    qi = pl.program_id(2)
    tq = q_ref.shape[2]
    low = _iota((tq, LANE), 1) < A_HD

    def attend(nk, row_max):
        outs = []
        for hd in range(A_GRP):
            s = _dot(q_ref[0, hd], kt_ref[0, 0, :, :nk])
            if row_max:
                s = s - jnp.max(s, axis=1, keepdims=True)
            r = _dot(jnp.exp2(s).astype(BF16), vd_ref[0, 0, :nk, :])
            outs.append(r / r[:, A_HD:A_HD + 1])
        t0 = jnp.where(low, outs[0], pltpu.roll(outs[1], A_HD, 1))
        t1 = jnp.where(low, outs[2], pltpu.roll(outs[3], A_HD, 1))
        o_ref[0, 0] = jnp.concatenate([t0, t1], axis=1).astype(BF16)

    @pl.when(qi == 0)
    def _():
        attend(TILE, True)

    @pl.when(jnp.logical_and(qi > 0, flag_ref[0] == 0))
    def _():
        attend(ntot, False)

    @pl.when(jnp.logical_and(qi > 0, flag_ref[0] != 0))
    def _():
        attend(ntot, True)


def _attention(q, kt, vd, flag):
    nb, _, ntot, _ = q.shape
    nq = ntot // TILE
    grid_spec = pltpu.PrefetchScalarGridSpec(
        num_scalar_prefetch=1,
        grid=(nb, A_KV, nq),
        in_specs=[pl.BlockSpec((1, A_GRP, TILE, LANE), lambda b, g, i, f: (b, g, i, 0)),
                  pl.BlockSpec((1, 1, LANE, ntot), lambda b, g, i, f: (b, g, 0, 0)),
                  pl.BlockSpec((1, 1, ntot, LANE), lambda b, g, i, f: (b, g, 0, 0))],
        out_specs=pl.BlockSpec((1, 1, TILE, A_GRP * A_HD), lambda b, g, i, f: (b, g, i, 0)),
    )
    return pl.pallas_call(
        functools.partial(_attn_kernel, ntot=ntot),
        out_shape=jax.ShapeDtypeStruct((nb, A_KV, ntot, A_GRP * A_HD), BF16),
        grid_spec=grid_spec,
        compiler_params=_cparams(("arbitrary", "arbitrary", "arbitrary")),
        name="attention",
    )(flag, q, kt, vd)


def _route_tail(x, y, m, gain_ref, wrt_ref, x1_ref, h2_ref, aff_ref):
    x1 = x + m[2:3] * y
    x1_ref[0] = x1
    h2 = _rms_mod(x1, gain_ref[...], m[3:4], m[4:5])
    h2_ref[0] = h2.astype(BF16)
    logits = _dot3_nt(wrt_ref[...], h2)
    e = jnp.exp(logits - jnp.max(logits, axis=0, keepdims=True))
    aff_ref[0] = e / jnp.sum(e, axis=0, keepdims=True)


def _dot3_nt(a, b):
    ah, al = _split2(a)
    bh, bl = _split2(b)
    return _dot_nt(ah, bh) + (_dot_nt(ah, bl) + _dot_nt(al, bh))


def _readout_mlstm_kernel(x_ref, mod_ref, hf_ref, hb_ref, o_ref, mn_ref, wo_ref, gain_ref, wrt_ref,
                          x1_ref, h2_ref, aff_ref):
    hh = hf_ref[0].astype(F32) + hb_ref[0].astype(F32)
    mn = mn_ref[...]
    parts = []
    for h in range(M_HEADS):
        seg = hh[:, M_DV * h:M_DV * (h + 1)]
        ms = jnp.mean(seg * seg, axis=-1, keepdims=True)
        parts.append(seg * lax.rsqrt(ms + EPS) * mn[:, M_DV * h:M_DV * (h + 1)])
    z = jnp.concatenate(parts, axis=1) * jax.nn.sigmoid(o_ref[0].astype(F32))
    y = _dot(z.astype(BF16), wo_ref[...])
    _route_tail(x_ref[0], y, mod_ref[0], gain_ref, wrt_ref, x1_ref, h2_ref, aff_ref)


def _readout_attn_kernel(x_ref, mod_ref, oa_ref, wo_ref, gain_ref, wrt_ref, x1_ref, h2_ref, aff_ref):
    gw = A_GRP * A_HD
    y = _dot(oa_ref[0, 0], wo_ref[0:gw, :])
    for g in range(1, A_KV):
        y = y + _dot(oa_ref[0, g], wo_ref[gw * g:gw * (g + 1), :])
    _route_tail(x_ref[0], y, mod_ref[0], gain_ref, wrt_ref, x1_ref, h2_ref, aff_ref)


def _readout(xa, mod_i, mixer_out, w_out, gain_ffn, w_router, mlstm_norm=None):
    nb, ntot, _ = xa.shape
    nt = ntot // TILE
    tok = lambda w: pl.BlockSpec((1, TILE, w), lambda b, t: (b, t, 0))
    full = lambda a: pl.BlockSpec(a.shape, lambda b, t: (0,) * a.ndim)
    wo = w_out.astype(BF16)
    wrt = w_router.T
    common_in = [tok(D), pl.BlockSpec((1, 6, D), _mod_index(nb))]
    tail_in = [full(wo), full(gain_ffn), full(wrt)]
    if mlstm_norm is not None:
        hf, hb, og = mixer_out
        kern = _readout_mlstm_kernel
        mid_in = [tok(M_V), tok(M_V), tok(M_V), full(mlstm_norm)]
        args = (xa, mod_i, hf, hb, og, mlstm_norm, wo, gain_ffn, wrt)
    else:
        kern = _readout_attn_kernel
        mid_in = [pl.BlockSpec((1, A_KV, TILE, A_GRP * A_HD), lambda b, t: (b, 0, t, 0))]
        args = (xa, mod_i, mixer_out, wo, gain_ffn, wrt)
    return pl.pallas_call(
        kern,
        out_shape=(jax.ShapeDtypeStruct((nb, ntot, D), F32), jax.ShapeDtypeStruct((nb, ntot, D), BF16),
                   jax.ShapeDtypeStruct((nb, N_EXP, ntot), F32)),
        grid=(nb, nt),
        in_specs=common_in + mid_in + tail_in,
        out_specs=(tok(D), tok(D), pl.BlockSpec((1, N_EXP, TILE), lambda b, t: (b, 0, t))),
        compiler_params=_cparams(("arbitrary", "arbitrary")),
        name="readout_route",
    )(*args)


def _slot_geometry(n_ctx, n_lat):
    cap_ctx = max(1, EC_CAPACITY * n_ctx // N_EXP)
    cap_lat = max(1, EC_CAPACITY * n_lat // N_EXP)
    lat_base = -(-cap_ctx // SUB) * SUB
    slots_max = lat_base + cap_lat + (SUB - 1) * (n_lat // TILE)
    rows_ffn = -(-slots_max // 16) * 16
    return cap_ctx, cap_lat, lat_base, rows_ffn


def _topk_kernel(a_ref, slot_ref, off_ref, cnt_ref, *, n_ctx, cap_ctx, cap_lat, lat_base):
    bits = lax.bitcast_convert_type(a_ref[0], I32)
    prefix = (_iota((LANE, LANE), 0) <= _iota((LANE, LANE), 1)).astype(BF16)

    def count(mask):
        return jnp.sum(mask.astype(F32), axis=1, keepdims=True)

    def select(x, cap):
        thr = jnp.zeros((N_EXP, 1), I32)
        for bit in range(30, -1, -1):
            cand = thr | (1 << bit)
            thr = jnp.where(count(x >= cand) >= cap, cand, thr)
        gt = x > thr
        eq = x == thr
        need = cap - count(gt)
        run = jnp.zeros((N_EXP, 1), F32)
        blocks = []
        for j in range(x.shape[1] // LANE):
            sl = slice(LANE * j, LANE * (j + 1))
            eqf = eq[:, sl].astype(F32)
            inc = _dot(eqf.astype(BF16), prefix)
            rank = run + inc - eqf
            blocks.append(jnp.logical_or(gt[:, sl], jnp.logical_and(eq[:, sl], rank < need)))
            run = run + inc[:, LANE - 1:LANE]
        return blocks

    blocks = select(bits[:, :n_ctx], cap_ctx) + select(bits[:, n_ctx:], cap_lat)
    per_tile = TILE // LANE
    lane = _iota((N_EXP, LANE), 1)
    off_acc = jnp.zeros((N_EXP, LANE), I32)
    cnt_acc = jnp.zeros((N_EXP, LANE), I32)
    base = jnp.zeros((N_EXP, 1), F32)
    for t in range(len(blocks) // per_tile):
        if t == n_ctx // TILE:
            base = jnp.full((N_EXP, 1), float(lat_base), F32)
        run = jnp.zeros((N_EXP, 1), F32)
        for j in range(per_tile):
            blk = blocks[per_tile * t + j]
            sf = blk.astype(F32)
            inc = _dot(sf.astype(BF16), prefix)
            pos = base + run + inc - sf
            c0 = LANE * (per_tile * t + j)
            slot_ref[0, :, c0:c0 + LANE] = jnp.where(blk, pos.astype(I32), -1)
            run = run + inc[:, LANE - 1:LANE]
        n8 = jnp.floor((run + (SUB - 1)) * (1.0 / SUB)) * SUB
        off_acc = jnp.where(lane == t, base.astype(I32), off_acc)
        cnt_acc = jnp.where(lane == t, n8.astype(I32), cnt_acc)
        base = base + n8
    off_ref[0] = off_acc
    cnt_ref[0] = cnt_acc


def _topk(aff_t, n_ctx):
    nb, _, ntot = aff_t.shape
    cap_ctx, cap_lat, lat_base, _ = _slot_geometry(n_ctx, ntot - n_ctx)
    kern = functools.partial(_topk_kernel, n_ctx=n_ctx, cap_ctx=cap_ctx, cap_lat=cap_lat, lat_base=lat_base)
    return pl.pallas_call(
        kern,
        out_shape=(jax.ShapeDtypeStruct((nb, N_EXP, ntot), I32), jax.ShapeDtypeStruct((nb, N_EXP, LANE), I32),
                   jax.ShapeDtypeStruct((nb, N_EXP, LANE), I32)),
        grid=(nb,),
        in_specs=[pl.BlockSpec((1, N_EXP, ntot), lambda b: (b, 0, 0))],
        out_specs=(pl.BlockSpec((1, N_EXP, ntot), lambda b: (b, 0, 0)),
                   pl.BlockSpec((1, N_EXP, LANE), lambda b: (b, 0, 0)),
                   pl.BlockSpec((1, N_EXP, LANE), lambda b: (b, 0, 0))),
        compiler_params=_cparams(("arbitrary",)),
        name="ec_topk",
    )(aff_t)


def _strip_index(shape, dim):
    i = _iota(shape, dim)
    e = jnp.floor((i.astype(F32) + 0.5) * (1.0 / WIN)).astype(I32)
    return e, i - WIN * e


def _dispatch_kernel(off_s, rounds_s, h_ref, slot_ref, aff_ref, xg_ref, stage_ref, sem, cnt_ref):
    b = pl.program_id(0)
    t = pl.program_id(1)
    nt = pl.num_programs(1)
    base = (b * nt + t) * N_EXP
    rows = N_EXP * WIN
    a = aff_ref[0]
    src = _iota((LANE, LANE), 0)
    dst = _iota((LANE, LANE), 1)
    a3 = None
    for p, piece in enumerate(_split3(a)):
        sel = jnp.logical_and(dst == 3 * src + p, src < N_EXP).astype(BF16)
        term = _dot(piece, sel)
        a3 = term if a3 is None else a3 + term
    rhs = jnp.concatenate([h_ref[0], a3.astype(BF16)], axis=1)
    e_of_row, _ = _strip_index((rows, LANE), 0)
    expand = (e_of_row == _iota((rows, LANE), 1)).astype(BF16)
    _, j_row = _strip_index((rows, 1), 0)
    sl = slot_ref[0]
    e_row = _iota((LANE, 1), 0)
    off_v = jnp.zeros((LANE, 1), I32)
    for e in range(N_EXP):
        off_v = jnp.where(e_row == e, off_s[base + e], off_v)

    @pl.when(jnp.logical_and(b == 0, t == 0))
    def _():
        cnt_ref[0] = 0

    def strip_copies(buf, sample, starts):
        return [pltpu.make_async_copy(stage_ref.at[buf, pl.ds(WIN * e, WIN), :],
                                      xg_ref.at[sample, e, pl.ds(starts[e], WIN), :], sem.at[buf])
                for e in range(N_EXP)]

    def wait_strips(buf):
        for cp in strip_copies(buf, 0, [0] * N_EXP):
            cp.wait()

    def round_body(r, carry):
        n = cnt_ref[0]
        buf = n & 1
        rel = jnp.clip(sl - (off_v + WIN * r), -1, WIN).astype(F32).astype(BF16)
        relx = _dot(expand, rel)
        onehot = (relx == j_row.astype(F32)).astype(BF16)
        bits = lax.bitcast_convert_type(_dot(onehot, rhs), I32)
        packed = jnp.bitwise_or(jnp.bitwise_and(bits[:, :HALF] >> 16, 0xFFFF),
                                jnp.bitwise_and(bits[:, HALF:D], HIGH16))
        stage_ref[buf] = jnp.concatenate([packed, bits[:, D:]], axis=1)

        @pl.when(n > 0)
        def _():
            wait_strips(1 - buf)

        last_start = xg_ref.shape[2] - WIN
        starts = [pl.multiple_of(jnp.minimum(off_s[base + e] + WIN * r, last_start), SUB) for e in range(N_EXP)]
        for cp in strip_copies(buf, b, starts):
            cp.start()
        cnt_ref[0] = n + 1
        return carry

    lax.fori_loop(0, rounds_s[b * nt + t], round_body, 0)

    @pl.when(jnp.logical_and(b == pl.num_programs(0) - 1, t == nt - 1))
    def _():
        wait_strips((cnt_ref[0] - 1) & 1)


def _dispatch(h2, slot_pad, aff_pad, off_flat, rounds_flat, rows_alloc):
    nb, ntot, _ = h2.shape
    nt = ntot // TILE
    grid_spec = pltpu.PrefetchScalarGridSpec(
        num_scalar_prefetch=2,
        grid=(nb, nt),
        in_specs=[pl.BlockSpec((1, TILE, D), lambda b, t, o, r: (b, t, 0)),
                  pl.BlockSpec((1, LANE, TILE), lambda b, t, o, r: (b, 0, t)),
                  pl.BlockSpec((1, TILE, LANE), lambda b, t, o, r: (b, t, 0))],
        out_specs=pl.BlockSpec(memory_space=pl.ANY),
        scratch_shapes=[pltpu.VMEM((2, N_EXP * WIN, XP), I32), pltpu.SemaphoreType.DMA((2,)),
                        pltpu.SMEM((1,), I32)],
    )
    return pl.pallas_call(
        _dispatch_kernel,
        out_shape=jax.ShapeDtypeStruct((nb, N_EXP, rows_alloc, XP), I32),
        grid_spec=grid_spec,
        compiler_params=_cparams(("arbitrary", "arbitrary")),
        name="ec_dispatch",
    )(off_flat, rounds_flat, h2, slot_pad, aff_pad)


def _ffn_kernel(xg_ref, wg_ref, wu_ref, wd_ref, y_ref):
    e = pl.program_id(0)
    words = xg_ref[0, 0, :, :HALF]
    x = jnp.concatenate([lax.bitcast_convert_type(words << 16, F32),
                         lax.bitcast_convert_type(jnp.bitwise_and(words, HIGH16), F32)], axis=1).astype(BF16)
    gl = lax.bitcast_convert_type(xg_ref[0, 0, :, HALF:], F32)
    lane = _iota(gl.shape, 1)
    mine = jnp.logical_and(lane >= 3 * e, lane < 3 * e + 3)
    gate = jnp.sum(jnp.where(mine, gl, 0.0), axis=1, keepdims=True)
    a = _dot(x, wg_ref[0])
    u = _dot(x, wu_ref[0])
    hm = (a * jax.nn.sigmoid(a) * u).astype(BF16)
    y_ref[0, 0] = _dot(hm, wd_ref[0]) * gate


def _expert_ffn(xg, wg, wu, wd, rows_ffn):
    nb = xg.shape[0]
    wspec = pl.BlockSpec((1, D, D), lambda e, b: (e, 0, 0))
    return pl.pallas_call(
        _ffn_kernel,
        out_shape=jax.ShapeDtypeStruct((nb, N_EXP, rows_ffn, D), F32),
        grid=(N_EXP, nb),
        in_specs=[pl.BlockSpec((1, 1, rows_ffn, XP), lambda e, b: (b, e, 0, 0)), wspec, wspec, wspec],
        out_specs=pl.BlockSpec((1, 1, rows_ffn, D), lambda e, b: (b, e, 0, 0)),
        compiler_params=_cparams(("arbitrary", "arbitrary")),
        name="ec_ffn",
    )(xg, wg, wu, wd)


def _combine_kernel(off_s, rounds_s, x_ref, mod_ref, slot_ref, y_ref, o_ref, strip_ref, sem):
    b = pl.program_id(0)
    t = pl.program_id(1)
    nt = pl.num_programs(1)
    base = (b * nt + t) * N_EXP
    cols = N_EXP * WIN
    e_of_col, _ = _strip_index((LANE, cols), 1)
    expand = (e_of_col == _iota((LANE, cols), 0)).astype(BF16)
    _, j_lane = _strip_index((1, cols), 1)
    j_lane = j_lane.astype(F32)
    sl = slot_ref[0]
    e_lane = _iota((1, LANE), 1)
    last_start = y_ref.shape[2] - WIN

    def strip_start(step_, e, r):
        return jnp.minimum(off_s[step_ * N_EXP + e] + WIN * r, last_start)

    step = b * nt + t
    buf = step & 1

    def strip_copies(step_, sample, r, dst):
        return [pltpu.make_async_copy(
            y_ref.at[sample, e, pl.ds(pl.multiple_of(strip_start(step_, e, r), SUB), WIN), :],
            strip_ref.at[dst, pl.ds(WIN * e, WIN), :], sem.at[dst]) for e in range(N_EXP)]

    @pl.when(step == 0)
    def _():
        for cp in strip_copies(step, b, 0, buf):
            cp.start()

    @pl.when(step + 1 < pl.num_programs(0) * nt)
    def _():
        for cp in strip_copies(step + 1, jnp.where(t == nt - 1, b + 1, b), 0, 1 - buf):
            cp.start()

    def expand_round(r, acc):
        first_v = jnp.zeros((1, LANE), I32)
        start_v = jnp.zeros((1, LANE), I32)
        for e in range(N_EXP):
            first_v = jnp.where(e_lane == e, off_s[base + e] + WIN * r, first_v)
            start_v = jnp.where(e_lane == e, strip_start(step, e, r), start_v)
        nominal = sl - first_v
        in_round = jnp.logical_and(nominal >= 0, nominal < WIN)
        rel = jnp.where(in_round, sl - start_v, -1).astype(F32).astype(BF16)
        relx = _dot(rel, expand)
        onehot = (relx == j_lane).astype(BF16)
        for cp in strip_copies(step, b, r, buf):
            cp.wait()
        hi, lo = _split2(strip_ref[buf])
        return acc + (_dot(onehot, hi) + _dot(onehot, lo))

    def extra_round(r, acc):
        for cp in strip_copies(step, b, r, buf):
            cp.start()
        return expand_round(r, acc)

    acc = expand_round(0, jnp.zeros((TILE, D), F32))
    acc = lax.fori_loop(1, rounds_s[step], extra_round, acc)
    o_ref[0] = x_ref[0] + mod_ref[0][5:6] * acc


def _combine(x1, mod_i, slot_t_pad, y, off_flat, rounds_flat):
    nb, ntot, _ = x1.shape
    nt = ntot // TILE
    grid_spec = pltpu.PrefetchScalarGridSpec(
        num_scalar_prefetch=2,
        grid=(nb, nt),
        in_specs=[pl.BlockSpec((1, TILE, D), lambda b, t, o, r: (b, t, 0)),
                  pl.BlockSpec((1, 6, D), lambda b, t, o, r: (jnp.where(t == 0, nb, b), 0, 0)),
                  pl.BlockSpec((1, TILE, LANE), lambda b, t, o, r: (b, t, 0)),
                  pl.BlockSpec(memory_space=pl.ANY)],
        out_specs=pl.BlockSpec((1, TILE, D), lambda b, t, o, r: (b, t, 0)),
        scratch_shapes=[pltpu.VMEM((2, N_EXP * WIN, D), F32), pltpu.SemaphoreType.DMA((2,))],
    )
    return pl.pallas_call(
        _combine_kernel,
        out_shape=jax.ShapeDtypeStruct((nb, ntot, D), F32),
        grid_spec=grid_spec,
        compiler_params=_cparams(("arbitrary", "arbitrary")),
        name="ec_combine",
    )(off_flat, rounds_flat, x1, mod_i, slot_t_pad, y)


def _final_kernel(x_ref, g_ref, o_ref):
    x = x_ref[0]
    ms = jnp.mean(x * x, axis=-1, keepdims=True)
    o_ref[0] = x * lax.rsqrt(ms + EPS) * g_ref[...]


def _final_norm(xa, gain, n_ctx):
    nb, ntot, _ = xa.shape
    n_lat = ntot - n_ctx
    skip = n_ctx // TILE
    return pl.pallas_call(
        _final_kernel,
        out_shape=jax.ShapeDtypeStruct((nb, n_lat, D), F32),
        grid=(nb, n_lat // TILE),
        in_specs=[pl.BlockSpec((1, TILE, D), lambda b, t: (b, t + skip, 0)),
                  pl.BlockSpec((1, D), lambda b, t: (0, 0))],
        out_specs=pl.BlockSpec((1, TILE, D), lambda b, t: (b, t, 0)),
        compiler_params=_cparams(("arbitrary", "arbitrary")),
        name="final_norm",
    )(xa, gain)


def _moe(x1, h2, aff_t, mod_i, wg, wu, wd, n_ctx):
    nb, ntot, _ = x1.shape
    nt = ntot // TILE
    _, _, _, rows_ffn = _slot_geometry(n_ctx, ntot - n_ctx)
    rows_alloc = rows_ffn + WIN
    slot, off, cnt = _topk(aff_t, n_ctx)
    off_t = jnp.swapaxes(off[:, :, :nt], 1, 2)
    cnt_t = jnp.swapaxes(cnt[:, :, :nt], 1, 2)
    rounds_c = jnp.maximum(1, jnp.max((cnt_t + WIN - 1) // WIN, axis=2))
    fill = jnp.max((rows_alloc - off_t[:, nt - 1, :] + WIN - 1) // WIN, axis=1)
    rounds_d = rounds_c.at[:, nt - 1].max(fill)
    off_flat = off_t.reshape(-1)
    slot_pad = jnp.pad(slot, ((0, 0), (0, LANE - N_EXP), (0, 0)), constant_values=-1)
    slot_t_pad = jnp.swapaxes(slot_pad, 1, 2)
    aff_pad = jnp.pad(jnp.swapaxes(aff_t, 1, 2), ((0, 0), (0, 0), (0, LANE - N_EXP)))
    xg = _dispatch(h2, slot_pad, aff_pad, off_flat, rounds_d.reshape(-1), rows_alloc)
    y = _expert_ffn(xg, wg, wu, wd, rows_ffn)
    return _combine(x1, mod_i, slot_t_pad, y, off_flat, rounds_c.reshape(-1))


def kernel(x, c, ctx, c_ctx, w_mod, b_mod, norm_mix, norm_ffn, mlstm_w_in, mlstm_b_gate, mlstm_norm, mlstm_w_out,
           attn_w_in, attn_q_norm, attn_k_norm, attn_w_out, moe_router, moe_w_gate, moe_w_up, moe_w_down,
           norm_final):
    nb, n_lat, _ = x.shape
    n_ctx = ctx.shape[1]
    depth = w_mod.shape[0]
    assert n_ctx == TILE and n_lat % TILE == 0 and x.shape[2] == D
    xa = jnp.concatenate([ctx, x], axis=1)
    rb = -(-(nb + 1) // SUB) * SUB
    cc = jnp.concatenate([c, c_ctx[None, :], jnp.zeros((rb - nb - 1, D), F32)], axis=0)
    mod = _modulation(cc, w_mod, b_mod)
    cos, sin = _rope_tables(n_lat, n_ctx)
    for i in range(depth):
        j = i // 2
        mod_i = mod[i]
        gain_mix = norm_mix[i].reshape(1, D)
        gain_ffn = norm_ffn[i].reshape(1, D)
        if i % 2 == 0:
            q, kt, v, og, gc, gr = _proj_mlstm(xa, mod_i, gain_mix, mlstm_w_in[j], mlstm_b_gate[j])
            hf, hb = _mlstm_scan(q, kt, v, gc, gr)
            x1, h2, aff_t = _readout(xa, mod_i, (hf, hb, og), mlstm_w_out[j], gain_ffn, moe_router[i],
                                     mlstm_norm=mlstm_norm[j].reshape(1, M_V))
        else:
            q, kt, vd, flag = _proj_attn(xa, mod_i, gain_mix, attn_w_in[j], attn_q_norm[j], attn_k_norm[j], cos, sin)
            oa = _attention(q, kt, vd, flag)
            x1, h2, aff_t = _readout(xa, mod_i, oa, attn_w_out[j], gain_ffn, moe_router[i])
        xa = _moe(x1, h2, aff_t, mod_i, moe_w_gate[i].astype(BF16), moe_w_up[i].astype(BF16),
                  moe_w_down[i].astype(BF16), n_ctx)
    return _final_norm(xa, norm_final.reshape(1, D), n_ctx)
```

```python
import functools

import jax
import jax.numpy as jnp
from jax import lax
from jax.experimental import pallas as pl
from jax.experimental.pallas import tpu as pltpu

F32 = jnp.float32
BF16 = jnp.bfloat16
I32 = jnp.int32

D = 1024
TILE = 256
EPS = 1e-6
DEPTH = 4

M_HEADS = 4
M_DK = 128
M_DV = 256
M_QK = M_HEADS * M_DK
M_V = M_HEADS * M_DV
M_AUG = M_DV + 128
GATE_CAP = 15.0

A_HEADS = 16
A_KV = 4
A_GRP = 4
A_HD = 64
ROPE_THETA = 10000.0
GRID_W = 64
LOG2E = 1.4426950408889634

N_EXP = 16
EC_CAPACITY = 2
WIN = 48
GATE_LANES = 128
XW = D + GATE_LANES
HALF = D // 2
XP = HALF + GATE_LANES
HIGH16 = -65536

LANE = 128
SUB = 8
VMEM_LIMIT = 56 * 1024 * 1024


def _cparams(sem):
    return pltpu.CompilerParams(dimension_semantics=sem, vmem_limit_bytes=VMEM_LIMIT)


def _dot(a, b):
    return jnp.dot(a, b, preferred_element_type=F32)


def _dot_nt(a, b):
    return lax.dot_general(a, b, (((1,), (1,)), ((), ())), preferred_element_type=F32)


def _split2(x):
    hi = x.astype(BF16)
    lo = (x - hi.astype(F32)).astype(BF16)
    return hi, lo


def _split3(x):
    hi = x.astype(BF16)
    r = x - hi.astype(F32)
    mid = r.astype(BF16)
    lo = (r - mid.astype(F32)).astype(BF16)
    return hi, mid, lo


def _dot3(a, b):
    ah, al = _split2(a)
    bh, bl = _split2(b)
    return _dot(ah, bh) + (_dot(ah, bl) + _dot(al, bh))


def _rms_mod(x, gain, shift, scale):
    ms = jnp.mean(x * x, axis=-1, keepdims=True)
    y = x * lax.rsqrt(ms + EPS) * gain
    return y * (1.0 + scale) + shift


def _iota(shape, dim):
    return lax.broadcasted_iota(I32, shape, dim)


def _mod_index(nb):
    return lambda b, t: (jnp.where(t == 0, nb, b), 0, 0)


def _mod_kernel(c_ref, w_ref, b_ref, o_ref):
    c = c_ref[...]
    s = c * jax.nn.sigmoid(c)
    o_ref[...] = _dot3(s, w_ref[...]) + b_ref[...]


def _modulation(cc, w_mod, b_mod):
    depth, _, n6 = w_mod.shape
    rb = cc.shape[0]
    nj = n6 // D
    out = pl.pallas_call(
        _mod_kernel,
        out_shape=jax.ShapeDtypeStruct((depth, rb, n6), F32),
        grid=(depth, nj),
        in_specs=[
            pl.BlockSpec((rb, D), lambda i, j: (0, 0)),
            pl.BlockSpec((None, D, D), lambda i, j: (i, 0, j)),
            pl.BlockSpec((None, 1, D), lambda i, j: (i, 0, j)),
        ],
        out_specs=pl.BlockSpec((None, rb, D), lambda i, j: (i, 0, j)),
        compiler_params=_cparams(("arbitrary", "arbitrary")),
        name="adaln_mod",
    )(cc, w_mod, b_mod.reshape(depth, 1, n6))
    return out.reshape(depth, rb, nj, D)


def _gate_act(g, idx):
    g = GATE_CAP * jnp.tanh(g * (1.0 / GATE_CAP))
    logsig = jnp.minimum(g, 0.0) - jnp.log(1.0 + jnp.exp(-jnp.abs(g)))
    is_forget = ((idx >> 2) & 1) == 1
    return jnp.where(is_forget, logsig, g)


def _proj_mlstm_kernel(x_ref, mod_ref, gain_ref, w_ref, wkt_ref, wg_ref, wgt_ref, bc_ref, br_ref,
                       q_ref, kt_ref, v_ref, o_ref, gc_ref, gr_ref):
    m = mod_ref[0]
    h = _rms_mod(x_ref[0], gain_ref[...], m[0:1], m[1:2])
    hb = h.astype(BF16)
    r = _dot(hb, w_ref[...])
    q_ref[0] = (r[:, :M_QK] * (M_DK ** -0.5)).astype(BF16)
    v_ref[0] = r[:, M_QK:M_QK + M_V].astype(BF16)
    o_ref[0] = r[:, M_QK + M_V:].astype(BF16)
    kt_ref[0] = _dot_nt(wkt_ref[...], hb).astype(BF16)
    gc = _dot(hb, wg_ref[...]) + bc_ref[...]
    gc_ref[0] = _gate_act(gc, _iota(gc.shape, 1))
    gr = _dot_nt(wgt_ref[...], hb) + br_ref[...]
    gr_ref[0] = _gate_act(gr, _iota(gr.shape, 0))


def _proj_mlstm(xa, mod_i, gain, w_in, b_gate):
    nb, ntot, _ = xa.shape
    nt = ntot // TILE
    n_g = 4 * M_HEADS
    wq = w_in[:, :M_QK]
    wk = w_in[:, M_QK:2 * M_QK]
    wvo = w_in[:, 2 * M_QK:2 * M_QK + 2 * M_V]
    wg = w_in[:, 2 * M_QK + 2 * M_V:]
    w_main = jnp.concatenate([wq, wvo], axis=1).astype(BF16)
    wkt = wk.T.astype(BF16)
    wg_pad = jnp.pad(wg, ((0, 0), (0, LANE - n_g))).astype(BF16)
    wgt = wg.T.astype(BF16)
    bc = jnp.pad(b_gate, (0, LANE - n_g)).reshape(1, LANE)
    br = b_gate.reshape(n_g, 1)
    tok = lambda w: pl.BlockSpec((1, TILE, w), lambda b, t: (b, t, 0))
    full = lambda a: pl.BlockSpec(a.shape, lambda b, t: (0,) * a.ndim)
    return pl.pallas_call(
        _proj_mlstm_kernel,
        out_shape=(
            jax.ShapeDtypeStruct((nb, ntot, M_QK), BF16),
            jax.ShapeDtypeStruct((nb, M_QK, ntot), BF16),
            jax.ShapeDtypeStruct((nb, ntot, M_V), BF16),
            jax.ShapeDtypeStruct((nb, ntot, M_V), BF16),
            jax.ShapeDtypeStruct((nb, ntot, LANE), F32),
            jax.ShapeDtypeStruct((nb, n_g, ntot), F32),
        ),
        grid=(nb, nt),
        in_specs=[tok(D), pl.BlockSpec((1, 6, D), _mod_index(nb)), full(gain), full(w_main), full(wkt),
                  full(wg_pad), full(wgt), full(bc), full(br)],
        out_specs=(tok(M_QK), pl.BlockSpec((1, M_QK, TILE), lambda b, t: (b, 0, t)), tok(M_V), tok(M_V),
                   tok(LANE), pl.BlockSpec((1, n_g, TILE), lambda b, t: (b, 0, t))),
        compiler_params=_cparams(("arbitrary", "arbitrary")),
        name="proj_mlstm",
    )(xa, mod_i, gain, w_main, wkt, wg_pad, wgt, bc, br)


def _mlstm_kernel(qf_ref, qb_ref, kf_ref, kb_ref, vf_ref, vb_ref, gcf_ref, gcb_ref, grf_ref, grb_ref,
                  hf_ref, hb_ref, c_ref, m_ref):
    t = pl.program_id(1)

    @pl.when(t == 0)
    def _():
        c_ref[...] = jnp.zeros(c_ref.shape, F32)
        m_ref[...] = jnp.zeros(m_ref.shape, F32)

    n = TILE
    row = _iota((n, n), 0)
    col = _iota((n, n), 1)
    lower = col <= row
    upper = col >= row
    lower_b = lower.astype(BF16)
    upper_b = upper.astype(BF16)
    ones_col = (_iota((n, M_AUG - M_DV), 1) == 0).astype(BF16)
    dirs = ((qf_ref, kf_ref, vf_ref, gcf_ref, grf_ref, hf_ref, lower, lower_b, upper_b, n - 1),
            (qb_ref, kb_ref, vb_ref, gcb_ref, grb_ref, hb_ref, upper, upper_b, lower_b, 0))
    for d, (q_ref, k_ref, v_ref, gc_ref, gr_ref, o_ref, mask, cum_l, cum_r, last) in enumerate(dirs):
        gc = gc_ref[0]
        gr = gr_ref[0]
        bcol = sum(_dot(cum_l, p) for p in _split3(gc))
        brow = sum(_dot(p, cum_r) for p in _split3(gr))
        for h in range(M_HEADS):
            gi = 8 * d + h
            gf = gi + 4
            sidx = 4 * d + h
            b_col = bcol[:, gf:gf + 1]
            b_row = brow[gf:gf + 1, :]
            ig_row = gr[gi:gi + 1, :]
            total = b_row[:, last:last + 1]
            m_st = m_ref[sidx][0:1, 0:1]
            key_row = ig_row - b_row
            log_d = jnp.where(mask, b_col + key_row, -jnp.inf)
            m_inter = b_col + m_st
            m_q = jnp.maximum(m_inter, jnp.max(log_d, axis=1, keepdims=True))
            w_intra = jnp.exp(log_d - m_q)
            w_inter = jnp.exp(m_inter - m_q)
            qh = q_ref[0, :, M_DK * h:M_DK * (h + 1)]
            kth = k_ref[0, M_DK * h:M_DK * (h + 1), :]
            v_aug = jnp.concatenate([v_ref[0, :, M_DV * h:M_DV * (h + 1)], ones_col], axis=1)
            c_aug = c_ref[sidx]
            s = (_dot(qh, kth) * w_intra).astype(BF16)
            nd = _dot(s, v_aug) + w_inter * _dot(qh, c_aug.astype(BF16))
            den = nd[:, M_DV:M_DV + 1]
            inv = 1.0 / jnp.maximum(jnp.abs(den), jnp.exp(-m_q))
            o_ref[0, :, M_DV * h:M_DV * (h + 1)] = (nd[:, :M_DV] * inv).astype(o_ref.dtype)
            log_w = total + key_row
            m_new = jnp.maximum(total + m_st, jnp.max(log_w, axis=1, keepdims=True))
            w_key = jnp.exp(log_w - m_new)
            decay = jnp.exp(total + m_st - m_new)
            kw = (kth.astype(F32) * w_key).astype(BF16)
            c_ref[sidx] = decay * c_aug + _dot(kw, v_aug)
            m_ref[sidx] = jnp.broadcast_to(m_new, m_ref.shape[1:])


def _mlstm_scan(q, kt, v, gc, gr):
    nb, ntot, _ = q.shape
    nt = ntot // TILE
    fwd = lambda b, t: (b, t, 0)
    bwd = lambda b, t: (b, jnp.where(t == 0, 0, nt - t), 0)
    fwd_t = lambda b, t: (b, 0, t)
    bwd_t = lambda b, t: (b, 0, jnp.where(t == 0, 0, nt - t))
    n_g = gr.shape[1]
    return pl.pallas_call(
        _mlstm_kernel,
        out_shape=(jax.ShapeDtypeStruct((nb, ntot, M_V), BF16), jax.ShapeDtypeStruct((nb, ntot, M_V), BF16)),
        grid=(nb, nt),
        in_specs=[
            pl.BlockSpec((1, TILE, M_QK), fwd), pl.BlockSpec((1, TILE, M_QK), bwd),
            pl.BlockSpec((1, M_QK, TILE), fwd_t), pl.BlockSpec((1, M_QK, TILE), bwd_t),
            pl.BlockSpec((1, TILE, M_V), fwd), pl.BlockSpec((1, TILE, M_V), bwd),
            pl.BlockSpec((1, TILE, LANE), fwd), pl.BlockSpec((1, TILE, LANE), bwd),
            pl.BlockSpec((1, n_g, TILE), fwd_t), pl.BlockSpec((1, n_g, TILE), bwd_t),
        ],
        out_specs=(pl.BlockSpec((1, TILE, M_V), fwd), pl.BlockSpec((1, TILE, M_V), bwd)),
        scratch_shapes=[pltpu.VMEM((2 * M_HEADS, M_DK, M_AUG), F32), pltpu.VMEM((2 * M_HEADS, SUB, LANE), F32)],
        compiler_params=_cparams(("arbitrary", "arbitrary")),
        name="mlstm_scan",
    )(q, q, kt, kt, v, v, gc, gc, gr, gr)


def _head_norm(x, gain):
    w = x.shape[1]
    gsum = ((_iota((w, LANE), 0) >> 6) == _iota((w, LANE), 1)).astype(BF16)
    gexp = ((_iota((LANE, w), 1) >> 6) == _iota((LANE, w), 0)).astype(BF16)
    hi, lo = _split2(x * x)
    ssum = _dot(hi, gsum) + _dot(lo, gsum)
    rh, rl = _split2(lax.rsqrt(ssum * (1.0 / A_HD) + EPS))
    return x * (_dot(rh, gexp) + _dot(rl, gexp)) * gain


def _rope(x, cos, sin_signed):
    first = (_iota((x.shape[0], LANE), 1) & (A_HD - 1)) < A_HD // 2
    tiles = []
    for i in range(x.shape[1] // LANE):
        xt = x[:, LANE * i:LANE * (i + 1)]
        partner = jnp.where(first, pltpu.roll(xt, LANE - A_HD // 2, 1), pltpu.roll(xt, A_HD // 2, 1))
        tiles.append(xt * cos + partner * sin_signed)
    return jnp.concatenate(tiles, axis=1)


def _proj_attn_kernel(x_ref, mod_ref, gain_ref, w_ref, qg_ref, kg_ref, cos_ref, sin_ref, shift_ref,
                      q_ref, kt_ref, vd_ref):
    m = mod_ref[0]
    h = _rms_mod(x_ref[0], gain_ref[...], m[0:1], m[1:2])
    r = _dot(h.astype(BF16), w_ref[...])
    nq = A_HEADS * A_HD
    nk = A_KV * A_HD
    cos = cos_ref[...]
    sin = sin_ref[...]
    qn = _rope(_head_norm(r[:, :nq], qg_ref[...]), cos, sin)
    kn = _rope(_head_norm(r[:, nq:nq + nk], kg_ref[...]), cos, sin)
    v = r[:, nq + nk:]
    lane = _iota((TILE, LANE), 1)
    low = lane < A_HD
    one_hot = jnp.where(lane == A_HD, 1.0, 0.0)

    def head_tile(x, idx):
        tile = x[:, LANE * (idx // 2):LANE * (idx // 2 + 1)]
        if idx % 2 == 1:
            tile = pltpu.roll(tile, A_HD, 1)
        return jnp.where(low, tile, one_hot).astype(BF16)

    for hd in range(A_HEADS):
        q_ref[0, hd] = head_tile(qn, hd)
    for g in range(A_KV):
        vd_ref[0, g] = head_tile(v, g)
    kt = jnp.transpose(kn)
    extra = jnp.where(_iota((LANE - A_HD, TILE), 0) == 0, shift_ref[...], 0.0).astype(BF16)
    for g in range(A_KV):
        kt_ref[0, g, 0:A_HD, :] = kt[A_HD * g:A_HD * (g + 1), :].astype(BF16)
        kt_ref[0, g, A_HD:LANE, :] = extra


def _rope_tables(n_lat, n_ctx):
    rows = n_lat // GRID_W
    row = jnp.repeat(jnp.arange(rows, dtype=F32), GRID_W)
    col = jnp.tile(jnp.arange(GRID_W, dtype=F32), rows)
    pairs = A_HD // 4
    inv = ROPE_THETA ** (-jnp.arange(pairs, dtype=F32) / pairs)
    ang = jnp.concatenate([row[:, None] * inv, col[:, None] * inv], axis=-1)
    c = jnp.cos(ang)
    s = jnp.sin(ang)
    cos = jnp.concatenate([c, c, c, c], axis=-1)
    sin = jnp.concatenate([-s, s, -s, s], axis=-1)
    cos = jnp.concatenate([jnp.ones((n_ctx, LANE), F32), cos], axis=0)
    sin = jnp.concatenate([jnp.zeros((n_ctx, LANE), F32), sin], axis=0)
    return cos, sin


SHIFT_LIMIT = 60.0


def _softmax_shift(qg, kg):
    bound = A_HD * jnp.max(jnp.abs(qg)) * jnp.max(jnp.abs(kg))
    fast = bound <= SHIFT_LIMIT
    shift = jnp.where(fast, jnp.ceil(bound), 0.0)
    return shift, jnp.logical_not(fast).astype(I32)


def _proj_attn(xa, mod_i, gain, w_in, q_norm, k_norm, cos, sin):
    nb, ntot, _ = xa.shape
    nt = ntot // TILE
    nq = A_HEADS * A_HD
    nk = A_KV * A_HD
    w = w_in.astype(BF16)
    qg = (jnp.tile(q_norm, A_HEADS) * (A_HD ** -0.5 * LOG2E)).reshape(1, nq)
    kg = jnp.tile(k_norm, A_KV).reshape(1, nk)
    shift, flag = _softmax_shift(qg, kg)
    neg_shift = (-shift).reshape(1, 1).astype(F32)
    full = lambda a: pl.BlockSpec(a.shape, lambda b, t: (0,) * a.ndim)
    tab = pl.BlockSpec((TILE, LANE), lambda b, t: (t, 0))
    q, kt, vd = pl.pallas_call(
        _proj_attn_kernel,
        out_shape=(
            jax.ShapeDtypeStruct((nb, A_HEADS, ntot, LANE), BF16),
            jax.ShapeDtypeStruct((nb, A_KV, LANE, ntot), BF16),
            jax.ShapeDtypeStruct((nb, A_KV, ntot, LANE), BF16),
        ),
        grid=(nb, nt),
        in_specs=[pl.BlockSpec((1, TILE, D), lambda b, t: (b, t, 0)), pl.BlockSpec((1, 6, D), _mod_index(nb)),
                  full(gain), full(w), full(qg), full(kg), tab, tab, full(neg_shift)],
        out_specs=(pl.BlockSpec((1, A_HEADS, TILE, LANE), lambda b, t: (b, 0, t, 0)),
                   pl.BlockSpec((1, A_KV, LANE, TILE), lambda b, t: (b, 0, 0, t)),
                   pl.BlockSpec((1, A_KV, TILE, LANE), lambda b, t: (b, 0, t, 0))),
        compiler_params=_cparams(("arbitrary", "arbitrary")),
        name="proj_attn",
    )(xa, mod_i, gain, w, qg, kg, cos, sin, neg_shift)
    return q, kt, vd, flag.reshape(1)


def _attn_kernel(flag_ref, q_ref, kt_ref, vd_ref, o_ref, *, ntot):
    low = _iota((TILE, LANE), 1) < A_HD

    def attend(row0, nk, row_max):
        outs = []
        for hd in range(A_GRP):
            s = _dot(q_ref[0, hd, pl.ds(row0, TILE), :], kt_ref[0, 0, :, :nk])
            if row_max:
                s = s - jnp.max(s, axis=1, keepdims=True)
            r = _dot(jnp.exp2(s).astype(BF16), vd_ref[0, 0, :nk, :])
            outs.append(r / r[:, A_HD:A_HD + 1])
        t0 = jnp.where(low, outs[0], pltpu.roll(outs[1], A_HD, 1))
        t1 = jnp.where(low, outs[2], pltpu.roll(outs[3], A_HD, 1))
        o_ref[0, 0, pl.ds(row0, TILE), :] = jnp.concatenate([t0, t1], axis=1).astype(BF16)

    attend(0, TILE, True)

    def latent_tiles(row_max):
        def body(i, carry):
            attend(pl.multiple_of(i * TILE, TILE), ntot, row_max)
            return carry
        lax.fori_loop(1, ntot // TILE, body, 0)

    @pl.when(flag_ref[0] == 0)
    def _():
        latent_tiles(False)

    @pl.when(flag_ref[0] != 0)
    def _():
        latent_tiles(True)


def _attention(q, kt, vd, flag):
    nb, _, ntot, _ = q.shape
    grid_spec = pltpu.PrefetchScalarGridSpec(
        num_scalar_prefetch=1,
        grid=(nb, A_KV),
        in_specs=[pl.BlockSpec((1, A_GRP, ntot, LANE), lambda b, g, f: (b, g, 0, 0)),
                  pl.BlockSpec((1, 1, LANE, ntot), lambda b, g, f: (b, g, 0, 0)),
                  pl.BlockSpec((1, 1, ntot, LANE), lambda b, g, f: (b, g, 0, 0))],
        out_specs=pl.BlockSpec((1, 1, ntot, A_GRP * A_HD), lambda b, g, f: (b, g, 0, 0)),
    )
    return pl.pallas_call(
        functools.partial(_attn_kernel, ntot=ntot),
        out_shape=jax.ShapeDtypeStruct((nb, A_KV, ntot, A_GRP * A_HD), BF16),
        grid_spec=grid_spec,
        compiler_params=_cparams(("arbitrary", "arbitrary")),
        name="attention",
    )(flag, q, kt, vd)


def _route_tail(x, y, m, gain_ref, wrt_ref, x1_ref, h2_ref, aff_ref):
    x1 = x + m[2:3] * y
    x1_ref[0] = x1
    h2 = _rms_mod(x1, gain_ref[...], m[3:4], m[4:5])
    h2_ref[0] = h2.astype(BF16)
    logits = _dot3_nt(wrt_ref[...], h2)
    e = jnp.exp(logits - jnp.max(logits, axis=0, keepdims=True))
    aff_ref[0] = e / jnp.sum(e, axis=0, keepdims=True)


def _dot3_nt(a, b):
    ah, al = _split2(a)
    bh, bl = _split2(b)
    return _dot_nt(ah, bh) + (_dot_nt(ah, bl) + _dot_nt(al, bh))


def _readout_mlstm_kernel(x_ref, mod_ref, hf_ref, hb_ref, o_ref, mn_ref, wo_ref, gain_ref, wrt_ref,
                          x1_ref, h2_ref, aff_ref):
    hh = hf_ref[0].astype(F32) + hb_ref[0].astype(F32)
    mn = mn_ref[...]
    parts = []
    for h in range(M_HEADS):
        seg = hh[:, M_DV * h:M_DV * (h + 1)]
        ms = jnp.mean(seg * seg, axis=-1, keepdims=True)
        parts.append(seg * lax.rsqrt(ms + EPS) * mn[:, M_DV * h:M_DV * (h + 1)])
    z = jnp.concatenate(parts, axis=1) * jax.nn.sigmoid(o_ref[0].astype(F32))
    y = _dot(z.astype(BF16), wo_ref[...])
    _route_tail(x_ref[0], y, mod_ref[0], gain_ref, wrt_ref, x1_ref, h2_ref, aff_ref)


def _readout_attn_kernel(x_ref, mod_ref, oa_ref, wo_ref, gain_ref, wrt_ref, x1_ref, h2_ref, aff_ref):
    gw = A_GRP * A_HD
    y = _dot(oa_ref[0, 0], wo_ref[0:gw, :])
    for g in range(1, A_KV):
        y = y + _dot(oa_ref[0, g], wo_ref[gw * g:gw * (g + 1), :])
    _route_tail(x_ref[0], y, mod_ref[0], gain_ref, wrt_ref, x1_ref, h2_ref, aff_ref)


def _readout(xa, mod_i, mixer_out, w_out, gain_ffn, w_router, mlstm_norm=None):
    nb, ntot, _ = xa.shape
    nt = ntot // TILE
    tok = lambda w: pl.BlockSpec((1, TILE, w), lambda b, t: (b, t, 0))
    full = lambda a: pl.BlockSpec(a.shape, lambda b, t: (0,) * a.ndim)
    wo = w_out.astype(BF16)
    wrt = w_router.T
    common_in = [tok(D), pl.BlockSpec((1, 6, D), _mod_index(nb))]
    tail_in = [full(wo), full(gain_ffn), full(wrt)]
    if mlstm_norm is not None:
        hf, hb, og = mixer_out
        kern = _readout_mlstm_kernel
        mid_in = [tok(M_V), tok(M_V), tok(M_V), full(mlstm_norm)]
        args = (xa, mod_i, hf, hb, og, mlstm_norm, wo, gain_ffn, wrt)
    else:
        kern = _readout_attn_kernel
        mid_in = [pl.BlockSpec((1, A_KV, TILE, A_GRP * A_HD), lambda b, t: (b, 0, t, 0))]
        args = (xa, mod_i, mixer_out, wo, gain_ffn, wrt)
    return pl.pallas_call(
        kern,
        out_shape=(jax.ShapeDtypeStruct((nb, ntot, D), F32), jax.ShapeDtypeStruct((nb, ntot, D), BF16),
                   jax.ShapeDtypeStruct((nb, N_EXP, ntot), F32)),
        grid=(nb, nt),
        in_specs=common_in + mid_in + tail_in,
        out_specs=(tok(D), tok(D), pl.BlockSpec((1, N_EXP, TILE), lambda b, t: (b, 0, t))),
        compiler_params=_cparams(("arbitrary", "arbitrary")),
        name="readout_route",
    )(*args)


def _slot_geometry(n_ctx, n_lat):
    cap_ctx = max(1, EC_CAPACITY * n_ctx // N_EXP)
    cap_lat = max(1, EC_CAPACITY * n_lat // N_EXP)
    lat_base = -(-cap_ctx // SUB) * SUB
    slots_max = lat_base + cap_lat + (SUB - 1) * (n_lat // TILE)
    rows_ffn = -(-slots_max // 16) * 16
    return cap_ctx, cap_lat, lat_base, rows_ffn


def _topk_kernel(a_ref, slot_ref, slot_t_ref, aff_t_ref, off_ref, cnt_ref, *, n_ctx, cap_ctx, cap_lat, lat_base):
    bits = lax.bitcast_convert_type(a_ref[0], I32)
    prefix = (_iota((LANE, LANE), 0) <= _iota((LANE, LANE), 1)).astype(BF16)

    def count(mask):
        return jnp.sum(mask.astype(F32), axis=1, keepdims=True)

    def select(x, cap):
        thr = jnp.zeros((N_EXP, 1), I32)
        for bit in range(30, -1, -1):
            cand = thr | (1 << bit)
            thr = jnp.where(count(x >= cand) >= cap, cand, thr)
        gt = x > thr
        eq = x == thr
        need = cap - count(gt)
        run = jnp.zeros((N_EXP, 1), F32)
        blocks = []
        for j in range(x.shape[1] // LANE):
            sl = slice(LANE * j, LANE * (j + 1))
            eqf = eq[:, sl].astype(F32)
            inc = _dot(eqf.astype(BF16), prefix)
            rank = run + inc - eqf
            blocks.append(jnp.logical_or(gt[:, sl], jnp.logical_and(eq[:, sl], rank < need)))
            run = run + inc[:, LANE - 1:LANE]
        return blocks

    blocks = select(bits[:, :n_ctx], cap_ctx) + select(bits[:, n_ctx:], cap_lat)
    per_tile = TILE // LANE
    lane = _iota((N_EXP, LANE), 1)
    off_acc = jnp.zeros((N_EXP, LANE), I32)
    cnt_acc = jnp.zeros((N_EXP, LANE), I32)
    base = jnp.zeros((N_EXP, 1), F32)
    for t in range(len(blocks) // per_tile):
        if t == n_ctx // TILE:
            base = jnp.full((N_EXP, 1), float(lat_base), F32)
        run = jnp.zeros((N_EXP, 1), F32)
        for j in range(per_tile):
            blk = blocks[per_tile * t + j]
            sf = blk.astype(F32)
            inc = _dot(sf.astype(BF16), prefix)
            pos = base + run + inc - sf
            c0 = LANE * (per_tile * t + j)
            slot_blk = jnp.concatenate([jnp.where(blk, pos.astype(I32), -1),
                                        jnp.full((LANE - N_EXP, LANE), -1, I32)], axis=0)
            slot_ref[0, :, c0:c0 + LANE] = slot_blk
            slot_t_ref[0, c0:c0 + LANE, :] = jnp.transpose(slot_blk)
            aff_blk = jnp.concatenate([a_ref[0, :, c0:c0 + LANE], jnp.zeros((LANE - N_EXP, LANE), F32)], axis=0)
            aff_t_ref[0, c0:c0 + LANE, :] = jnp.transpose(aff_blk)
            run = run + inc[:, LANE - 1:LANE]
        n8 = jnp.floor((run + (SUB - 1)) * (1.0 / SUB)) * SUB
        off_acc = jnp.where(lane == t, base.astype(I32), off_acc)
        cnt_acc = jnp.where(lane == t, n8.astype(I32), cnt_acc)
        base = base + n8
    off_ref[0] = off_acc
    cnt_ref[0] = cnt_acc


def _topk(aff_t, n_ctx):
    nb, _, ntot = aff_t.shape
    cap_ctx, cap_lat, lat_base, _ = _slot_geometry(n_ctx, ntot - n_ctx)
    kern = functools.partial(_topk_kernel, n_ctx=n_ctx, cap_ctx=cap_ctx, cap_lat=cap_lat, lat_base=lat_base)
    return pl.pallas_call(
        kern,
        out_shape=(jax.ShapeDtypeStruct((nb, LANE, ntot), I32), jax.ShapeDtypeStruct((nb, ntot, LANE), I32),
                   jax.ShapeDtypeStruct((nb, ntot, LANE), F32), jax.ShapeDtypeStruct((nb, N_EXP, LANE), I32),
                   jax.ShapeDtypeStruct((nb, N_EXP, LANE), I32)),
        grid=(nb,),
        in_specs=[pl.BlockSpec((1, N_EXP, ntot), lambda b: (b, 0, 0))],
        out_specs=(pl.BlockSpec((1, LANE, ntot), lambda b: (b, 0, 0)),
                   pl.BlockSpec((1, ntot, LANE), lambda b: (b, 0, 0)),
                   pl.BlockSpec((1, ntot, LANE), lambda b: (b, 0, 0)),
                   pl.BlockSpec((1, N_EXP, LANE), lambda b: (b, 0, 0)),
                   pl.BlockSpec((1, N_EXP, LANE), lambda b: (b, 0, 0))),
        compiler_params=_cparams(("arbitrary",)),
        name="ec_topk",
    )(aff_t)


def _strip_index(shape, dim):
    i = _iota(shape, dim)
    e = jnp.floor((i.astype(F32) + 0.5) * (1.0 / WIN)).astype(I32)
    return e, i - WIN * e


def _dispatch_kernel(off_s, rounds_s, h_ref, slot_ref, aff_ref, xg_ref, stage_ref, sem, cnt_ref):
    b = pl.program_id(0)
    t = pl.program_id(1)
    nt = pl.num_programs(1)
    base = (b * nt + t) * N_EXP
    rows = N_EXP * WIN
    a = aff_ref[0]
    src = _iota((LANE, LANE), 0)
    dst = _iota((LANE, LANE), 1)
    a3 = None
    for p, piece in enumerate(_split3(a)):
        sel = jnp.logical_and(dst == 3 * src + p, src < N_EXP).astype(BF16)
        term = _dot(piece, sel)
        a3 = term if a3 is None else a3 + term
    rhs = jnp.concatenate([h_ref[0], a3.astype(BF16)], axis=1)
    e_of_row, _ = _strip_index((rows, LANE), 0)
    expand = (e_of_row == _iota((rows, LANE), 1)).astype(BF16)
    _, j_row = _strip_index((rows, 1), 0)
    sl = slot_ref[0]
    e_row = _iota((LANE, 1), 0)
    off_v = jnp.zeros((LANE, 1), I32)
    for e in range(N_EXP):
        off_v = jnp.where(e_row == e, off_s[base + e], off_v)

    @pl.when(jnp.logical_and(b == 0, t == 0))
    def _():
        cnt_ref[0] = 0

    def strip_copies(buf, sample, starts):
        return [pltpu.make_async_copy(stage_ref.at[buf, pl.ds(WIN * e, WIN), :],
                                      xg_ref.at[sample, e, pl.ds(starts[e], WIN), :], sem.at[buf])
                for e in range(N_EXP)]

    def wait_strips(buf):
        for cp in strip_copies(buf, 0, [0] * N_EXP):
            cp.wait()

    def round_body(r, carry):
        n = cnt_ref[0]
        buf = n & 1
        rel = jnp.clip(sl - (off_v + WIN * r), -1, WIN).astype(F32).astype(BF16)
        relx = _dot(expand, rel)
        onehot = (relx == j_row.astype(F32)).astype(BF16)
        bits = lax.bitcast_convert_type(_dot(onehot, rhs), I32)
        packed = jnp.bitwise_or(jnp.bitwise_and(bits[:, :HALF] >> 16, 0xFFFF),
                                jnp.bitwise_and(bits[:, HALF:D], HIGH16))
        stage_ref[buf] = jnp.concatenate([packed, bits[:, D:]], axis=1)

        @pl.when(n > 0)
        def _():
            wait_strips(1 - buf)

        last_start = xg_ref.shape[2] - WIN
        starts = [pl.multiple_of(jnp.minimum(off_s[base + e] + WIN * r, last_start), SUB) for e in range(N_EXP)]
        for cp in strip_copies(buf, b, starts):
            cp.start()
        cnt_ref[0] = n + 1
        return carry

    lax.fori_loop(0, rounds_s[b * nt + t], round_body, 0)

    @pl.when(jnp.logical_and(b == pl.num_programs(0) - 1, t == nt - 1))
    def _():
        wait_strips((cnt_ref[0] - 1) & 1)


def _dispatch(h2, slot_pad, aff_pad, off_flat, rounds_flat, rows_alloc):
    nb, ntot, _ = h2.shape
    nt = ntot // TILE
    grid_spec = pltpu.PrefetchScalarGridSpec(
        num_scalar_prefetch=2,
        grid=(nb, nt),
        in_specs=[pl.BlockSpec((1, TILE, D), lambda b, t, o, r: (b, t, 0)),
                  pl.BlockSpec((1, LANE, TILE), lambda b, t, o, r: (b, 0, t)),
                  pl.BlockSpec((1, TILE, LANE), lambda b, t, o, r: (b, t, 0))],
        out_specs=pl.BlockSpec(memory_space=pl.ANY),
        scratch_shapes=[pltpu.VMEM((2, N_EXP * WIN, XP), I32), pltpu.SemaphoreType.DMA((2,)),
                        pltpu.SMEM((1,), I32)],
    )
    return pl.pallas_call(
        _dispatch_kernel,
        out_shape=jax.ShapeDtypeStruct((nb, N_EXP, rows_alloc, XP), I32),
        grid_spec=grid_spec,
        compiler_params=_cparams(("arbitrary", "arbitrary")),
        name="ec_dispatch",
    )(off_flat, rounds_flat, h2, slot_pad, aff_pad)


def _ffn_kernel(xg_ref, wg32_ref, wu32_ref, wd32_ref, y_ref, wg_ref, wu_ref, wd_ref):
    e = pl.program_id(0)

    @pl.when(pl.program_id(1) == 0)
    def _():
        wg_ref[0] = wg32_ref[0].astype(BF16)
        wu_ref[0] = wu32_ref[0].astype(BF16)
        wd_ref[0] = wd32_ref[0].astype(BF16)

    words = xg_ref[0, 0, :, :HALF]
    x = jnp.concatenate([lax.bitcast_convert_type(words << 16, F32),
                         lax.bitcast_convert_type(jnp.bitwise_and(words, HIGH16), F32)], axis=1).astype(BF16)
    gl = lax.bitcast_convert_type(xg_ref[0, 0, :, HALF:], F32)
    lane = _iota(gl.shape, 1)
    mine = jnp.logical_and(lane >= 3 * e, lane < 3 * e + 3)
    gate = jnp.sum(jnp.where(mine, gl, 0.0), axis=1, keepdims=True)
    a = _dot(x, wg_ref[0])
    u = _dot(x, wu_ref[0])
    hm = (a * jax.nn.sigmoid(a) * u).astype(BF16)
    y_ref[0, 0] = _dot(hm, wd_ref[0]) * gate


def _expert_ffn(xg, wg, wu, wd, rows_ffn, layer=0):
    nb = xg.shape[0]
    if wg.ndim == 4:
        wspec = pl.BlockSpec((None, 1, D, D), lambda e, b: (layer, e, 0, 0))
    else:
        wspec = pl.BlockSpec((1, D, D), lambda e, b: (e, 0, 0))
    return pl.pallas_call(
        _ffn_kernel,
        out_shape=jax.ShapeDtypeStruct((nb, N_EXP, rows_ffn, D), F32),
        grid=(N_EXP, nb),
        in_specs=[pl.BlockSpec((1, 1, rows_ffn, XP), lambda e, b: (b, e, 0, 0)), wspec, wspec, wspec],
        out_specs=pl.BlockSpec((1, 1, rows_ffn, D), lambda e, b: (b, e, 0, 0)),
        scratch_shapes=[pltpu.VMEM((1, D, D), BF16)] * 3,
        compiler_params=_cparams(("arbitrary", "arbitrary")),
        name="ec_ffn",
    )(xg, wg, wu, wd)


def _combine_kernel(off_s, rounds_s, x_ref, mod_ref, slot_ref, fin_ref, y_ref, o_ref, strip_ref, sem, *,
                    final_ctx_tiles):
    b = pl.program_id(0)
    t = pl.program_id(1)
    nt = pl.num_programs(1)
    base = (b * nt + t) * N_EXP
    cols = N_EXP * WIN
    e_of_col, _ = _strip_index((LANE, cols), 1)
    expand = (e_of_col == _iota((LANE, cols), 0)).astype(BF16)
    _, j_lane = _strip_index((1, cols), 1)
    j_lane = j_lane.astype(F32)
    sl = slot_ref[0]
    e_lane = _iota((1, LANE), 1)
    last_start = y_ref.shape[2] - WIN

    def strip_start(step_, e, r):
        return jnp.minimum(off_s[step_ * N_EXP + e] + WIN * r, last_start)

    step = b * nt + t
    buf = step & 1

    def strip_copies(step_, sample, r, dst):
        return [pltpu.make_async_copy(
            y_ref.at[sample, e, pl.ds(pl.multiple_of(strip_start(step_, e, r), SUB), WIN), :],
            strip_ref.at[dst, pl.ds(WIN * e, WIN), :], sem.at[dst]) for e in range(N_EXP)]

    @pl.when(step == 0)
    def _():
        for cp in strip_copies(step, b, 0, buf):
            cp.start()

    @pl.when(step + 1 < pl.num_programs(0) * nt)
    def _():
        for cp in strip_copies(step + 1, jnp.where(t == nt - 1, b + 1, b), 0, 1 - buf):
            cp.start()

    def expand_round(r, acc):
        first_v = jnp.zeros((1, LANE), I32)
        start_v = jnp.zeros((1, LANE), I32)
        for e in range(N_EXP):
            first_v = jnp.where(e_lane == e, off_s[base + e] + WIN * r, first_v)
            start_v = jnp.where(e_lane == e, strip_start(step, e, r), start_v)
        nominal = sl - first_v
        in_round = jnp.logical_and(nominal >= 0, nominal < WIN)
        rel = jnp.where(in_round, sl - start_v, -1).astype(F32).astype(BF16)
        relx = _dot(rel, expand)
        onehot = (relx == j_lane).astype(BF16)
        for cp in strip_copies(step, b, r, buf):
            cp.wait()
        hi, lo = _split2(strip_ref[buf])
        return acc + (_dot(onehot, hi) + _dot(onehot, lo))

    def extra_round(r, acc):
        for cp in strip_copies(step, b, r, buf):
            cp.start()
        return expand_round(r, acc)

    acc = expand_round(0, jnp.zeros((TILE, D), F32))
    acc = lax.fori_loop(1, rounds_s[step], extra_round, acc)
    x2 = x_ref[0] + mod_ref[0][5:6] * acc
    if final_ctx_tiles is None:
        o_ref[0] = x2
    else:
        @pl.when(t >= final_ctx_tiles)
        def _():
            ms = jnp.mean(x2 * x2, axis=-1, keepdims=True)
            o_ref[0] = x2 * lax.rsqrt(ms + EPS) * fin_ref[...]


def _combine(x1, mod_i, slot_t_pad, y, off_flat, rounds_flat, final_gain, final_ctx_tiles=None):
    nb, ntot, _ = x1.shape
    nt = ntot // TILE
    skip = 0 if final_ctx_tiles is None else final_ctx_tiles
    grid_spec = pltpu.PrefetchScalarGridSpec(
        num_scalar_prefetch=2,
        grid=(nb, nt),
        in_specs=[pl.BlockSpec((1, TILE, D), lambda b, t, o, r: (b, t, 0)),
                  pl.BlockSpec((1, 6, D), lambda b, t, o, r: (jnp.where(t == 0, nb, b), 0, 0)),
                  pl.BlockSpec((1, TILE, LANE), lambda b, t, o, r: (b, t, 0)),
                  pl.BlockSpec((1, D), lambda b, t, o, r: (0, 0)),
                  pl.BlockSpec(memory_space=pl.ANY)],
        out_specs=pl.BlockSpec((1, TILE, D), lambda b, t, o, r: (b, jnp.maximum(t - skip, 0), 0)),
        scratch_shapes=[pltpu.VMEM((2, N_EXP * WIN, D), F32), pltpu.SemaphoreType.DMA((2,))],
    )
    return pl.pallas_call(
        functools.partial(_combine_kernel, final_ctx_tiles=final_ctx_tiles),
        out_shape=jax.ShapeDtypeStruct((nb, ntot - skip * TILE, D), F32),
        grid_spec=grid_spec,
        compiler_params=_cparams(("arbitrary", "arbitrary")),
        name="ec_combine",
    )(off_flat, rounds_flat, x1, mod_i, slot_t_pad, final_gain, y)


def _moe(x1, h2, aff_t, mod_i, wg, wu, wd, n_ctx, final_gain, last, layer=0):
    nb, ntot, _ = x1.shape
    nt = ntot // TILE
    _, _, _, rows_ffn = _slot_geometry(n_ctx, ntot - n_ctx)
    rows_alloc = rows_ffn + WIN
    slot_pad, slot_t_pad, aff_pad, off, cnt = _topk(aff_t, n_ctx)
    off_t = jnp.swapaxes(off[:, :, :nt], 1, 2)
    cnt_t = jnp.swapaxes(cnt[:, :, :nt], 1, 2)
    rounds_c = jnp.maximum(1, jnp.max((cnt_t + WIN - 1) // WIN, axis=2))
    fill = jnp.max((rows_alloc - off_t[:, nt - 1, :] + WIN - 1) // WIN, axis=1)
    rounds_d = rounds_c.at[:, nt - 1].max(fill)
    off_flat = off_t.reshape(-1)
    xg = _dispatch(h2, slot_pad, aff_pad, off_flat, rounds_d.reshape(-1), rows_alloc)
    y = _expert_ffn(xg, wg, wu, wd, rows_ffn, layer)
    return _combine(x1, mod_i, slot_t_pad, y, off_flat, rounds_c.reshape(-1), final_gain,
                    final_ctx_tiles=n_ctx // TILE if last else None)


def kernel(x, c, ctx, c_ctx, w_mod, b_mod, norm_mix, norm_ffn, mlstm_w_in, mlstm_b_gate, mlstm_norm, mlstm_w_out,
           attn_w_in, attn_q_norm, attn_k_norm, attn_w_out, moe_router, moe_w_gate, moe_w_up, moe_w_down,
           norm_final):
    nb, n_lat, _ = x.shape
    n_ctx = ctx.shape[1]
    depth = w_mod.shape[0]
    assert n_ctx == TILE and n_lat % TILE == 0 and x.shape[2] == D
    xa = jnp.concatenate([ctx, x], axis=1)
    rb = -(-(nb + 1) // SUB) * SUB
    cc = jnp.concatenate([c, c_ctx[None, :], jnp.zeros((rb - nb - 1, D), F32)], axis=0)
    mod = _modulation(cc, w_mod, b_mod)
    cos, sin = _rope_tables(n_lat, n_ctx)
    for i in range(depth):
        j = i // 2
        mod_i = mod[i]
        gain_mix = norm_mix[i].reshape(1, D)
        gain_ffn = norm_ffn[i].reshape(1, D)
        if i % 2 == 0:
            q, kt, v, og, gc, gr = _proj_mlstm(xa, mod_i, gain_mix, mlstm_w_in[j], mlstm_b_gate[j])
            hf, hb = _mlstm_scan(q, kt, v, gc, gr)
            x1, h2, aff_t = _readout(xa, mod_i, (hf, hb, og), mlstm_w_out[j], gain_ffn, moe_router[i],
                                     mlstm_norm=mlstm_norm[j].reshape(1, M_V))
        else:
            q, kt, vd, flag = _proj_attn(xa, mod_i, gain_mix, attn_w_in[j], attn_q_norm[j], attn_k_norm[j], cos, sin)
            oa = _attention(q, kt, vd, flag)
            x1, h2, aff_t = _readout(xa, mod_i, oa, attn_w_out[j], gain_ffn, moe_router[i])
        xa = _moe(x1, h2, aff_t, mod_i, moe_w_gate, moe_w_up, moe_w_down, n_ctx,
                  norm_final.reshape(1, D), last=i == depth - 1, layer=i)
    return xa
```

```python
import functools

import jax
import jax.numpy as jnp
from jax import lax
from jax.experimental import pallas as pl
from jax.experimental.pallas import tpu as pltpu

F32 = jnp.float32
BF16 = jnp.bfloat16
I32 = jnp.int32

D = 1024
TILE = 256
EPS = 1e-6
DEPTH = 4

M_HEADS = 4
M_DK = 128
M_DV = 256
M_QK = M_HEADS * M_DK
M_V = M_HEADS * M_DV
M_AUG = M_DV + 128
GATE_CAP = 15.0

A_HEADS = 16
A_KV = 4
A_GRP = 4
A_HD = 64
ROPE_THETA = 10000.0
GRID_W = 64
LOG2E = 1.4426950408889634

N_EXP = 16
EC_CAPACITY = 2
WIN = 48
GATE_LANES = 128
XW = D + GATE_LANES
HALF = D // 2
XP = HALF + GATE_LANES
HIGH16 = -65536

LANE = 128
SUB = 8
VMEM_LIMIT = 56 * 1024 * 1024


def _cparams(sem):
    return pltpu.CompilerParams(dimension_semantics=sem, vmem_limit_bytes=VMEM_LIMIT)


def _dot(a, b):
    return jnp.dot(a, b, preferred_element_type=F32)


def _dot_nt(a, b):
    return lax.dot_general(a, b, (((1,), (1,)), ((), ())), preferred_element_type=F32)


def _split2(x):
    hi = x.astype(BF16)
    lo = (x - hi.astype(F32)).astype(BF16)
    return hi, lo


def _split3(x):
    hi = x.astype(BF16)
    r = x - hi.astype(F32)
    mid = r.astype(BF16)
    lo = (r - mid.astype(F32)).astype(BF16)
    return hi, mid, lo


def _dot3(a, b):
    ah, al = _split2(a)
    bh, bl = _split2(b)
    return _dot(ah, bh) + (_dot(ah, bl) + _dot(al, bh))


def _rms_mod(x, gain, shift, scale):
    ms = jnp.mean(x * x, axis=-1, keepdims=True)
    y = x * lax.rsqrt(ms + EPS) * gain
    return y * (1.0 + scale) + shift


def _iota(shape, dim):
    return lax.broadcasted_iota(I32, shape, dim)


def _mod_index(nb):
    return lambda b, t: (jnp.where(t == 0, nb, b), 0, 0)


def _mod_kernel(c_ref, w_ref, b_ref, o_ref):
    c = c_ref[...]
    s = c * jax.nn.sigmoid(c)
    o_ref[...] = _dot3(s, w_ref[...]) + b_ref[...]


def _modulation(cc, w_mod, b_mod):
    depth, _, n6 = w_mod.shape
    rb = cc.shape[0]
    nj = n6 // D
    out = pl.pallas_call(
        _mod_kernel,
        out_shape=jax.ShapeDtypeStruct((depth, rb, n6), F32),
        grid=(depth, nj),
        in_specs=[
            pl.BlockSpec((rb, D), lambda i, j: (0, 0)),
            pl.BlockSpec((None, D, D), lambda i, j: (i, 0, j)),
            pl.BlockSpec((None, 1, D), lambda i, j: (i, 0, j)),
        ],
        out_specs=pl.BlockSpec((None, rb, D), lambda i, j: (i, 0, j)),
        compiler_params=_cparams(("arbitrary", "arbitrary")),
        name="adaln_mod",
    )(cc, w_mod, b_mod.reshape(depth, 1, n6))
    return out.reshape(depth, rb, nj, D)


def _gate_act(g, idx):
    g = GATE_CAP * jnp.tanh(g * (1.0 / GATE_CAP))
    logsig = jnp.minimum(g, 0.0) - jnp.log(1.0 + jnp.exp(-jnp.abs(g)))
    is_forget = ((idx >> 2) & 1) == 1
    return jnp.where(is_forget, logsig, g)


def _proj_mlstm_body(x, m, gain_ref, w_ref, wkt_ref, wg_ref, wgt_ref, bc_ref, br_ref,
                     q_ref, kt_ref, v_ref, o_ref, gc_ref, gr_ref):
    h = _rms_mod(x, gain_ref[...], m[0:1], m[1:2])
    hb = h.astype(BF16)
    r = _dot(hb, w_ref[...])
    q_ref[0] = (r[:, :M_QK] * (M_DK ** -0.5)).astype(BF16)
    v_ref[0] = r[:, M_QK:M_QK + M_V].astype(BF16)
    o_ref[0] = r[:, M_QK + M_V:].astype(BF16)
    kt_ref[0] = _dot_nt(wkt_ref[...], hb).astype(BF16)
    gc = _dot(hb, wg_ref[...]) + bc_ref[...]
    gc_ref[0] = _gate_act(gc, _iota(gc.shape, 1))
    gr = _dot_nt(wgt_ref[...], hb) + br_ref[...]
    gr_ref[0] = _gate_act(gr, _iota(gr.shape, 0))


def _proj_mlstm_kernel(x_ref, mod_ref, *refs):
    _proj_mlstm_body(x_ref[0], mod_ref[0], *refs)


def _tok_spec(width):
    return pl.BlockSpec((1, TILE, width), lambda b, t, *_: (b, t, 0))


def _full_spec(a):
    return pl.BlockSpec(a.shape, lambda b, t, *_: (0,) * a.ndim)


def _mod_spec(nb):
    return pl.BlockSpec((1, 6, D), lambda b, t, *_: (jnp.where(t == 0, nb, b), 0, 0))


def _proj_mlstm_operands(nb, ntot, gain, w_in, b_gate):
    n_g = 4 * M_HEADS
    wq = w_in[:, :M_QK]
    wk = w_in[:, M_QK:2 * M_QK]
    wvo = w_in[:, 2 * M_QK:2 * M_QK + 2 * M_V]
    wg = w_in[:, 2 * M_QK + 2 * M_V:]
    w_main = jnp.concatenate([wq, wvo], axis=1).astype(BF16)
    wkt = wk.T.astype(BF16)
    wg_pad = jnp.pad(wg, ((0, 0), (0, LANE - n_g))).astype(BF16)
    wgt = wg.T.astype(BF16)
    bc = jnp.pad(b_gate, (0, LANE - n_g)).reshape(1, LANE)
    br = b_gate.reshape(n_g, 1)
    arrays = [gain, w_main, wkt, wg_pad, wgt, bc, br]
    out_shape = [
        jax.ShapeDtypeStruct((nb, ntot, M_QK), BF16),
        jax.ShapeDtypeStruct((nb, M_QK, ntot), BF16),
        jax.ShapeDtypeStruct((nb, ntot, M_V), BF16),
        jax.ShapeDtypeStruct((nb, ntot, M_V), BF16),
        jax.ShapeDtypeStruct((nb, ntot, LANE), F32),
        jax.ShapeDtypeStruct((nb, n_g, ntot), F32),
    ]
    out_specs = [_tok_spec(M_QK), pl.BlockSpec((1, M_QK, TILE), lambda b, t, *_: (b, 0, t)), _tok_spec(M_V),
                 _tok_spec(M_V), _tok_spec(LANE), pl.BlockSpec((1, n_g, TILE), lambda b, t, *_: (b, 0, t))]
    return arrays, [_full_spec(a) for a in arrays], out_shape, out_specs


def _proj_mlstm(xa, mod_i, operands):
    nb, ntot, _ = xa.shape
    arrays, in_specs, out_shape, out_specs = operands
    return pl.pallas_call(
        _proj_mlstm_kernel,
        out_shape=tuple(out_shape),
        grid=(nb, ntot // TILE),
        in_specs=[_tok_spec(D), _mod_spec(nb)] + in_specs,
        out_specs=tuple(out_specs),
        compiler_params=_cparams(("arbitrary", "arbitrary")),
        name="proj_mlstm",
    )(xa, mod_i, *arrays)


def _mlstm_kernel(qf_ref, qb_ref, kf_ref, kb_ref, vf_ref, vb_ref, gcf_ref, gcb_ref, grf_ref, grb_ref,
                  hf_ref, hb_ref, c_ref, m_ref):
    t = pl.program_id(1)

    @pl.when(t == 0)
    def _():
        c_ref[...] = jnp.zeros(c_ref.shape, F32)
        m_ref[...] = jnp.zeros(m_ref.shape, F32)

    n = TILE
    row = _iota((n, n), 0)
    col = _iota((n, n), 1)
    lower = col <= row
    upper = col >= row
    lower_b = lower.astype(BF16)
    upper_b = upper.astype(BF16)
    ones_col = (_iota((n, M_AUG - M_DV), 1) == 0).astype(BF16)
    dirs = ((qf_ref, kf_ref, vf_ref, gcf_ref, grf_ref, hf_ref, lower, lower_b, upper_b, n - 1),
            (qb_ref, kb_ref, vb_ref, gcb_ref, grb_ref, hb_ref, upper, upper_b, lower_b, 0))
    c_old = [c_ref[i] for i in range(2 * M_HEADS)]
    m_old = [m_ref[i][0:1, 0:1] for i in range(2 * M_HEADS)]
    c_new, m_new_all, h_out = {}, {}, {0: [], 1: []}
    for d, (q_ref, k_ref, v_ref, gc_ref, gr_ref, o_ref, mask, cum_l, cum_r, last) in enumerate(dirs):
        gc = gc_ref[0]
        gr = gr_ref[0]
        bcol = sum(_dot(cum_l, p) for p in _split3(gc))
        brow = sum(_dot(p, cum_r) for p in _split3(gr))
        for h in range(M_HEADS):
            gi = 8 * d + h
            gf = gi + 4
            sidx = 4 * d + h
            b_col = bcol[:, gf:gf + 1]
            b_row = brow[gf:gf + 1, :]
            ig_row = gr[gi:gi + 1, :]
            total = b_row[:, last:last + 1]
            m_st = m_old[sidx]
            key_row = ig_row - b_row
            log_d = jnp.where(mask, b_col + key_row, -jnp.inf)
            m_inter = b_col + m_st
            m_q = jnp.maximum(m_inter, jnp.max(log_d, axis=1, keepdims=True))
            w_intra = jnp.exp(log_d - m_q)
            w_inter = jnp.exp(m_inter - m_q)
            qh = q_ref[0, :, M_DK * h:M_DK * (h + 1)]
            kth = k_ref[0, M_DK * h:M_DK * (h + 1), :]
            v_aug = jnp.concatenate([v_ref[0, :, M_DV * h:M_DV * (h + 1)], ones_col], axis=1)
            c_aug = c_old[sidx]
            s = (_dot(qh, kth) * w_intra).astype(BF16)
            nd = _dot(s, v_aug) + w_inter * _dot(qh, c_aug.astype(BF16))
            den = nd[:, M_DV:M_DV + 1]
            inv = 1.0 / jnp.maximum(jnp.abs(den), jnp.exp(-m_q))
            h_out[d].append((nd[:, :M_DV] * inv).astype(o_ref.dtype))
            log_w = total + key_row
            m_new = jnp.maximum(total + m_st, jnp.max(log_w, axis=1, keepdims=True))
            w_key = jnp.exp(log_w - m_new)
            decay = jnp.exp(total + m_st - m_new)
            kw = (kth.astype(F32) * w_key).astype(BF16)
            c_new[sidx] = decay * c_aug + _dot(kw, v_aug)
            m_new_all[sidx] = m_new
    hf_ref[0] = jnp.concatenate(h_out[0], axis=1)
    hb_ref[0] = jnp.concatenate(h_out[1], axis=1)
    for i in range(2 * M_HEADS):
        c_ref[i] = c_new[i]
        m_ref[i] = jnp.broadcast_to(m_new_all[i], m_ref.shape[1:])


def _mlstm_scan(q, kt, v, gc, gr):
    nb, ntot, _ = q.shape
    nt = ntot // TILE
    fwd = lambda b, t: (b, t, 0)
    bwd = lambda b, t: (b, jnp.where(t == 0, 0, nt - t), 0)
    fwd_t = lambda b, t: (b, 0, t)
    bwd_t = lambda b, t: (b, 0, jnp.where(t == 0, 0, nt - t))
    n_g = gr.shape[1]
    return pl.pallas_call(
        _mlstm_kernel,
        out_shape=(jax.ShapeDtypeStruct((nb, ntot, M_V), BF16), jax.ShapeDtypeStruct((nb, ntot, M_V), BF16)),
        grid=(nb, nt),
        in_specs=[
            pl.BlockSpec((1, TILE, M_QK), fwd), pl.BlockSpec((1, TILE, M_QK), bwd),
            pl.BlockSpec((1, M_QK, TILE), fwd_t), pl.BlockSpec((1, M_QK, TILE), bwd_t),
            pl.BlockSpec((1, TILE, M_V), fwd), pl.BlockSpec((1, TILE, M_V), bwd),
            pl.BlockSpec((1, TILE, LANE), fwd), pl.BlockSpec((1, TILE, LANE), bwd),
            pl.BlockSpec((1, n_g, TILE), fwd_t), pl.BlockSpec((1, n_g, TILE), bwd_t),
        ],
        out_specs=(pl.BlockSpec((1, TILE, M_V), fwd), pl.BlockSpec((1, TILE, M_V), bwd)),
        scratch_shapes=[pltpu.VMEM((2 * M_HEADS, M_DK, M_AUG), F32), pltpu.VMEM((2 * M_HEADS, SUB, LANE), F32)],
        compiler_params=_cparams(("arbitrary", "arbitrary")),
        name="mlstm_scan",
    )(q, q, kt, kt, v, v, gc, gc, gr, gr)


def _head_norm(x, gain):
    w = x.shape[1]
    gsum = ((_iota((w, LANE), 0) >> 6) == _iota((w, LANE), 1)).astype(BF16)
    gexp = ((_iota((LANE, w), 1) >> 6) == _iota((LANE, w), 0)).astype(BF16)
    hi, lo = _split2(x * x)
    ssum = _dot(hi, gsum) + _dot(lo, gsum)
    rh, rl = _split2(lax.rsqrt(ssum * (1.0 / A_HD) + EPS))
    return x * (_dot(rh, gexp) + _dot(rl, gexp)) * gain


def _rope(x, cos, sin_signed):
    first = (_iota((x.shape[0], LANE), 1) & (A_HD - 1)) < A_HD // 2
    tiles = []
    for i in range(x.shape[1] // LANE):
        xt = x[:, LANE * i:LANE * (i + 1)]
        partner = jnp.where(first, pltpu.roll(xt, LANE - A_HD // 2, 1), pltpu.roll(xt, A_HD // 2, 1))
        tiles.append(xt * cos + partner * sin_signed)
    return jnp.concatenate(tiles, axis=1)


def _proj_attn_kernel(x_ref, mod_ref, *refs):
    _proj_attn_body(x_ref[0], mod_ref[0], *refs)


def _proj_attn_body(x, m, gain_ref, w_ref, qg_ref, kg_ref, cos_ref, sin_ref, shift_ref, q_ref, kt_ref, vd_ref):
    h = _rms_mod(x, gain_ref[...], m[0:1], m[1:2])
    r = _dot(h.astype(BF16), w_ref[...])
    nq = A_HEADS * A_HD
    nk = A_KV * A_HD
    cos = cos_ref[...]
    sin = sin_ref[...]
    qn = _rope(_head_norm(r[:, :nq], qg_ref[...]), cos, sin)
    kn = _rope(_head_norm(r[:, nq:nq + nk], kg_ref[...]), cos, sin)
    v = r[:, nq + nk:]
    lane = _iota((TILE, LANE), 1)
    low = lane < A_HD
    one_hot = jnp.where(lane == A_HD, 1.0, 0.0)

    def head_tile(x, idx):
        tile = x[:, LANE * (idx // 2):LANE * (idx // 2 + 1)]
        if idx % 2 == 1:
            tile = pltpu.roll(tile, A_HD, 1)
        return jnp.where(low, tile, one_hot).astype(BF16)

    for hd in range(A_HEADS):
        q_ref[0, hd] = head_tile(qn, hd)
    for g in range(A_KV):
        vd_ref[0, g] = head_tile(v, g)
    kt = jnp.transpose(kn)
    extra = jnp.where(_iota((LANE - A_HD, TILE), 0) == 0, shift_ref[...], 0.0).astype(BF16)
    for g in range(A_KV):
        kt_ref[0, g, 0:A_HD, :] = kt[A_HD * g:A_HD * (g + 1), :].astype(BF16)
        kt_ref[0, g, A_HD:LANE, :] = extra


def _rope_tables(n_lat, n_ctx):
    rows = n_lat // GRID_W
    row = jnp.repeat(jnp.arange(rows, dtype=F32), GRID_W)
    col = jnp.tile(jnp.arange(GRID_W, dtype=F32), rows)
    pairs = A_HD // 4
    inv = ROPE_THETA ** (-jnp.arange(pairs, dtype=F32) / pairs)
    ang = jnp.concatenate([row[:, None] * inv, col[:, None] * inv], axis=-1)
    c = jnp.cos(ang)
    s = jnp.sin(ang)
    cos = jnp.concatenate([c, c, c, c], axis=-1)
    sin = jnp.concatenate([-s, s, -s, s], axis=-1)
    cos = jnp.concatenate([jnp.ones((n_ctx, LANE), F32), cos], axis=0)
    sin = jnp.concatenate([jnp.zeros((n_ctx, LANE), F32), sin], axis=0)
    return cos, sin


SHIFT_LIMIT = 60.0


def _softmax_shift(qg, kg):
    bound = A_HD * jnp.max(jnp.abs(qg)) * jnp.max(jnp.abs(kg))
    fast = bound <= SHIFT_LIMIT
    shift = jnp.where(fast, jnp.ceil(bound), 0.0)
    return shift, jnp.logical_not(fast).astype(I32)


def _proj_attn_operands(nb, ntot, gain, w_in, q_norm, k_norm, cos, sin):
    nq = A_HEADS * A_HD
    nk = A_KV * A_HD
    w = w_in.astype(BF16)
    qg = (jnp.tile(q_norm, A_HEADS) * (A_HD ** -0.5 * LOG2E)).reshape(1, nq)
    kg = jnp.tile(k_norm, A_KV).reshape(1, nk)
    shift, flag = _softmax_shift(qg, kg)
    neg_shift = (-shift).reshape(1, 1).astype(F32)
    tab = pl.BlockSpec((TILE, LANE), lambda b, t, *_: (t, 0))
    arrays = [gain, w, qg, kg, cos, sin, neg_shift]
    in_specs = [_full_spec(gain), _full_spec(w), _full_spec(qg), _full_spec(kg), tab, tab, _full_spec(neg_shift)]
    out_shape = [
        jax.ShapeDtypeStruct((nb, A_HEADS, ntot, LANE), BF16),
        jax.ShapeDtypeStruct((nb, A_KV, LANE, ntot), BF16),
        jax.ShapeDtypeStruct((nb, A_KV, ntot, LANE), BF16),
    ]
    out_specs = [pl.BlockSpec((1, A_HEADS, TILE, LANE), lambda b, t, *_: (b, 0, t, 0)),
                 pl.BlockSpec((1, A_KV, LANE, TILE), lambda b, t, *_: (b, 0, 0, t)),
                 pl.BlockSpec((1, A_KV, TILE, LANE), lambda b, t, *_: (b, 0, t, 0))]
    return (arrays, in_specs, out_shape, out_specs), flag.reshape(1)


def _proj_attn(xa, mod_i, operands):
    nb, ntot, _ = xa.shape
    arrays, in_specs, out_shape, out_specs = operands
    return pl.pallas_call(
        _proj_attn_kernel,
        out_shape=tuple(out_shape),
        grid=(nb, ntot // TILE),
        in_specs=[_tok_spec(D), _mod_spec(nb)] + in_specs,
        out_specs=tuple(out_specs),
        compiler_params=_cparams(("arbitrary", "arbitrary")),
        name="proj_attn",
    )(xa, mod_i, *arrays)


def _attn_kernel(flag_ref, q_ref, kt_ref, vd_ref, o_ref, *, ntot):
    low = _iota((TILE, LANE), 1) < A_HD

    def attend(row0, nk, row_max):
        outs = []
        for hd in range(A_GRP):
            s = _dot(q_ref[0, hd, pl.ds(row0, TILE), :], kt_ref[0, 0, :, :nk])
            if row_max:
                s = s - jnp.max(s, axis=1, keepdims=True)
            r = _dot(jnp.exp2(s).astype(BF16), vd_ref[0, 0, :nk, :])
            outs.append(r / r[:, A_HD:A_HD + 1])
        t0 = jnp.where(low, outs[0], pltpu.roll(outs[1], A_HD, 1))
        t1 = jnp.where(low, outs[2], pltpu.roll(outs[3], A_HD, 1))
        o_ref[0, 0, pl.ds(row0, TILE), :] = jnp.concatenate([t0, t1], axis=1).astype(BF16)

    attend(0, TILE, True)

    def latent_tiles(row_max):
        def body(i, carry):
            attend(pl.multiple_of(i * TILE, TILE), ntot, row_max)
            return carry
        lax.fori_loop(1, ntot // TILE, body, 0)

    @pl.when(flag_ref[0] == 0)
    def _():
        latent_tiles(False)

    @pl.when(flag_ref[0] != 0)
    def _():
        latent_tiles(True)


def _attention(q, kt, vd, flag):
    nb, _, ntot, _ = q.shape
    grid_spec = pltpu.PrefetchScalarGridSpec(
        num_scalar_prefetch=1,
        grid=(nb, A_KV),
        in_specs=[pl.BlockSpec((1, A_GRP, ntot, LANE), lambda b, g, f: (b, g, 0, 0)),
                  pl.BlockSpec((1, 1, LANE, ntot), lambda b, g, f: (b, g, 0, 0)),
                  pl.BlockSpec((1, 1, ntot, LANE), lambda b, g, f: (b, g, 0, 0))],
        out_specs=pl.BlockSpec((1, 1, ntot, A_GRP * A_HD), lambda b, g, f: (b, g, 0, 0)),
    )
    return pl.pallas_call(
        functools.partial(_attn_kernel, ntot=ntot),
        out_shape=jax.ShapeDtypeStruct((nb, A_KV, ntot, A_GRP * A_HD), BF16),
        grid_spec=grid_spec,
        compiler_params=_cparams(("arbitrary", "arbitrary")),
        name="attention",
    )(flag, q, kt, vd)


def _route_tail(x, y, m, gain_ref, wrt_ref, x1_ref, h2_ref, aff_ref):
    x1 = x + m[2:3] * y
    x1_ref[0] = x1
    h2 = _rms_mod(x1, gain_ref[...], m[3:4], m[4:5])
    h2_ref[0] = h2.astype(BF16)
    logits = _dot3_nt(wrt_ref[...], h2)
    e = jnp.exp(logits - jnp.max(logits, axis=0, keepdims=True))
    aff_ref[0] = e / jnp.sum(e, axis=0, keepdims=True)


def _dot3_nt(a, b):
    ah, al = _split2(a)
    bh, bl = _split2(b)
    return _dot_nt(ah, bh) + (_dot_nt(ah, bl) + _dot_nt(al, bh))


def _readout_mlstm_kernel(x_ref, mod_ref, hf_ref, hb_ref, o_ref, mn_ref, wo_ref, gain_ref, wrt_ref,
                          x1_ref, h2_ref, aff_ref):
    hh = hf_ref[0].astype(F32) + hb_ref[0].astype(F32)
    mn = mn_ref[...]
    parts = []
    for h in range(M_HEADS):
        seg = hh[:, M_DV * h:M_DV * (h + 1)]
        ms = jnp.mean(seg * seg, axis=-1, keepdims=True)
        parts.append(seg * lax.rsqrt(ms + EPS) * mn[:, M_DV * h:M_DV * (h + 1)])
    z = jnp.concatenate(parts, axis=1) * jax.nn.sigmoid(o_ref[0].astype(F32))
    y = _dot(z.astype(BF16), wo_ref[...])
    _route_tail(x_ref[0], y, mod_ref[0], gain_ref, wrt_ref, x1_ref, h2_ref, aff_ref)


def _readout_attn_kernel(x_ref, mod_ref, oa_ref, wo_ref, gain_ref, wrt_ref, x1_ref, h2_ref, aff_ref):
    gw = A_GRP * A_HD
    y = _dot(oa_ref[0, 0], wo_ref[0:gw, :])
    for g in range(1, A_KV):
        y = y + _dot(oa_ref[0, g], wo_ref[gw * g:gw * (g + 1), :])
    _route_tail(x_ref[0], y, mod_ref[0], gain_ref, wrt_ref, x1_ref, h2_ref, aff_ref)


def _readout(xa, mod_i, mixer_out, w_out, gain_ffn, w_router, mlstm_norm=None):
    nb, ntot, _ = xa.shape
    nt = ntot // TILE
    tok = lambda w: pl.BlockSpec((1, TILE, w), lambda b, t: (b, t, 0))
    full = lambda a: pl.BlockSpec(a.shape, lambda b, t: (0,) * a.ndim)
    wo = w_out.astype(BF16)
    wrt = w_router.T
    common_in = [tok(D), pl.BlockSpec((1, 6, D), _mod_index(nb))]
    tail_in = [full(wo), full(gain_ffn), full(wrt)]
    if mlstm_norm is not None:
        hf, hb, og = mixer_out
        kern = _readout_mlstm_kernel
        mid_in = [tok(M_V), tok(M_V), tok(M_V), full(mlstm_norm)]
        args = (xa, mod_i, hf, hb, og, mlstm_norm, wo, gain_ffn, wrt)
    else:
        kern = _readout_attn_kernel
        mid_in = [pl.BlockSpec((1, A_KV, TILE, A_GRP * A_HD), lambda b, t: (b, 0, t, 0))]
        args = (xa, mod_i, mixer_out, wo, gain_ffn, wrt)
    return pl.pallas_call(
        kern,
        out_shape=(jax.ShapeDtypeStruct((nb, ntot, D), F32), jax.ShapeDtypeStruct((nb, ntot, D), BF16),
                   jax.ShapeDtypeStruct((nb, N_EXP, ntot), F32)),
        grid=(nb, nt),
        in_specs=common_in + mid_in + tail_in,
        out_specs=(tok(D), tok(D), pl.BlockSpec((1, N_EXP, TILE), lambda b, t: (b, 0, t))),
        compiler_params=_cparams(("arbitrary", "arbitrary")),
        name="readout_route",
    )(*args)


def _slot_geometry(n_ctx, n_lat):
    cap_ctx = max(1, EC_CAPACITY * n_ctx // N_EXP)
    cap_lat = max(1, EC_CAPACITY * n_lat // N_EXP)
    lat_base = -(-cap_ctx // SUB) * SUB
    slots_max = lat_base + cap_lat + (SUB - 1) * (n_lat // TILE)
    rows_ffn = -(-slots_max // 16) * 16
    return cap_ctx, cap_lat, lat_base, rows_ffn


def _topk_kernel(a_ref, slot_ref, slot_t_ref, aff_t_ref, off_ref, cnt_ref, *, n_ctx, cap_ctx, cap_lat, lat_base):
    bits = lax.bitcast_convert_type(a_ref[0], I32)
    prefix = (_iota((LANE, LANE), 0) <= _iota((LANE, LANE), 1)).astype(BF16)

    def count(mask):
        return jnp.sum(mask.astype(F32), axis=1, keepdims=True)

    def select(x, cap):
        thr = jnp.zeros((N_EXP, 1), I32)
        for bit in range(30, -1, -1):
            cand = thr | (1 << bit)
            thr = jnp.where(count(x >= cand) >= cap, cand, thr)
        gt = x > thr
        eq = x == thr
        need = cap - count(gt)
        run = jnp.zeros((N_EXP, 1), F32)
        blocks = []
        for j in range(x.shape[1] // LANE):
            sl = slice(LANE * j, LANE * (j + 1))
            eqf = eq[:, sl].astype(F32)
            inc = _dot(eqf.astype(BF16), prefix)
            rank = run + inc - eqf
            blocks.append(jnp.logical_or(gt[:, sl], jnp.logical_and(eq[:, sl], rank < need)))
            run = run + inc[:, LANE - 1:LANE]
        return blocks

    blocks = select(bits[:, :n_ctx], cap_ctx) + select(bits[:, n_ctx:], cap_lat)
    per_tile = TILE // LANE
    lane = _iota((N_EXP, LANE), 1)
    off_acc = jnp.zeros((N_EXP, LANE), I32)
    cnt_acc = jnp.zeros((N_EXP, LANE), I32)
    base = jnp.zeros((N_EXP, 1), F32)
    for t in range(len(blocks) // per_tile):
        if t == n_ctx // TILE:
            base = jnp.full((N_EXP, 1), float(lat_base), F32)
        run = jnp.zeros((N_EXP, 1), F32)
        for j in range(per_tile):
            blk = blocks[per_tile * t + j]
            sf = blk.astype(F32)
            inc = _dot(sf.astype(BF16), prefix)
            pos = base + run + inc - sf
            c0 = LANE * (per_tile * t + j)
            slot_blk = jnp.concatenate([jnp.where(blk, pos.astype(I32), -1),
                                        jnp.full((LANE - N_EXP, LANE), -1, I32)], axis=0)
            slot_ref[0, :, c0:c0 + LANE] = slot_blk
            slot_t_ref[0, c0:c0 + LANE, :] = jnp.transpose(slot_blk)
            aff_blk = jnp.concatenate([a_ref[0, :, c0:c0 + LANE], jnp.zeros((LANE - N_EXP, LANE), F32)], axis=0)
            aff_t_ref[0, c0:c0 + LANE, :] = jnp.transpose(aff_blk)
            run = run + inc[:, LANE - 1:LANE]
        n8 = jnp.floor((run + (SUB - 1)) * (1.0 / SUB)) * SUB
        off_acc = jnp.where(lane == t, base.astype(I32), off_acc)
        cnt_acc = jnp.where(lane == t, n8.astype(I32), cnt_acc)
        base = base + n8
    off_ref[0] = off_acc
    cnt_ref[0] = cnt_acc


def _topk(aff_t, n_ctx):
    nb, _, ntot = aff_t.shape
    cap_ctx, cap_lat, lat_base, _ = _slot_geometry(n_ctx, ntot - n_ctx)
    kern = functools.partial(_topk_kernel, n_ctx=n_ctx, cap_ctx=cap_ctx, cap_lat=cap_lat, lat_base=lat_base)
    return pl.pallas_call(
        kern,
        out_shape=(jax.ShapeDtypeStruct((nb, LANE, ntot), I32), jax.ShapeDtypeStruct((nb, ntot, LANE), I32),
                   jax.ShapeDtypeStruct((nb, ntot, LANE), F32), jax.ShapeDtypeStruct((nb, N_EXP, LANE), I32),
                   jax.ShapeDtypeStruct((nb, N_EXP, LANE), I32)),
        grid=(nb,),
        in_specs=[pl.BlockSpec((1, N_EXP, ntot), lambda b: (b, 0, 0))],
        out_specs=(pl.BlockSpec((1, LANE, ntot), lambda b: (b, 0, 0)),
                   pl.BlockSpec((1, ntot, LANE), lambda b: (b, 0, 0)),
                   pl.BlockSpec((1, ntot, LANE), lambda b: (b, 0, 0)),
                   pl.BlockSpec((1, N_EXP, LANE), lambda b: (b, 0, 0)),
                   pl.BlockSpec((1, N_EXP, LANE), lambda b: (b, 0, 0))),
        compiler_params=_cparams(("arbitrary",)),
        name="ec_topk",
    )(aff_t)


def _strip_index(shape, dim):
    i = _iota(shape, dim)
    e = jnp.floor((i.astype(F32) + 0.5) * (1.0 / WIN)).astype(I32)
    return e, i - WIN * e


def _dispatch_kernel(off_s, rounds_s, h_ref, slot_ref, aff_ref, xg_ref, stage_ref, sem, cnt_ref):
    b = pl.program_id(0)
    t = pl.program_id(1)
    nt = pl.num_programs(1)
    base = (b * nt + t) * N_EXP
    rows = N_EXP * WIN
    a = aff_ref[0]
    src = _iota((LANE, LANE), 0)
    dst = _iota((LANE, LANE), 1)
    a3 = None
    for p, piece in enumerate(_split3(a)):
        sel = jnp.logical_and(dst == 3 * src + p, src < N_EXP).astype(BF16)
        term = _dot(piece, sel)
        a3 = term if a3 is None else a3 + term
    rhs = jnp.concatenate([h_ref[0], a3.astype(BF16)], axis=1)
    e_of_row, _ = _strip_index((rows, LANE), 0)
    expand = (e_of_row == _iota((rows, LANE), 1)).astype(BF16)
    _, j_row = _strip_index((rows, 1), 0)
    sl = slot_ref[0]
    e_row = _iota((LANE, 1), 0)
    off_v = jnp.zeros((LANE, 1), I32)
    for e in range(N_EXP):
        off_v = jnp.where(e_row == e, off_s[base + e], off_v)

    @pl.when(jnp.logical_and(b == 0, t == 0))
    def _():
        cnt_ref[0] = 0

    def strip_copies(buf, sample, starts):
        return [pltpu.make_async_copy(stage_ref.at[buf, pl.ds(WIN * e, WIN), :],
                                      xg_ref.at[sample, e, pl.ds(starts[e], WIN), :], sem.at[buf])
                for e in range(N_EXP)]

    def wait_strips(buf):
        for cp in strip_copies(buf, 0, [0] * N_EXP):
            cp.wait()

    def round_body(r, carry):
        n = cnt_ref[0]
        buf = n & 1
        rel = jnp.clip(sl - (off_v + WIN * r), -1, WIN).astype(F32).astype(BF16)
        relx = _dot(expand, rel)
        onehot = (relx == j_row.astype(F32)).astype(BF16)
        bits = lax.bitcast_convert_type(_dot(onehot, rhs), I32)
        packed = jnp.bitwise_or(jnp.bitwise_and(bits[:, :HALF] >> 16, 0xFFFF),
                                jnp.bitwise_and(bits[:, HALF:D], HIGH16))
        stage_ref[buf] = jnp.concatenate([packed, bits[:, D:]], axis=1)

        @pl.when(n > 0)
        def _():
            wait_strips(1 - buf)

        last_start = xg_ref.shape[2] - WIN
        starts = [pl.multiple_of(jnp.minimum(off_s[base + e] + WIN * r, last_start), SUB) for e in range(N_EXP)]
        for cp in strip_copies(buf, b, starts):
            cp.start()
        cnt_ref[0] = n + 1
        return carry

    lax.fori_loop(0, rounds_s[b * nt + t], round_body, 0)

    @pl.when(jnp.logical_and(b == pl.num_programs(0) - 1, t == nt - 1))
    def _():
        wait_strips((cnt_ref[0] - 1) & 1)


def _dispatch(h2, slot_pad, aff_pad, off_flat, rounds_flat, rows_alloc):
    nb, ntot, _ = h2.shape
    nt = ntot // TILE
    grid_spec = pltpu.PrefetchScalarGridSpec(
        num_scalar_prefetch=2,
        grid=(nb, nt),
        in_specs=[pl.BlockSpec((1, TILE, D), lambda b, t, o, r: (b, t, 0)),
                  pl.BlockSpec((1, LANE, TILE), lambda b, t, o, r: (b, 0, t)),
                  pl.BlockSpec((1, TILE, LANE), lambda b, t, o, r: (b, t, 0))],
        out_specs=pl.BlockSpec(memory_space=pl.ANY),
        scratch_shapes=[pltpu.VMEM((2, N_EXP * WIN, XP), I32), pltpu.SemaphoreType.DMA((2,)),
                        pltpu.SMEM((1,), I32)],
    )
    return pl.pallas_call(
        _dispatch_kernel,
        out_shape=jax.ShapeDtypeStruct((nb, N_EXP, rows_alloc, XP), I32),
        grid_spec=grid_spec,
        compiler_params=_cparams(("arbitrary", "arbitrary")),
        name="ec_dispatch",
    )(off_flat, rounds_flat, h2, slot_pad, aff_pad)


def _ffn_kernel(used_s, xg_ref, wg32_ref, wu32_ref, wd32_ref, y_ref, wg_ref, wu_ref, wd_ref, *, rows_short):
    e = pl.program_id(0)
    b = pl.program_id(1)

    @pl.when(b == 0)
    def _():
        wg_ref[0] = wg32_ref[0].astype(BF16)
        wu_ref[0] = wu32_ref[0].astype(BF16)
        wd_ref[0] = wd32_ref[0].astype(BF16)

    rows_all = y_ref.shape[2]

    def run(rows):
        words = xg_ref[0, 0, :rows, :HALF]
        x = jnp.concatenate([lax.bitcast_convert_type(words << 16, F32),
                             lax.bitcast_convert_type(jnp.bitwise_and(words, HIGH16), F32)], axis=1).astype(BF16)
        gl = lax.bitcast_convert_type(xg_ref[0, 0, :rows, HALF:], F32)
        lane = _iota(gl.shape, 1)
        mine = jnp.logical_and(lane >= 3 * e, lane < 3 * e + 3)
        gate = jnp.sum(jnp.where(mine, gl, 0.0), axis=1, keepdims=True)
        a = _dot(x, wg_ref[0])
        u = _dot(x, wu_ref[0])
        hm = (a * jax.nn.sigmoid(a) * u).astype(BF16)
        y_ref[0, 0, :rows] = _dot(hm, wd_ref[0]) * gate
        if rows < rows_all:
            y_ref[0, 0, rows:] = jnp.zeros((rows_all - rows, D), F32)

    used = used_s[b * N_EXP + e]

    @pl.when(used <= rows_short)
    def _():
        run(rows_short)

    @pl.when(used > rows_short)
    def _():
        run(rows_all)


def _expert_ffn(xg, wg, wu, wd, used_flat, rows_ffn, layer=0):
    nb = xg.shape[0]
    rows_short = max(16, rows_ffn - 48)
    if wg.ndim == 4:
        wspec = pl.BlockSpec((None, 1, D, D), lambda e, b, u: (layer, e, 0, 0))
    else:
        wspec = pl.BlockSpec((1, D, D), lambda e, b, u: (e, 0, 0))
    grid_spec = pltpu.PrefetchScalarGridSpec(
        num_scalar_prefetch=1,
        grid=(N_EXP, nb),
        in_specs=[pl.BlockSpec((1, 1, rows_ffn, XP), lambda e, b, u: (b, e, 0, 0)), wspec, wspec, wspec],
        out_specs=pl.BlockSpec((1, 1, rows_ffn, D), lambda e, b, u: (b, e, 0, 0)),
        scratch_shapes=[pltpu.VMEM((1, D, D), BF16)] * 3,
    )
    return pl.pallas_call(
        functools.partial(_ffn_kernel, rows_short=rows_short),
        out_shape=jax.ShapeDtypeStruct((nb, N_EXP, rows_ffn, D), F32),
        grid_spec=grid_spec,
        compiler_params=_cparams(("arbitrary", "arbitrary")),
        name="ec_ffn",
    )(used_flat, xg, wg, wu, wd)


def _combine_kernel(off_s, rounds_s, x_ref, mod_ref, slot_ref, fin_ref, y_ref, *rest, final_ctx_tiles, proj_body,
                    n_proj_in):
    if proj_body is None:
        o_ref, strip_ref, sem = rest
    else:
        next_mod_ref = rest[0]
        proj_in = rest[1:1 + n_proj_in]
        o_ref = rest[1 + n_proj_in]
        proj_out = rest[2 + n_proj_in:-2]
        strip_ref, sem = rest[-2:]
    b = pl.program_id(0)
    t = pl.program_id(1)
    nt = pl.num_programs(1)
    base = (b * nt + t) * N_EXP
    cols = N_EXP * WIN
    e_of_col, _ = _strip_index((LANE, cols), 1)
    expand = (e_of_col == _iota((LANE, cols), 0)).astype(BF16)
    _, j_lane = _strip_index((1, cols), 1)
    j_lane = j_lane.astype(F32)
    sl = slot_ref[0]
    e_lane = _iota((1, LANE), 1)
    last_start = y_ref.shape[2] - WIN

    def strip_start(step_, e, r):
        return jnp.minimum(off_s[step_ * N_EXP + e] + WIN * r, last_start)

    step = b * nt + t
    buf = step & 1

    def strip_copies(step_, sample, r, dst):
        return [pltpu.make_async_copy(
            y_ref.at[sample, e, pl.ds(pl.multiple_of(strip_start(step_, e, r), SUB), WIN), :],
            strip_ref.at[dst, pl.ds(WIN * e, WIN), :], sem.at[dst]) for e in range(N_EXP)]

    @pl.when(step == 0)
    def _():
        for cp in strip_copies(step, b, 0, buf):
            cp.start()

    @pl.when(step + 1 < pl.num_programs(0) * nt)
    def _():
        for cp in strip_copies(step + 1, jnp.where(t == nt - 1, b + 1, b), 0, 1 - buf):
            cp.start()

    def expand_round(r, acc):
        first_v = jnp.zeros((1, LANE), I32)
        start_v = jnp.zeros((1, LANE), I32)
        for e in range(N_EXP):
            first_v = jnp.where(e_lane == e, off_s[base + e] + WIN * r, first_v)
            start_v = jnp.where(e_lane == e, strip_start(step, e, r), start_v)
        nominal = sl - first_v
        in_round = jnp.logical_and(nominal >= 0, nominal < WIN)
        rel = jnp.where(in_round, sl - start_v, -1).astype(F32).astype(BF16)
        relx = _dot(rel, expand)
        onehot = (relx == j_lane).astype(BF16)
        for cp in strip_copies(step, b, r, buf):
            cp.wait()
        hi, lo = _split2(strip_ref[buf])
        return acc + (_dot(onehot, hi) + _dot(onehot, lo))

    def extra_round(r, acc):
        for cp in strip_copies(step, b, r, buf):
            cp.start()
        return expand_round(r, acc)

    acc = expand_round(0, jnp.zeros((TILE, D), F32))
    acc = lax.fori_loop(1, rounds_s[step], extra_round, acc)
    x2 = x_ref[0] + mod_ref[0][5:6] * acc
    if final_ctx_tiles is None:
        o_ref[0] = x2
        if proj_body is not None:
            proj_body(x2, next_mod_ref[0], *proj_in, *proj_out)
    else:
        @pl.when(t >= final_ctx_tiles)
        def _():
            ms = jnp.mean(x2 * x2, axis=-1, keepdims=True)
            o_ref[0] = x2 * lax.rsqrt(ms + EPS) * fin_ref[...]


def _combine(x1, mod_i, slot_t_pad, y, off_flat, rounds_flat, final_gain, final_ctx_tiles=None, next_proj=None):
    nb, ntot, _ = x1.shape
    nt = ntot // TILE
    skip = 0 if final_ctx_tiles is None else final_ctx_tiles
    in_specs = [_tok_spec(D), _mod_spec(nb), _tok_spec(LANE), pl.BlockSpec((1, D), lambda b, t, *_: (0, 0)),
                pl.BlockSpec(memory_space=pl.ANY)]
    args = [x1, mod_i, slot_t_pad, final_gain, y]
    out_shape = [jax.ShapeDtypeStruct((nb, ntot - skip * TILE, D), F32)]
    out_specs = [pl.BlockSpec((1, TILE, D), lambda b, t, *_: (b, jnp.maximum(t - skip, 0), 0))]
    proj_body, n_proj_in = None, 0
    if next_proj is not None:
        proj_body, next_mod, (arrays, specs, p_shape, p_specs) = next_proj
        in_specs += [_mod_spec(nb)] + specs
        args += [next_mod] + arrays
        out_shape += p_shape
        out_specs += p_specs
        n_proj_in = len(arrays)
    grid_spec = pltpu.PrefetchScalarGridSpec(
        num_scalar_prefetch=2,
        grid=(nb, nt),
        in_specs=in_specs,
        out_specs=tuple(out_specs),
        scratch_shapes=[pltpu.VMEM((2, N_EXP * WIN, D), F32), pltpu.SemaphoreType.DMA((2,))],
    )
    outs = pl.pallas_call(
        functools.partial(_combine_kernel, final_ctx_tiles=final_ctx_tiles, proj_body=proj_body,
                          n_proj_in=n_proj_in),
        out_shape=tuple(out_shape),
        grid_spec=grid_spec,
        compiler_params=_cparams(("arbitrary", "arbitrary")),
        name="ec_combine",
    )(off_flat, rounds_flat, *args)
    return outs[0], tuple(outs[1:])


def _moe(x1, h2, aff_t, mod_i, wg, wu, wd, n_ctx, final_gain, last, layer=0, next_proj=None):
    nb, ntot, _ = x1.shape
    nt = ntot // TILE
    _, _, _, rows_ffn = _slot_geometry(n_ctx, ntot - n_ctx)
    rows_alloc = rows_ffn + WIN
    slot_pad, slot_t_pad, aff_pad, off, cnt = _topk(aff_t, n_ctx)
    off_t = jnp.swapaxes(off[:, :, :nt], 1, 2)
    cnt_t = jnp.swapaxes(cnt[:, :, :nt], 1, 2)
    rounds_c = jnp.maximum(1, jnp.max((cnt_t + WIN - 1) // WIN, axis=2))
    fill = jnp.max((rows_alloc - off_t[:, nt - 1, :] + WIN - 1) // WIN, axis=1)
    rounds_d = rounds_c.at[:, nt - 1].max(fill)
    off_flat = off_t.reshape(-1)
    xg = _dispatch(h2, slot_pad, aff_pad, off_flat, rounds_d.reshape(-1), rows_alloc)
    used = (off_t[:, nt - 1, :] + cnt_t[:, nt - 1, :]).reshape(-1)
    y = _expert_ffn(xg, wg, wu, wd, used, rows_ffn, layer)
    return _combine(x1, mod_i, slot_t_pad, y, off_flat, rounds_c.reshape(-1), final_gain,
                    final_ctx_tiles=n_ctx // TILE if last else None, next_proj=next_proj)


def kernel(x, c, ctx, c_ctx, w_mod, b_mod, norm_mix, norm_ffn, mlstm_w_in, mlstm_b_gate, mlstm_norm, mlstm_w_out,
           attn_w_in, attn_q_norm, attn_k_norm, attn_w_out, moe_router, moe_w_gate, moe_w_up, moe_w_down,
           norm_final):
    nb, n_lat, _ = x.shape
    n_ctx = ctx.shape[1]
    depth = w_mod.shape[0]
    assert n_ctx == TILE and n_lat % TILE == 0 and x.shape[2] == D
    xa = jnp.concatenate([ctx, x], axis=1)
    rb = -(-(nb + 1) // SUB) * SUB
    cc = jnp.concatenate([c, c_ctx[None, :], jnp.zeros((rb - nb - 1, D), F32)], axis=0)
    mod = _modulation(cc, w_mod, b_mod)
    cos, sin = _rope_tables(n_lat, n_ctx)
    ntot = n_ctx + n_lat

    def proj_operands(i):
        j = i // 2
        gain_mix = norm_mix[i].reshape(1, D)
        if i % 2 == 0:
            return _proj_mlstm_body, _proj_mlstm_operands(nb, ntot, gain_mix, mlstm_w_in[j], mlstm_b_gate[j]), None
        ops, flag = _proj_attn_operands(nb, ntot, gain_mix, attn_w_in[j], attn_q_norm[j], attn_k_norm[j], cos, sin)
        return _proj_attn_body, ops, flag

    _, ops, flag = proj_operands(0)
    proj = _proj_mlstm(xa, mod[0], ops)
    for i in range(depth):
        j = i // 2
        mod_i = mod[i]
        gain_ffn = norm_ffn[i].reshape(1, D)
        if i % 2 == 0:
            q, kt, v, og, gc, gr = proj
            hf, hb = _mlstm_scan(q, kt, v, gc, gr)
            x1, h2, aff_t = _readout(xa, mod_i, (hf, hb, og), mlstm_w_out[j], gain_ffn, moe_router[i],
                                     mlstm_norm=mlstm_norm[j].reshape(1, M_V))
        else:
            q, kt, vd = proj
            oa = _attention(q, kt, vd, flag)
            x1, h2, aff_t = _readout(xa, mod_i, oa, attn_w_out[j], gain_ffn, moe_router[i])
        last = i == depth - 1
        next_proj = None
        if not last:
            body, ops, flag = proj_operands(i + 1)
            next_proj = (body, mod[i + 1], ops)
        xa, proj = _moe(x1, h2, aff_t, mod_i, moe_w_gate, moe_w_up, moe_w_down, n_ctx,
                        norm_final.reshape(1, D), last=last, layer=i, next_proj=next_proj)
    return xa
```

```python
import functools

import jax
import jax.numpy as jnp
from jax import lax
from jax.experimental import pallas as pl
from jax.experimental.pallas import tpu as pltpu

F32 = jnp.float32
BF16 = jnp.bfloat16
I32 = jnp.int32

D = 1024
TILE = 256
HALVES = (slice(0, TILE // 2), slice(TILE // 2, TILE))
EPS = 1e-6
DEPTH = 4

M_HEADS = 4
M_DK = 128
M_DV = 256
M_QK = M_HEADS * M_DK
M_V = M_HEADS * M_DV
M_AUG = M_DV + 128
GATE_CAP = 15.0

A_HEADS = 16
A_KV = 4
A_GRP = 4
A_HD = 64
ROPE_THETA = 10000.0
GRID_W = 64
LOG2E = 1.4426950408889634

N_EXP = 16
EC_CAPACITY = 2
WIN = 48
GATE_LANES = 128
XW = D + GATE_LANES
HALF = D // 2
XP = HALF + GATE_LANES
HIGH16 = -65536

LANE = 128
SUB = 8
VMEM_LIMIT = 56 * 1024 * 1024


def _cparams(sem):
    return pltpu.CompilerParams(dimension_semantics=sem, vmem_limit_bytes=VMEM_LIMIT)


def _dot(a, b):
    return jnp.dot(a, b, preferred_element_type=F32)


def _dot_nt(a, b):
    return lax.dot_general(a, b, (((1,), (1,)), ((), ())), preferred_element_type=F32)


def _split2(x):
    hi = x.astype(BF16)
    lo = (x - hi.astype(F32)).astype(BF16)
    return hi, lo


def _split3(x):
    hi = x.astype(BF16)
    r = x - hi.astype(F32)
    mid = r.astype(BF16)
    lo = (r - mid.astype(F32)).astype(BF16)
    return hi, mid, lo


def _dot3(a, b):
    ah, al = _split2(a)
    bh, bl = _split2(b)
    return _dot(ah, bh) + (_dot(ah, bl) + _dot(al, bh))


def _rms_mod(x, gain, shift, scale):
    ms = jnp.mean(x * x, axis=-1, keepdims=True)
    y = x * lax.rsqrt(ms + EPS) * gain
    return y * (1.0 + scale) + shift


def _iota(shape, dim):
    return lax.broadcasted_iota(I32, shape, dim)


def _mod_kernel(c_ref, w_ref, b_ref, o_ref):
    c = c_ref[...]
    s = c * jax.nn.sigmoid(c)
    o_ref[...] = _dot3(s, w_ref[...]) + b_ref[...]


def _modulation(cc, w_mod, b_mod):
    depth, _, n6 = w_mod.shape
    rb = cc.shape[0]
    nj = n6 // D
    out = pl.pallas_call(
        _mod_kernel,
        out_shape=jax.ShapeDtypeStruct((depth, rb, n6), F32),
        grid=(depth, nj),
        in_specs=[
            pl.BlockSpec((rb, D), lambda i, j: (0, 0)),
            pl.BlockSpec((None, D, D), lambda i, j: (i, 0, j)),
            pl.BlockSpec((None, 1, D), lambda i, j: (i, 0, j)),
        ],
        out_specs=pl.BlockSpec((None, rb, D), lambda i, j: (i, 0, j)),
        compiler_params=_cparams(("arbitrary", "arbitrary")),
        name="adaln_mod",
    )(cc, w_mod, b_mod.reshape(depth, 1, n6))
    return out.reshape(depth, rb, nj, D)


def _gate_act(g, idx):
    g = GATE_CAP * jnp.tanh(g * (1.0 / GATE_CAP))
    logsig = jnp.minimum(g, 0.0) - jnp.log(1.0 + jnp.exp(-jnp.abs(g)))
    is_forget = ((idx >> 2) & 1) == 1
    return jnp.where(is_forget, logsig, g)


def _proj_mlstm_body(x, m, gain_ref, w_ref, wkt_ref, wg_ref, wgt_ref, bc_ref, br_ref,
                     q_ref, kt_ref, v_ref, o_ref, gc_ref, gr_ref):
    h = _rms_mod(x, gain_ref[...], m[0:1], m[1:2])
    hb = h.astype(BF16)
    r = _dot(hb, w_ref[...])
    q_ref[0] = (r[:, :M_QK] * (M_DK ** -0.5)).astype(BF16)
    v_ref[0] = r[:, M_QK:M_QK + M_V].astype(BF16)
    o_ref[0] = r[:, M_QK + M_V:].astype(BF16)
    kt_ref[0] = _dot_nt(wkt_ref[...], hb).astype(BF16)
    gc = _dot(hb, wg_ref[...]) + bc_ref[...]
    gc_ref[0] = _gate_act(gc, _iota(gc.shape, 1))
    gr = _dot_nt(wgt_ref[...], hb) + br_ref[...]
    gr_ref[0] = _gate_act(gr, _iota(gr.shape, 0))


def _proj_mlstm_kernel(*refs, n_src):
    _proj_mlstm_body(_tile_value(refs[:n_src]), refs[n_src][0], *refs[n_src + 1:])


def _tile_sources(xa):
    if isinstance(xa, tuple):
        ctx, lat = xa
        assert ctx.shape[1] == TILE
        specs = [pl.BlockSpec((1, TILE, D), lambda b, t, *_: (b, 0, 0)),
                 pl.BlockSpec((1, TILE, D), lambda b, t, *_: (b, jnp.maximum(t - 1, 0), 0))]
        return [ctx, lat], specs, lat.shape[0], ctx.shape[1] + lat.shape[1]
    return [xa], [_tok_spec(D)], xa.shape[0], xa.shape[1]


def _tile_value(src_refs):
    if len(src_refs) == 1:
        return src_refs[0][0]
    return jnp.where(pl.program_id(1) == 0, src_refs[0][0], src_refs[1][0])


def _tok_spec(width):
    return pl.BlockSpec((1, TILE, width), lambda b, t, *_: (b, t, 0))


def _full_spec(a):
    return pl.BlockSpec(a.shape, lambda b, t, *_: (0,) * a.ndim)


def _mod_spec(nb):
    return pl.BlockSpec((1, 6, D), lambda b, t, *_: (jnp.where(t == 0, nb, b), 0, 0))


def _proj_mlstm_operands(nb, ntot, gain, w_in, b_gate):
    n_g = 4 * M_HEADS
    wq = w_in[:, :M_QK]
    wk = w_in[:, M_QK:2 * M_QK]
    wvo = w_in[:, 2 * M_QK:2 * M_QK + 2 * M_V]
    wg = w_in[:, 2 * M_QK + 2 * M_V:]
    w_main = jnp.concatenate([wq, wvo], axis=1).astype(BF16)
    wkt = wk.T.astype(BF16)
    wg_pad = jnp.pad(wg, ((0, 0), (0, LANE - n_g))).astype(BF16)
    wgt = wg.T.astype(BF16)
    bc = jnp.pad(b_gate, (0, LANE - n_g)).reshape(1, LANE)
    br = b_gate.reshape(n_g, 1)
    arrays = [gain, w_main, wkt, wg_pad, wgt, bc, br]
    out_shape = [
        jax.ShapeDtypeStruct((nb, ntot, M_QK), BF16),
        jax.ShapeDtypeStruct((nb, M_QK, ntot), BF16),
        jax.ShapeDtypeStruct((nb, ntot, M_V), BF16),
        jax.ShapeDtypeStruct((nb, ntot, M_V), BF16),
        jax.ShapeDtypeStruct((nb, ntot, LANE), F32),
        jax.ShapeDtypeStruct((nb, n_g, ntot), F32),
    ]
    out_specs = [_tok_spec(M_QK), pl.BlockSpec((1, M_QK, TILE), lambda b, t, *_: (b, 0, t)), _tok_spec(M_V),
                 _tok_spec(M_V), _tok_spec(LANE), pl.BlockSpec((1, n_g, TILE), lambda b, t, *_: (b, 0, t))]
    return arrays, [_full_spec(a) for a in arrays], out_shape, out_specs


def _proj_mlstm(xa, mod_i, operands):
    srcs, src_specs, nb, ntot = _tile_sources(xa)
    arrays, in_specs, out_shape, out_specs = operands
    return pl.pallas_call(
        functools.partial(_proj_mlstm_kernel, n_src=len(srcs)),
        out_shape=tuple(out_shape),
        grid=(nb, ntot // TILE),
        in_specs=src_specs + [_mod_spec(nb)] + in_specs,
        out_specs=tuple(out_specs),
        compiler_params=_cparams(("arbitrary", "arbitrary")),
        name="proj_mlstm",
    )(*srcs, mod_i, *arrays)


def _mlstm_kernel(qf_ref, qb_ref, kf_ref, kb_ref, vf_ref, vb_ref, gcf_ref, gcb_ref, grf_ref, grb_ref,
                  hf_ref, hb_ref, c_ref, m_ref):
    t = pl.program_id(1)

    @pl.when(t == 0)
    def _():
        c_ref[...] = jnp.zeros(c_ref.shape, F32)
        m_ref[...] = jnp.zeros(m_ref.shape, F32)

    n = TILE
    row = _iota((n, n), 0)
    col = _iota((n, n), 1)
    lower = col <= row
    upper = col >= row
    lower_b = lower.astype(BF16)
    upper_b = upper.astype(BF16)
    ones_col = (_iota((n, M_AUG - M_DV), 1) == 0).astype(BF16)
    dirs = ((qf_ref, kf_ref, vf_ref, gcf_ref, grf_ref, hf_ref, lower, lower_b, upper_b, n - 1),
            (qb_ref, kb_ref, vb_ref, gcb_ref, grb_ref, hb_ref, upper, upper_b, lower_b, 0))
    c_old = [c_ref[i] for i in range(2 * M_HEADS)]
    m_old = [m_ref[i][0:1, 0:1] for i in range(2 * M_HEADS)]
    c_new, m_new_all, h_out = {}, {}, {0: [], 1: []}
    for d, (q_ref, k_ref, v_ref, gc_ref, gr_ref, o_ref, mask, cum_l, cum_r, last) in enumerate(dirs):
        gc = gc_ref[0]
        gr = gr_ref[0]
        bcol = sum(_dot(cum_l, p) for p in _split3(gc))
        brow = sum(_dot(p, cum_r) for p in _split3(gr))
        for h in range(M_HEADS):
            gi = 8 * d + h
            gf = gi + 4
            sidx = 4 * d + h
            b_col = bcol[:, gf:gf + 1]
            b_row = brow[gf:gf + 1, :]
            ig_row = gr[gi:gi + 1, :]
            total = b_row[:, last:last + 1]
            m_st = m_old[sidx]
            key_row = ig_row - b_row
            log_d = jnp.where(mask, b_col + key_row, -jnp.inf)
            m_inter = b_col + m_st
            m_q = jnp.maximum(m_inter, jnp.max(log_d, axis=1, keepdims=True))
            w_intra = jnp.exp(log_d - m_q)
            w_inter = jnp.exp(m_inter - m_q)
            qh = q_ref[0, :, M_DK * h:M_DK * (h + 1)]
            kth = k_ref[0, M_DK * h:M_DK * (h + 1), :]
            v_aug = jnp.concatenate([v_ref[0, :, M_DV * h:M_DV * (h + 1)], ones_col], axis=1)
            c_aug = c_old[sidx]
            s = (_dot(qh, kth) * w_intra).astype(BF16)
            nd = _dot(s, v_aug) + w_inter * _dot(qh, c_aug.astype(BF16))
            den = nd[:, M_DV:M_DV + 1]
            inv = 1.0 / jnp.maximum(jnp.abs(den), jnp.exp(-m_q))
            h_out[d].append((nd[:, :M_DV] * inv).astype(o_ref.dtype))
            log_w = total + key_row
            m_new = jnp.maximum(total + m_st, jnp.max(log_w, axis=1, keepdims=True))
            w_key = jnp.exp(log_w - m_new)
            decay = jnp.exp(total + m_st - m_new)
            kw = (kth.astype(F32) * w_key).astype(BF16)
            c_new[sidx] = decay * c_aug + _dot(kw, v_aug)
            m_new_all[sidx] = m_new
    hf_ref[0] = jnp.concatenate(h_out[0], axis=1)
    hb_ref[0] = jnp.concatenate(h_out[1], axis=1)
    for i in range(2 * M_HEADS):
        c_ref[i] = c_new[i]
        m_ref[i] = jnp.broadcast_to(m_new_all[i], m_ref.shape[1:])


def _mlstm_scan(q, kt, v, gc, gr):
    nb, ntot, _ = q.shape
    nt = ntot // TILE
    fwd = lambda b, t: (b, t, 0)
    bwd = lambda b, t: (b, jnp.where(t == 0, 0, nt - t), 0)
    fwd_t = lambda b, t: (b, 0, t)
    bwd_t = lambda b, t: (b, 0, jnp.where(t == 0, 0, nt - t))
    n_g = gr.shape[1]
    return pl.pallas_call(
        _mlstm_kernel,
        out_shape=(jax.ShapeDtypeStruct((nb, ntot, M_V), BF16), jax.ShapeDtypeStruct((nb, ntot, M_V), BF16)),
        grid=(nb, nt),
        in_specs=[
            pl.BlockSpec((1, TILE, M_QK), fwd), pl.BlockSpec((1, TILE, M_QK), bwd),
            pl.BlockSpec((1, M_QK, TILE), fwd_t), pl.BlockSpec((1, M_QK, TILE), bwd_t),
            pl.BlockSpec((1, TILE, M_V), fwd), pl.BlockSpec((1, TILE, M_V), bwd),
            pl.BlockSpec((1, TILE, LANE), fwd), pl.BlockSpec((1, TILE, LANE), bwd),
            pl.BlockSpec((1, n_g, TILE), fwd_t), pl.BlockSpec((1, n_g, TILE), bwd_t),
        ],
        out_specs=(pl.BlockSpec((1, TILE, M_V), fwd), pl.BlockSpec((1, TILE, M_V), bwd)),
        scratch_shapes=[pltpu.VMEM((2 * M_HEADS, M_DK, M_AUG), F32), pltpu.VMEM((2 * M_HEADS, SUB, LANE), F32)],
        compiler_params=_cparams(("arbitrary", "arbitrary")),
        name="mlstm_scan",
    )(q, q, kt, kt, v, v, gc, gc, gr, gr)


def _head_norm(x, gain):
    w = x.shape[1]
    gsum = ((_iota((w, LANE), 0) >> 6) == _iota((w, LANE), 1)).astype(BF16)
    gexp = ((_iota((LANE, w), 1) >> 6) == _iota((LANE, w), 0)).astype(BF16)
    hi, lo = _split2(x * x)
    ssum = _dot(hi, gsum) + _dot(lo, gsum)
    rh, rl = _split2(lax.rsqrt(ssum * (1.0 / A_HD) + EPS))
    return x * (_dot(rh, gexp) + _dot(rl, gexp)) * gain


def _rope(x, cos, sin_signed):
    first = (_iota((x.shape[0], LANE), 1) & (A_HD - 1)) < A_HD // 2
    tiles = []
    for i in range(x.shape[1] // LANE):
        xt = x[:, LANE * i:LANE * (i + 1)]
        partner = jnp.where(first, pltpu.roll(xt, LANE - A_HD // 2, 1), pltpu.roll(xt, A_HD // 2, 1))
        tiles.append(xt * cos + partner * sin_signed)
    return jnp.concatenate(tiles, axis=1)


def _proj_attn_body(x, m, gain_ref, w_ref, qg_ref, kg_ref, cos_ref, sin_ref, shift_ref, q_ref, kt_ref, vd_ref):
    h = _rms_mod(x, gain_ref[...], m[0:1], m[1:2])
    r = _dot(h.astype(BF16), w_ref[...])
    nq = A_HEADS * A_HD
    nk = A_KV * A_HD
    cos = cos_ref[...]
    sin = sin_ref[...]
    qn = _rope(_head_norm(r[:, :nq], qg_ref[...]), cos, sin)
    kn = _rope(_head_norm(r[:, nq:nq + nk], kg_ref[...]), cos, sin)
    v = r[:, nq + nk:]
    lane = _iota((TILE, LANE), 1)
    low = lane < A_HD
    one_hot = jnp.where(lane == A_HD, 1.0, 0.0)

    def head_tile(a, idx):
        tile = a[:, LANE * (idx // 2):LANE * (idx // 2 + 1)]
        if idx % 2 == 1:
            tile = pltpu.roll(tile, A_HD, 1)
        return jnp.where(low, tile, one_hot).astype(BF16)

    for hd in range(A_HEADS):
        q_ref[0, hd] = head_tile(qn, hd)
    for g in range(A_KV):
        vd_ref[0, g] = head_tile(v, g)
    kt = jnp.transpose(kn)
    extra = jnp.where(_iota((LANE - A_HD, TILE), 0) == 0, shift_ref[...], 0.0).astype(BF16)
    for g in range(A_KV):
        kt_ref[0, g, 0:A_HD, :] = kt[A_HD * g:A_HD * (g + 1), :].astype(BF16)
        kt_ref[0, g, A_HD:LANE, :] = extra


def _rope_tables(n_lat, n_ctx):
    rows = n_lat // GRID_W
    row = jnp.repeat(jnp.arange(rows, dtype=F32), GRID_W)
    col = jnp.tile(jnp.arange(GRID_W, dtype=F32), rows)
    pairs = A_HD // 4
    inv = ROPE_THETA ** (-jnp.arange(pairs, dtype=F32) / pairs)
    ang = jnp.concatenate([row[:, None] * inv, col[:, None] * inv], axis=-1)
    c = jnp.cos(ang)
    s = jnp.sin(ang)
    cos = jnp.concatenate([c, c, c, c], axis=-1)
    sin = jnp.concatenate([-s, s, -s, s], axis=-1)
    cos = jnp.concatenate([jnp.ones((n_ctx, LANE), F32), cos], axis=0)
    sin = jnp.concatenate([jnp.zeros((n_ctx, LANE), F32), sin], axis=0)
    return cos, sin


SHIFT_LIMIT = 60.0


def _softmax_shift(qg, kg):
    bound = A_HD * jnp.max(jnp.abs(qg)) * jnp.max(jnp.abs(kg))
    fast = bound <= SHIFT_LIMIT
    shift = jnp.where(fast, jnp.ceil(bound), 0.0)
    return shift, jnp.logical_not(fast).astype(I32)


def _proj_attn_operands(nb, ntot, gain, w_in, q_norm, k_norm, cos, sin):
    nq = A_HEADS * A_HD
    nk = A_KV * A_HD
    w = w_in.astype(BF16)
    qg = (jnp.tile(q_norm, A_HEADS) * (A_HD ** -0.5 * LOG2E)).reshape(1, nq)
    kg = jnp.tile(k_norm, A_KV).reshape(1, nk)
    shift, flag = _softmax_shift(qg, kg)
    neg_shift = (-shift).reshape(1, 1).astype(F32)
    tab = pl.BlockSpec((TILE, LANE), lambda b, t, *_: (t, 0))
    arrays = [gain, w, qg, kg, cos, sin, neg_shift]
    in_specs = [_full_spec(gain), _full_spec(w), _full_spec(qg), _full_spec(kg), tab, tab, _full_spec(neg_shift)]
    out_shape = [
        jax.ShapeDtypeStruct((nb, A_HEADS, ntot, LANE), BF16),
        jax.ShapeDtypeStruct((nb, A_KV, LANE, ntot), BF16),
        jax.ShapeDtypeStruct((nb, A_KV, ntot, LANE), BF16),
    ]
    out_specs = [pl.BlockSpec((1, A_HEADS, TILE, LANE), lambda b, t, *_: (b, 0, t, 0)),
                 pl.BlockSpec((1, A_KV, LANE, TILE), lambda b, t, *_: (b, 0, 0, t)),
                 pl.BlockSpec((1, A_KV, TILE, LANE), lambda b, t, *_: (b, 0, t, 0))]
    return (arrays, in_specs, out_shape, out_specs), flag.reshape(1)


def _attn_kernel(flag_ref, q_ref, kt_ref, vd_ref, o_ref, *, ntot):
    low = _iota((TILE, LANE), 1) < A_HD

    def attend(row0, nk, row_max):
        outs = []
        for hd in range(A_GRP):
            s = _dot(q_ref[0, hd, pl.ds(row0, TILE), :], kt_ref[0, 0, :, :nk])
            if row_max:
                s = s - jnp.max(s, axis=1, keepdims=True)
            r = _dot(jnp.exp2(s).astype(BF16), vd_ref[0, 0, :nk, :])
            outs.append(r / r[:, A_HD:A_HD + 1])
        t0 = jnp.where(low, outs[0], pltpu.roll(outs[1], A_HD, 1))
        t1 = jnp.where(low, outs[2], pltpu.roll(outs[3], A_HD, 1))
        o_ref[0, 0, pl.ds(row0, TILE), :] = jnp.concatenate([t0, t1], axis=1).astype(BF16)

    attend(0, TILE, True)

    def latent_tiles(row_max):
        def body(i, carry):
            attend(pl.multiple_of(i * TILE, TILE), ntot, row_max)
            return carry
        lax.fori_loop(1, ntot // TILE, body, 0)

    @pl.when(flag_ref[0] == 0)
    def _():
        latent_tiles(False)

    @pl.when(flag_ref[0] != 0)
    def _():
        latent_tiles(True)


def _attention(q, kt, vd, flag):
    nb, _, ntot, _ = q.shape
    grid_spec = pltpu.PrefetchScalarGridSpec(
        num_scalar_prefetch=1,
        grid=(nb, A_KV),
        in_specs=[pl.BlockSpec((1, A_GRP, ntot, LANE), lambda b, g, f: (b, g, 0, 0)),
                  pl.BlockSpec((1, 1, LANE, ntot), lambda b, g, f: (b, g, 0, 0)),
                  pl.BlockSpec((1, 1, ntot, LANE), lambda b, g, f: (b, g, 0, 0))],
        out_specs=pl.BlockSpec((1, 1, ntot, A_GRP * A_HD), lambda b, g, f: (b, g, 0, 0)),
    )
    return pl.pallas_call(
        functools.partial(_attn_kernel, ntot=ntot),
        out_shape=jax.ShapeDtypeStruct((nb, A_KV, ntot, A_GRP * A_HD), BF16),
        grid_spec=grid_spec,
        compiler_params=_cparams(("arbitrary", "arbitrary")),
        name="attention",
    )(flag, q, kt, vd)


def _route_tail(x, y, m, gain_ref, wrt_ref, x1_ref, h2_ref, aff_ref, rows):
    x1 = x + m[2:3] * y
    x1_ref[0, rows] = x1
    h2 = _rms_mod(x1, gain_ref[...], m[3:4], m[4:5])
    h2_ref[0, rows] = h2.astype(BF16)
    logits = _dot3_nt(wrt_ref[...], h2)
    e = jnp.exp(logits - jnp.max(logits, axis=0, keepdims=True))
    aff_ref[0, :, rows] = e / jnp.sum(e, axis=0, keepdims=True)


def _dot3_nt(a, b):
    ah, al = _split2(a)
    bh, bl = _split2(b)
    return _dot_nt(ah, bh) + (_dot_nt(ah, bl) + _dot_nt(al, bh))


def _readout_mlstm_kernel(*refs, n_src):
    x = _tile_value(refs[:n_src])
    mod_ref, hf_ref, hb_ref, o_ref, mn_ref, wo_ref, gain_ref, wrt_ref, x1_ref, h2_ref, aff_ref = refs[n_src:]
    mn = mn_ref[...]
    for rows in HALVES:
        hh = hf_ref[0, rows].astype(F32) + hb_ref[0, rows].astype(F32)
        parts = []
        for h in range(M_HEADS):
            seg = hh[:, M_DV * h:M_DV * (h + 1)]
            ms = jnp.mean(seg * seg, axis=-1, keepdims=True)
            parts.append(seg * lax.rsqrt(ms + EPS) * mn[:, M_DV * h:M_DV * (h + 1)])
        z = jnp.concatenate(parts, axis=1) * jax.nn.sigmoid(o_ref[0, rows].astype(F32))
        y = _dot(z.astype(BF16), wo_ref[...])
        _route_tail(x[rows], y, mod_ref[0], gain_ref, wrt_ref, x1_ref, h2_ref, aff_ref, rows)


def _readout_attn_kernel(*refs, n_src):
    x = _tile_value(refs[:n_src])
    mod_ref, oa_ref, wo_ref, gain_ref, wrt_ref, x1_ref, h2_ref, aff_ref = refs[n_src:]
    gw = A_GRP * A_HD
    for rows in HALVES:
        y = _dot(oa_ref[0, 0, rows], wo_ref[0:gw, :])
        for g in range(1, A_KV):
            y = y + _dot(oa_ref[0, g, rows], wo_ref[gw * g:gw * (g + 1), :])
        _route_tail(x[rows], y, mod_ref[0], gain_ref, wrt_ref, x1_ref, h2_ref, aff_ref, rows)


def _readout(xa, mod_i, mixer_out, w_out, gain_ffn, w_router, mlstm_norm=None):
    srcs, src_specs, nb, ntot = _tile_sources(xa)
    nt = ntot // TILE
    wo = w_out.astype(BF16)
    wrt = w_router.T
    tail = [wo, gain_ffn, wrt]
    if mlstm_norm is not None:
        hf, hb, og = mixer_out
        kern = _readout_mlstm_kernel
        mid = [hf, hb, og, mlstm_norm]
        mid_in = [_tok_spec(M_V), _tok_spec(M_V), _tok_spec(M_V), _full_spec(mlstm_norm)]
    else:
        kern = _readout_attn_kernel
        mid = [mixer_out]
        mid_in = [pl.BlockSpec((1, A_KV, TILE, A_GRP * A_HD), lambda b, t: (b, 0, t, 0))]
    return pl.pallas_call(
        functools.partial(kern, n_src=len(srcs)),
        out_shape=(jax.ShapeDtypeStruct((nb, ntot, D), F32), jax.ShapeDtypeStruct((nb, ntot, D), BF16),
                   jax.ShapeDtypeStruct((nb, N_EXP, ntot), F32)),
        grid=(nb, nt),
        in_specs=src_specs + [_mod_spec(nb)] + mid_in + [_full_spec(a) for a in tail],
        out_specs=(_tok_spec(D), _tok_spec(D), pl.BlockSpec((1, N_EXP, TILE), lambda b, t: (b, 0, t))),
        compiler_params=_cparams(("arbitrary", "arbitrary")),
        name="readout_route",
    )(*srcs, mod_i, *mid, *tail)


def _slot_geometry(n_ctx, n_lat):
    cap_ctx = max(1, EC_CAPACITY * n_ctx // N_EXP)
    cap_lat = max(1, EC_CAPACITY * n_lat // N_EXP)
    lat_base = -(-cap_ctx // SUB) * SUB
    slots_max = lat_base + cap_lat + (SUB - 1) * (n_lat // TILE)
    rows_ffn = -(-slots_max // 16) * 16
    return cap_ctx, cap_lat, lat_base, rows_ffn


def _topk_kernel(a_ref, slot_ref, slot_t_ref, aff_t_ref, off_ref, cnt_ref, *, n_ctx, cap_ctx, cap_lat, lat_base):
    bits = lax.bitcast_convert_type(a_ref[0], I32)
    prefix = (_iota((LANE, LANE), 0) <= _iota((LANE, LANE), 1)).astype(BF16)

    def count(mask):
        return jnp.sum(mask.astype(F32), axis=1, keepdims=True)

    def select(x, cap):
        thr = jnp.zeros((N_EXP, 1), I32)
        for bit in range(30, -1, -1):
            cand = thr | (1 << bit)
            thr = jnp.where(count(x >= cand) >= cap, cand, thr)
        gt = x > thr
        eq = x == thr
        need = cap - count(gt)
        run = jnp.zeros((N_EXP, 1), F32)
        blocks = []
        for j in range(x.shape[1] // LANE):
            sl = slice(LANE * j, LANE * (j + 1))
            eqf = eq[:, sl].astype(F32)
            inc = _dot(eqf.astype(BF16), prefix)
            rank = run + inc - eqf
            blocks.append(jnp.logical_or(gt[:, sl], jnp.logical_and(eq[:, sl], rank < need)))
            run = run + inc[:, LANE - 1:LANE]
        return blocks

    blocks = select(bits[:, :n_ctx], cap_ctx) + select(bits[:, n_ctx:], cap_lat)
    per_tile = TILE // LANE
    lane = _iota((N_EXP, LANE), 1)
    off_acc = jnp.zeros((N_EXP, LANE), I32)
    cnt_acc = jnp.zeros((N_EXP, LANE), I32)
    base = jnp.zeros((N_EXP, 1), F32)
    for t in range(len(blocks) // per_tile):
        if t == n_ctx // TILE:
            base = jnp.full((N_EXP, 1), float(lat_base), F32)
        run = jnp.zeros((N_EXP, 1), F32)
        for j in range(per_tile):
            blk = blocks[per_tile * t + j]
            sf = blk.astype(F32)
            inc = _dot(sf.astype(BF16), prefix)
            pos = base + run + inc - sf
            c0 = LANE * (per_tile * t + j)
            slot_blk = jnp.concatenate([jnp.where(blk, pos.astype(I32), -1),
                                        jnp.full((LANE - N_EXP, LANE), -1, I32)], axis=0)
            slot_ref[0, :, c0:c0 + LANE] = slot_blk
            slot_t_ref[0, c0:c0 + LANE, :] = jnp.transpose(slot_blk)
            aff_blk = jnp.concatenate([a_ref[0, :, c0:c0 + LANE], jnp.zeros((LANE - N_EXP, LANE), F32)], axis=0)
            aff_t_ref[0, c0:c0 + LANE, :] = jnp.transpose(aff_blk)
            run = run + inc[:, LANE - 1:LANE]
        n8 = jnp.floor((run + (SUB - 1)) * (1.0 / SUB)) * SUB
        off_acc = jnp.where(lane == t, base.astype(I32), off_acc)
        cnt_acc = jnp.where(lane == t, n8.astype(I32), cnt_acc)
        base = base + n8
    off_ref[0] = off_acc
    cnt_ref[0] = cnt_acc


def _topk(aff_t, n_ctx):
    nb, _, ntot = aff_t.shape
    cap_ctx, cap_lat, lat_base, _ = _slot_geometry(n_ctx, ntot - n_ctx)
    kern = functools.partial(_topk_kernel, n_ctx=n_ctx, cap_ctx=cap_ctx, cap_lat=cap_lat, lat_base=lat_base)
    return pl.pallas_call(
        kern,
        out_shape=(jax.ShapeDtypeStruct((nb, LANE, ntot), I32), jax.ShapeDtypeStruct((nb, ntot, LANE), I32),
                   jax.ShapeDtypeStruct((nb, ntot, LANE), F32), jax.ShapeDtypeStruct((nb, N_EXP, LANE), I32),
                   jax.ShapeDtypeStruct((nb, N_EXP, LANE), I32)),
        grid=(nb,),
        in_specs=[pl.BlockSpec((1, N_EXP, ntot), lambda b: (b, 0, 0))],
        out_specs=(pl.BlockSpec((1, LANE, ntot), lambda b: (b, 0, 0)),
                   pl.BlockSpec((1, ntot, LANE), lambda b: (b, 0, 0)),
                   pl.BlockSpec((1, ntot, LANE), lambda b: (b, 0, 0)),
                   pl.BlockSpec((1, N_EXP, LANE), lambda b: (b, 0, 0)),
                   pl.BlockSpec((1, N_EXP, LANE), lambda b: (b, 0, 0))),
        compiler_params=_cparams(("arbitrary",)),
        name="ec_topk",
    )(aff_t)


def _strip_index(shape, dim):
    i = _iota(shape, dim)
    e = jnp.floor((i.astype(F32) + 0.5) * (1.0 / WIN)).astype(I32)
    return e, i - WIN * e


def _dispatch_kernel(off_s, rounds_s, h_ref, slot_ref, aff_ref, xg_ref, stage_ref, sem, cnt_ref):
    b = pl.program_id(0)
    t = pl.program_id(1)
    nt = pl.num_programs(1)
    base = (b * nt + t) * N_EXP
    rows = N_EXP * WIN
    a = aff_ref[0]
    src = _iota((LANE, LANE), 0)
    dst = _iota((LANE, LANE), 1)
    a3 = None
    for p, piece in enumerate(_split3(a)):
        sel = jnp.logical_and(dst == 3 * src + p, src < N_EXP).astype(BF16)
        term = _dot(piece, sel)
        a3 = term if a3 is None else a3 + term
    rhs = jnp.concatenate([h_ref[0], a3.astype(BF16)], axis=1)
    e_of_row, _ = _strip_index((rows, LANE), 0)
    expand = (e_of_row == _iota((rows, LANE), 1)).astype(BF16)
    _, j_row = _strip_index((rows, 1), 0)
    sl = slot_ref[0]
    e_row = _iota((LANE, 1), 0)
    off_v = jnp.zeros((LANE, 1), I32)
    for e in range(N_EXP):
        off_v = jnp.where(e_row == e, off_s[base + e], off_v)

    @pl.when(jnp.logical_and(b == 0, t == 0))
    def _():
        cnt_ref[0] = 0

    def strip_copies(buf, sample, starts):
        return [pltpu.make_async_copy(stage_ref.at[buf, pl.ds(WIN * e, WIN), :],
                                      xg_ref.at[sample, e, pl.ds(starts[e], WIN), :], sem.at[buf])
                for e in range(N_EXP)]

    def wait_strips(buf):
        for cp in strip_copies(buf, 0, [0] * N_EXP):
            cp.wait()

    def round_body(r, carry):
        n = cnt_ref[0]
        buf = n & 1
        rel = jnp.clip(sl - (off_v + WIN * r), -1, WIN).astype(F32).astype(BF16)
        relx = _dot(expand, rel)
        onehot = (relx == j_row.astype(F32)).astype(BF16)
        bits = lax.bitcast_convert_type(_dot(onehot, rhs), I32)
        packed = jnp.bitwise_or(jnp.bitwise_and(bits[:, :HALF] >> 16, 0xFFFF),
                                jnp.bitwise_and(bits[:, HALF:D], HIGH16))
        stage_ref[buf] = jnp.concatenate([packed, bits[:, D:]], axis=1)

        @pl.when(n > 0)
        def _():
            wait_strips(1 - buf)

        last_start = xg_ref.shape[2] - WIN
        starts = [pl.multiple_of(jnp.minimum(off_s[base + e] + WIN * r, last_start), SUB) for e in range(N_EXP)]
        for cp in strip_copies(buf, b, starts):
            cp.start()
        cnt_ref[0] = n + 1
        return carry

    lax.fori_loop(0, rounds_s[b * nt + t], round_body, 0)

    @pl.when(jnp.logical_and(b == pl.num_programs(0) - 1, t == nt - 1))
    def _():
        wait_strips((cnt_ref[0] - 1) & 1)


def _dispatch(h2, slot_pad, aff_pad, off_flat, rounds_flat, rows_alloc):
    nb, ntot, _ = h2.shape
    nt = ntot // TILE
    grid_spec = pltpu.PrefetchScalarGridSpec(
        num_scalar_prefetch=2,
        grid=(nb, nt),
        in_specs=[pl.BlockSpec((1, TILE, D), lambda b, t, o, r: (b, t, 0)),
                  pl.BlockSpec((1, LANE, TILE), lambda b, t, o, r: (b, 0, t)),
                  pl.BlockSpec((1, TILE, LANE), lambda b, t, o, r: (b, t, 0))],
        out_specs=pl.BlockSpec(memory_space=pl.ANY),
        scratch_shapes=[pltpu.VMEM((2, N_EXP * WIN, XP), I32), pltpu.SemaphoreType.DMA((2,)),
                        pltpu.SMEM((1,), I32)],
    )
    return pl.pallas_call(
        _dispatch_kernel,
        out_shape=jax.ShapeDtypeStruct((nb, N_EXP, rows_alloc, XP), I32),
        grid_spec=grid_spec,
        compiler_params=_cparams(("arbitrary", "arbitrary")),
        name="ec_dispatch",
    )(off_flat, rounds_flat, h2, slot_pad, aff_pad)


def _ffn_kernel(used_s, xg_ref, wg32_ref, wu32_ref, wd32_ref, y_ref, wg_ref, wu_ref, wd_ref, *, rows_short):
    e = pl.program_id(0)
    b = pl.program_id(1)

    @pl.when(b == 0)
    def _():
        wg_ref[0] = wg32_ref[0].astype(BF16)
        wu_ref[0] = wu32_ref[0].astype(BF16)
        wd_ref[0] = wd32_ref[0].astype(BF16)

    rows_all = y_ref.shape[2]

    def run(rows):
        words = xg_ref[0, 0, :rows, :HALF]
        x = jnp.concatenate([lax.bitcast_convert_type(words << 16, F32),
                             lax.bitcast_convert_type(jnp.bitwise_and(words, HIGH16), F32)], axis=1).astype(BF16)
        gl = lax.bitcast_convert_type(xg_ref[0, 0, :rows, HALF:], F32)
        lane = _iota(gl.shape, 1)
        mine = jnp.logical_and(lane >= 3 * e, lane < 3 * e + 3)
        gate = jnp.sum(jnp.where(mine, gl, 0.0), axis=1, keepdims=True)
        a = _dot(x, wg_ref[0])
        u = _dot(x, wu_ref[0])
        hm = (a * jax.nn.sigmoid(a) * u).astype(BF16)
        y_ref[0, 0, :rows] = _dot(hm, wd_ref[0]) * gate
        if rows < rows_all:
            y_ref[0, 0, rows:] = jnp.zeros((rows_all - rows, D), F32)

    used = used_s[b * N_EXP + e]

    @pl.when(used <= rows_short)
    def _():
        run(rows_short)

    @pl.when(used > rows_short)
    def _():
        run(rows_all)


def _expert_ffn(xg, wg, wu, wd, used_flat, rows_ffn, layer=0):
    nb = xg.shape[0]
    rows_short = max(16, rows_ffn - 48)
    if wg.ndim == 4:
        wspec = pl.BlockSpec((None, 1, D, D), lambda e, b, u: (layer, e, 0, 0))
    else:
        wspec = pl.BlockSpec((1, D, D), lambda e, b, u: (e, 0, 0))
    grid_spec = pltpu.PrefetchScalarGridSpec(
        num_scalar_prefetch=1,
        grid=(N_EXP, nb),
        in_specs=[pl.BlockSpec((1, 1, rows_ffn, XP), lambda e, b, u: (b, e, 0, 0)), wspec, wspec, wspec],
        out_specs=pl.BlockSpec((1, 1, rows_ffn, D), lambda e, b, u: (b, e, 0, 0)),
        scratch_shapes=[pltpu.VMEM((1, D, D), BF16)] * 3,
    )
    return pl.pallas_call(
        functools.partial(_ffn_kernel, rows_short=rows_short),
        out_shape=jax.ShapeDtypeStruct((nb, N_EXP, rows_ffn, D), F32),
        grid_spec=grid_spec,
        compiler_params=_cparams(("arbitrary", "arbitrary")),
        name="ec_ffn",
    )(used_flat, xg, wg, wu, wd)


def _combine_kernel(off_s, rounds_s, x_ref, mod_ref, slot_ref, fin_ref, y_ref, *rest, final_ctx_tiles, proj_body,
                    n_proj_in):
    if proj_body is None:
        o_ref, strip_ref, sem = rest
    else:
        next_mod_ref = rest[0]
        proj_in = rest[1:1 + n_proj_in]
        o_ref = rest[1 + n_proj_in]
        proj_out = rest[2 + n_proj_in:-2]
        strip_ref, sem = rest[-2:]
    b = pl.program_id(0)
    t = pl.program_id(1)
    nt = pl.num_programs(1)
    base = (b * nt + t) * N_EXP
    cols = N_EXP * WIN
    e_of_col, _ = _strip_index((LANE, cols), 1)
    expand = (e_of_col == _iota((LANE, cols), 0)).astype(BF16)
    _, j_lane = _strip_index((1, cols), 1)
    j_lane = j_lane.astype(F32)
    sl = slot_ref[0]
    e_lane = _iota((1, LANE), 1)
    last_start = y_ref.shape[2] - WIN

    def strip_start(step_, e, r):
        return jnp.minimum(off_s[step_ * N_EXP + e] + WIN * r, last_start)

    step = b * nt + t
    buf = step & 1

    def strip_copies(step_, sample, r, dst):
        return [pltpu.make_async_copy(
            y_ref.at[sample, e, pl.ds(pl.multiple_of(strip_start(step_, e, r), SUB), WIN), :],
            strip_ref.at[dst, pl.ds(WIN * e, WIN), :], sem.at[dst]) for e in range(N_EXP)]

    @pl.when(step == 0)
    def _():
        for cp in strip_copies(step, b, 0, buf):
            cp.start()

    @pl.when(step + 1 < pl.num_programs(0) * nt)
    def _():
        for cp in strip_copies(step + 1, jnp.where(t == nt - 1, b + 1, b), 0, 1 - buf):
            cp.start()

    def expand_round(r, acc):
        first_v = jnp.zeros((1, LANE), I32)
        start_v = jnp.zeros((1, LANE), I32)
        for e in range(N_EXP):
            first_v = jnp.where(e_lane == e, off_s[base + e] + WIN * r, first_v)
            start_v = jnp.where(e_lane == e, strip_start(step, e, r), start_v)
        nominal = sl - first_v
        in_round = jnp.logical_and(nominal >= 0, nominal < WIN)
        rel = jnp.where(in_round, sl - start_v, -1).astype(F32).astype(BF16)
        relx = _dot(rel, expand)
        onehot = (relx == j_lane).astype(BF16)
        for cp in strip_copies(step, b, r, buf):
            cp.wait()
        hi, lo = _split2(strip_ref[buf])
        return acc + (_dot(onehot, hi) + _dot(onehot, lo))

    def extra_round(r, acc):
        for cp in strip_copies(step, b, r, buf):
            cp.start()
        return expand_round(r, acc)

    acc = expand_round(0, jnp.zeros((TILE, D), F32))
    acc = lax.fori_loop(1, rounds_s[step], extra_round, acc)
    x2 = x_ref[0] + mod_ref[0][5:6] * acc
    if final_ctx_tiles is None:
        o_ref[0] = x2
        if proj_body is not None:
            proj_body(x2, next_mod_ref[0], *proj_in, *proj_out)
    else:
        @pl.when(t >= final_ctx_tiles)
        def _():
            ms = jnp.mean(x2 * x2, axis=-1, keepdims=True)
            o_ref[0] = x2 * lax.rsqrt(ms + EPS) * fin_ref[...]


def _combine(x1, mod_i, slot_t_pad, y, off_flat, rounds_flat, final_gain, final_ctx_tiles=None, next_proj=None):
    nb, ntot, _ = x1.shape
    nt = ntot // TILE
    skip = 0 if final_ctx_tiles is None else final_ctx_tiles
    in_specs = [_tok_spec(D), _mod_spec(nb), _tok_spec(LANE), pl.BlockSpec((1, D), lambda b, t, *_: (0, 0)),
                pl.BlockSpec(memory_space=pl.ANY)]
    args = [x1, mod_i, slot_t_pad, final_gain, y]
    out_shape = [jax.ShapeDtypeStruct((nb, ntot - skip * TILE, D), F32)]
    out_specs = [pl.BlockSpec((1, TILE, D), lambda b, t, *_: (b, jnp.maximum(t - skip, 0), 0))]
    proj_body, n_proj_in = None, 0
    if next_proj is not None:
        proj_body, next_mod, (arrays, specs, p_shape, p_specs) = next_proj
        in_specs += [_mod_spec(nb)] + specs
        args += [next_mod] + arrays
        out_shape += p_shape
        out_specs += p_specs
        n_proj_in = len(arrays)
    grid_spec = pltpu.PrefetchScalarGridSpec(
        num_scalar_prefetch=2,
        grid=(nb, nt),
        in_specs=in_specs,
        out_specs=tuple(out_specs),
        scratch_shapes=[pltpu.VMEM((2, N_EXP * WIN, D), F32), pltpu.SemaphoreType.DMA((2,))],
    )
    outs = pl.pallas_call(
        functools.partial(_combine_kernel, final_ctx_tiles=final_ctx_tiles, proj_body=proj_body,
                          n_proj_in=n_proj_in),
        out_shape=tuple(out_shape),
        grid_spec=grid_spec,
        compiler_params=_cparams(("arbitrary", "arbitrary")),
        name="ec_combine",
    )(off_flat, rounds_flat, *args)
    return outs[0], tuple(outs[1:])


def _moe(x1, h2, aff_t, mod_i, wg, wu, wd, n_ctx, final_gain, last, layer=0, next_proj=None):
    nb, ntot, _ = x1.shape
    nt = ntot // TILE
    _, _, _, rows_ffn = _slot_geometry(n_ctx, ntot - n_ctx)
    rows_alloc = rows_ffn + WIN
    slot_pad, slot_t_pad, aff_pad, off, cnt = _topk(aff_t, n_ctx)
    off_t = jnp.swapaxes(off[:, :, :nt], 1, 2)
    cnt_t = jnp.swapaxes(cnt[:, :, :nt], 1, 2)
    rounds_c = jnp.maximum(1, jnp.max((cnt_t + WIN - 1) // WIN, axis=2))
    fill = jnp.max((rows_alloc - off_t[:, nt - 1, :] + WIN - 1) // WIN, axis=1)
    rounds_d = rounds_c.at[:, nt - 1].max(fill)
    off_flat = off_t.reshape(-1)
    xg = _dispatch(h2, slot_pad, aff_pad, off_flat, rounds_d.reshape(-1), rows_alloc)
    used = (off_t[:, nt - 1, :] + cnt_t[:, nt - 1, :]).reshape(-1)
    y = _expert_ffn(xg, wg, wu, wd, used, rows_ffn, layer)
    return _combine(x1, mod_i, slot_t_pad, y, off_flat, rounds_c.reshape(-1), final_gain,
                    final_ctx_tiles=n_ctx // TILE if last else None, next_proj=next_proj)


def kernel(x, c, ctx, c_ctx, w_mod, b_mod, norm_mix, norm_ffn, mlstm_w_in, mlstm_b_gate, mlstm_norm, mlstm_w_out,
           attn_w_in, attn_q_norm, attn_k_norm, attn_w_out, moe_router, moe_w_gate, moe_w_up, moe_w_down,
           norm_final):
    nb, n_lat, _ = x.shape
    n_ctx = ctx.shape[1]
    depth = w_mod.shape[0]
    assert n_ctx == TILE and n_lat % TILE == 0 and x.shape[2] == D
    xa = (ctx, x)
    rb = -(-(nb + 1) // SUB) * SUB
    cc = jnp.concatenate([c, c_ctx[None, :], jnp.zeros((rb - nb - 1, D), F32)], axis=0)
    mod = _modulation(cc, w_mod, b_mod)
    cos, sin = _rope_tables(n_lat, n_ctx)
    ntot = n_ctx + n_lat

    def proj_operands(i):
        j = i // 2
        gain_mix = norm_mix[i].reshape(1, D)
        if i % 2 == 0:
            return _proj_mlstm_body, _proj_mlstm_operands(nb, ntot, gain_mix, mlstm_w_in[j], mlstm_b_gate[j]), None
        ops, flag = _proj_attn_operands(nb, ntot, gain_mix, attn_w_in[j], attn_q_norm[j], attn_k_norm[j], cos, sin)
        return _proj_attn_body, ops, flag

    _, ops, flag = proj_operands(0)
    proj = _proj_mlstm(xa, mod[0], ops)
    for i in range(depth):
        j = i // 2
        mod_i = mod[i]
        gain_ffn = norm_ffn[i].reshape(1, D)
        if i % 2 == 0:
            q, kt, v, og, gc, gr = proj
            hf, hb = _mlstm_scan(q, kt, v, gc, gr)
            x1, h2, aff_t = _readout(xa, mod_i, (hf, hb, og), mlstm_w_out[j], gain_ffn, moe_router[i],
                                     mlstm_norm=mlstm_norm[j].reshape(1, M_V))
        else:
            q, kt, vd = proj
            oa = _attention(q, kt, vd, flag)
            x1, h2, aff_t = _readout(xa, mod_i, oa, attn_w_out[j], gain_ffn, moe_router[i])
        last = i == depth - 1
        next_proj = None
        if not last:
            body, ops, flag = proj_operands(i + 1)
            next_proj = (body, mod[i + 1], ops)
        xa, proj = _moe(x1, h2, aff_t, mod_i, moe_w_gate, moe_w_up, moe_w_down, n_ctx,
                        norm_final.reshape(1, D), last=last, layer=i, next_proj=next_proj)
    return xa
```

```python
import functools

import jax
import jax.numpy as jnp
from jax import lax
from jax.experimental import pallas as pl
from jax.experimental.pallas import tpu as pltpu

F32 = jnp.float32
BF16 = jnp.bfloat16
I32 = jnp.int32

D = 1024
TILE = 256
HALVES = (slice(0, TILE // 2), slice(TILE // 2, TILE))
EPS = 1e-6
DEPTH = 4

M_HEADS = 4
M_DK = 128
M_DV = 256
M_QK = M_HEADS * M_DK
M_V = M_HEADS * M_DV
M_AUG = M_DV + 128
GATE_CAP = 15.0

A_HEADS = 16
A_KV = 4
A_GRP = 4
A_HD = 64
ROPE_THETA = 10000.0
GRID_W = 64
LOG2E = 1.4426950408889634

N_EXP = 16
EC_CAPACITY = 2
WIN = 48
GATE_LANES = 128
XW = D + GATE_LANES
HALF = D // 2
XP = HALF + GATE_LANES
HIGH16 = -65536

LANE = 128
SUB = 8
VMEM_LIMIT = 56 * 1024 * 1024


def _cparams(sem):
    return pltpu.CompilerParams(dimension_semantics=sem, vmem_limit_bytes=VMEM_LIMIT)


def _dot(a, b):
    return jnp.dot(a, b, preferred_element_type=F32)


def _dot_nt(a, b):
    return lax.dot_general(a, b, (((1,), (1,)), ((), ())), preferred_element_type=F32)


def _split2(x):
    hi = x.astype(BF16)
    lo = (x - hi.astype(F32)).astype(BF16)
    return hi, lo


def _split3(x):
    hi = x.astype(BF16)
    r = x - hi.astype(F32)
    mid = r.astype(BF16)
    lo = (r - mid.astype(F32)).astype(BF16)
    return hi, mid, lo


def _dot3(a, b):
    ah, al = _split2(a)
    bh, bl = _split2(b)
    return _dot(ah, bh) + (_dot(ah, bl) + _dot(al, bh))


def _rms_mod(x, gain, shift, scale):
    ms = jnp.mean(x * x, axis=-1, keepdims=True)
    y = x * lax.rsqrt(ms + EPS) * gain
    return y * (1.0 + scale) + shift


def _iota(shape, dim):
    return lax.broadcasted_iota(I32, shape, dim)


def _mod_kernel(c_ref, w_ref, b_ref, o_ref):
    c = c_ref[...]
    s = c * jax.nn.sigmoid(c)
    o_ref[...] = _dot3(s, w_ref[...]) + b_ref[...]


def _modulation(cc, w_mod, b_mod):
    depth, _, n6 = w_mod.shape
    rb = cc.shape[0]
    nj = n6 // D
    out = pl.pallas_call(
        _mod_kernel,
        out_shape=jax.ShapeDtypeStruct((depth, rb, n6), F32),
        grid=(depth, nj),
        in_specs=[
            pl.BlockSpec((rb, D), lambda i, j: (0, 0)),
            pl.BlockSpec((None, D, D), lambda i, j: (i, 0, j)),
            pl.BlockSpec((None, 1, D), lambda i, j: (i, 0, j)),
        ],
        out_specs=pl.BlockSpec((None, rb, D), lambda i, j: (i, 0, j)),
        compiler_params=_cparams(("arbitrary", "arbitrary")),
        name="adaln_mod",
    )(cc, w_mod, b_mod.reshape(depth, 1, n6))
    return out.reshape(depth, rb, nj, D)


def _gate_act(g, idx):
    g = GATE_CAP * jnp.tanh(g * (1.0 / GATE_CAP))
    logsig = jnp.minimum(g, 0.0) - jnp.log(1.0 + jnp.exp(-jnp.abs(g)))
    is_forget = ((idx >> 2) & 1) == 1
    return jnp.where(is_forget, logsig, g)


def _proj_mlstm_body(x, m, gain_ref, w_ref, wkt_ref, wg_ref, wgt_ref, bc_ref, br_ref,
                     q_ref, kt_ref, v_ref, o_ref, gc_ref, gr_ref):
    h = _rms_mod(x, gain_ref[...], m[0:1], m[1:2])
    hb = h.astype(BF16)
    r = _dot(hb, w_ref[...])
    q_ref[0] = (r[:, :M_QK] * (M_DK ** -0.5)).astype(BF16)
    v_ref[0] = r[:, M_QK:M_QK + M_V].astype(BF16)
    o_ref[0] = r[:, M_QK + M_V:].astype(BF16)
    kt_ref[0] = _dot_nt(wkt_ref[...], hb).astype(BF16)
    gc = _dot(hb, wg_ref[...]) + bc_ref[...]
    gc_ref[0] = _gate_act(gc, _iota(gc.shape, 1))
    gr = _dot_nt(wgt_ref[...], hb) + br_ref[...]
    gr_ref[0] = _gate_act(gr, _iota(gr.shape, 0))


def _proj_mlstm_kernel(*refs, n_src):
    _proj_mlstm_body(_tile_value(refs[:n_src]), refs[n_src][0], *refs[n_src + 1:])


def _tile_sources(xa):
    if isinstance(xa, tuple):
        ctx, lat = xa
        assert ctx.shape[1] == TILE
        specs = [pl.BlockSpec((1, TILE, D), lambda b, t, *_: (b, 0, 0)),
                 pl.BlockSpec((1, TILE, D), lambda b, t, *_: (b, jnp.maximum(t - 1, 0), 0))]
        return [ctx, lat], specs, lat.shape[0], ctx.shape[1] + lat.shape[1]
    return [xa], [_tok_spec(D)], xa.shape[0], xa.shape[1]


def _tile_value(src_refs):
    if len(src_refs) == 1:
        return src_refs[0][0]
    return jnp.where(pl.program_id(1) == 0, src_refs[0][0], src_refs[1][0])


def _tok_spec(width):
    return pl.BlockSpec((1, TILE, width), lambda b, t, *_: (b, t, 0))


def _full_spec(a):
    return pl.BlockSpec(a.shape, lambda b, t, *_: (0,) * a.ndim)


def _mod_spec(nb):
    return pl.BlockSpec((1, 6, D), lambda b, t, *_: (jnp.where(t == 0, nb, b), 0, 0))


def _proj_mlstm_operands(nb, ntot, gain, w_in, b_gate):
    n_g = 4 * M_HEADS
    wq = w_in[:, :M_QK]
    wk = w_in[:, M_QK:2 * M_QK]
    wvo = w_in[:, 2 * M_QK:2 * M_QK + 2 * M_V]
    wg = w_in[:, 2 * M_QK + 2 * M_V:]
    w_main = jnp.concatenate([wq, wvo], axis=1).astype(BF16)
    wkt = wk.T.astype(BF16)
    wg_pad = jnp.pad(wg, ((0, 0), (0, LANE - n_g))).astype(BF16)
    wgt = wg.T.astype(BF16)
    bc = jnp.pad(b_gate, (0, LANE - n_g)).reshape(1, LANE)
    br = b_gate.reshape(n_g, 1)
    arrays = [gain, w_main, wkt, wg_pad, wgt, bc, br]
    out_shape = [
        jax.ShapeDtypeStruct((nb, ntot, M_QK), BF16),
        jax.ShapeDtypeStruct((nb, M_QK, ntot), BF16),
        jax.ShapeDtypeStruct((nb, ntot, M_V), BF16),
        jax.ShapeDtypeStruct((nb, ntot, M_V), BF16),
        jax.ShapeDtypeStruct((nb, ntot, LANE), F32),
        jax.ShapeDtypeStruct((nb, n_g, ntot), F32),
    ]
    out_specs = [_tok_spec(M_QK), pl.BlockSpec((1, M_QK, TILE), lambda b, t, *_: (b, 0, t)), _tok_spec(M_V),
                 _tok_spec(M_V), _tok_spec(LANE), pl.BlockSpec((1, n_g, TILE), lambda b, t, *_: (b, 0, t))]
    return arrays, [_full_spec(a) for a in arrays], out_shape, out_specs


def _proj_mlstm(xa, mod_i, operands):
    srcs, src_specs, nb, ntot = _tile_sources(xa)
    arrays, in_specs, out_shape, out_specs = operands
    return pl.pallas_call(
        functools.partial(_proj_mlstm_kernel, n_src=len(srcs)),
        out_shape=tuple(out_shape),
        grid=(nb, ntot // TILE),
        in_specs=src_specs + [_mod_spec(nb)] + in_specs,
        out_specs=tuple(out_specs),
        compiler_params=_cparams(("arbitrary", "arbitrary")),
        name="proj_mlstm",
    )(*srcs, mod_i, *arrays)


def _mlstm_kernel(qf_ref, qb_ref, kf_ref, kb_ref, vf_ref, vb_ref, gcf_ref, gcb_ref, grf_ref, grb_ref,
                  hf_ref, hb_ref, c_ref, m_ref):
    t = pl.program_id(1)

    @pl.when(t == 0)
    def _():
        c_ref[...] = jnp.zeros(c_ref.shape, F32)
        m_ref[...] = jnp.zeros(m_ref.shape, F32)

    n = TILE
    row = _iota((n, n), 0)
    col = _iota((n, n), 1)
    lower = col <= row
    upper = col >= row
    lower_b = lower.astype(BF16)
    upper_b = upper.astype(BF16)
    ones_col = (_iota((n, M_AUG - M_DV), 1) == 0).astype(BF16)
    dirs = ((qf_ref, kf_ref, vf_ref, gcf_ref, grf_ref, hf_ref, lower, lower_b, upper_b, n - 1),
            (qb_ref, kb_ref, vb_ref, gcb_ref, grb_ref, hb_ref, upper, upper_b, lower_b, 0))
    c_old = [c_ref[i] for i in range(2 * M_HEADS)]
    m_old = [m_ref[i][0:1, 0:1] for i in range(2 * M_HEADS)]
    c_new, m_new_all, h_out = {}, {}, {0: [], 1: []}
    for d, (q_ref, k_ref, v_ref, gc_ref, gr_ref, o_ref, mask, cum_l, cum_r, last) in enumerate(dirs):
        gc = gc_ref[0]
        gr = gr_ref[0]
        bcol = sum(_dot(cum_l, p) for p in _split3(gc))
        brow = sum(_dot(p, cum_r) for p in _split3(gr))
        for h in range(M_HEADS):
            gi = 8 * d + h
            gf = gi + 4
            sidx = 4 * d + h
            b_col = bcol[:, gf:gf + 1]
            b_row = brow[gf:gf + 1, :]
            ig_row = gr[gi:gi + 1, :]
            total = b_row[:, last:last + 1]
            m_st = m_old[sidx]
            key_row = ig_row - b_row
            log_d = jnp.where(mask, b_col + key_row, -jnp.inf)
            m_inter = b_col + m_st
            m_q = jnp.maximum(m_inter, jnp.max(log_d, axis=1, keepdims=True))
            w_intra = jnp.exp(log_d - m_q)
            w_inter = jnp.exp(m_inter - m_q)
            qh = q_ref[0, :, M_DK * h:M_DK * (h + 1)]
            kth = k_ref[0, M_DK * h:M_DK * (h + 1), :]
            v_aug = jnp.concatenate([v_ref[0, :, M_DV * h:M_DV * (h + 1)], ones_col], axis=1)
            c_aug = c_old[sidx]
            s = (_dot(qh, kth) * w_intra).astype(BF16)
            nd = _dot(s, v_aug) + w_inter * _dot(qh, c_aug.astype(BF16))
            den = nd[:, M_DV:M_DV + 1]
            inv = 1.0 / jnp.maximum(jnp.abs(den), jnp.exp(-m_q))
            h_out[d].append((nd[:, :M_DV] * inv).astype(o_ref.dtype))
            log_w = total + key_row
            m_new = jnp.maximum(total + m_st, jnp.max(log_w, axis=1, keepdims=True))
            w_key = jnp.exp(log_w - m_new)
            decay = jnp.exp(total + m_st - m_new)
            kw = (kth.astype(F32) * w_key).astype(BF16)
            c_new[sidx] = decay * c_aug + _dot(kw, v_aug)
            m_new_all[sidx] = m_new
    hf_ref[0] = jnp.concatenate(h_out[0], axis=1)
    hb_ref[0] = jnp.concatenate(h_out[1], axis=1)
    for i in range(2 * M_HEADS):
        c_ref[i] = c_new[i]
        m_ref[i] = jnp.broadcast_to(m_new_all[i], m_ref.shape[1:])


def _mlstm_scan(q, kt, v, gc, gr):
    nb, ntot, _ = q.shape
    nt = ntot // TILE
    fwd = lambda b, t: (b, t, 0)
    bwd = lambda b, t: (b, jnp.where(t == 0, 0, nt - t), 0)
    fwd_t = lambda b, t: (b, 0, t)
    bwd_t = lambda b, t: (b, 0, jnp.where(t == 0, 0, nt - t))
    n_g = gr.shape[1]
    return pl.pallas_call(
        _mlstm_kernel,
        out_shape=(jax.ShapeDtypeStruct((nb, ntot, M_V), BF16), jax.ShapeDtypeStruct((nb, ntot, M_V), BF16)),
        grid=(nb, nt),
        in_specs=[
            pl.BlockSpec((1, TILE, M_QK), fwd), pl.BlockSpec((1, TILE, M_QK), bwd),
            pl.BlockSpec((1, M_QK, TILE), fwd_t), pl.BlockSpec((1, M_QK, TILE), bwd_t),
            pl.BlockSpec((1, TILE, M_V), fwd), pl.BlockSpec((1, TILE, M_V), bwd),
            pl.BlockSpec((1, TILE, LANE), fwd), pl.BlockSpec((1, TILE, LANE), bwd),
            pl.BlockSpec((1, n_g, TILE), fwd_t), pl.BlockSpec((1, n_g, TILE), bwd_t),
        ],
        out_specs=(pl.BlockSpec((1, TILE, M_V), fwd), pl.BlockSpec((1, TILE, M_V), bwd)),
        scratch_shapes=[pltpu.VMEM((2 * M_HEADS, M_DK, M_AUG), F32), pltpu.VMEM((2 * M_HEADS, SUB, LANE), F32)],
        compiler_params=_cparams(("arbitrary", "arbitrary")),
        name="mlstm_scan",
    )(q, q, kt, kt, v, v, gc, gc, gr, gr)


def _head_norm(x, gain):
    w = x.shape[1]
    gsum = ((_iota((w, LANE), 0) >> 6) == _iota((w, LANE), 1)).astype(BF16)
    gexp = ((_iota((LANE, w), 1) >> 6) == _iota((LANE, w), 0)).astype(BF16)
    hi, lo = _split2(x * x)
    ssum = _dot(hi, gsum) + _dot(lo, gsum)
    rh, rl = _split2(lax.rsqrt(ssum * (1.0 / A_HD) + EPS))
    return x * (_dot(rh, gexp) + _dot(rl, gexp)) * gain


def _rope(x, cos, sin_signed):
    first = (_iota((x.shape[0], LANE), 1) & (A_HD - 1)) < A_HD // 2
    tiles = []
    for i in range(x.shape[1] // LANE):
        xt = x[:, LANE * i:LANE * (i + 1)]
        partner = jnp.where(first, pltpu.roll(xt, LANE - A_HD // 2, 1), pltpu.roll(xt, A_HD // 2, 1))
        tiles.append(xt * cos + partner * sin_signed)
    return jnp.concatenate(tiles, axis=1)


def _proj_attn_body(x, m, gain_ref, w_ref, qg_ref, kg_ref, cos_ref, sin_ref, shift_ref, q_ref, kt_ref, vd_ref):
    h = _rms_mod(x, gain_ref[...], m[0:1], m[1:2])
    r = _dot(h.astype(BF16), w_ref[...])
    nq = A_HEADS * A_HD
    nk = A_KV * A_HD
    cos = cos_ref[...]
    sin = sin_ref[...]
    qn = _rope(_head_norm(r[:, :nq], qg_ref[...]), cos, sin)
    kn = _rope(_head_norm(r[:, nq:nq + nk], kg_ref[...]), cos, sin)
    v = r[:, nq + nk:]
    lane = _iota((TILE, LANE), 1)
    low = lane < A_HD
    one_hot = jnp.where(lane == A_HD, 1.0, 0.0)

    def head_tile(a, idx):
        tile = a[:, LANE * (idx // 2):LANE * (idx // 2 + 1)]
        if idx % 2 == 1:
            tile = pltpu.roll(tile, A_HD, 1)
        return jnp.where(low, tile, one_hot).astype(BF16)

    for hd in range(A_HEADS):
        q_ref[0, hd] = head_tile(qn, hd)
    for g in range(A_KV):
        vd_ref[0, g] = head_tile(v, g)
    kt = jnp.transpose(kn)
    extra = jnp.where(_iota((LANE - A_HD, TILE), 0) == 0, shift_ref[...], 0.0).astype(BF16)
    for g in range(A_KV):
        kt_ref[0, g, 0:A_HD, :] = kt[A_HD * g:A_HD * (g + 1), :].astype(BF16)
        kt_ref[0, g, A_HD:LANE, :] = extra


def _rope_tables(n_lat, n_ctx):
    rows = n_lat // GRID_W
    row = jnp.repeat(jnp.arange(rows, dtype=F32), GRID_W)
    col = jnp.tile(jnp.arange(GRID_W, dtype=F32), rows)
    pairs = A_HD // 4
    inv = ROPE_THETA ** (-jnp.arange(pairs, dtype=F32) / pairs)
    ang = jnp.concatenate([row[:, None] * inv, col[:, None] * inv], axis=-1)
    c = jnp.cos(ang)
    s = jnp.sin(ang)
    cos = jnp.concatenate([c, c, c, c], axis=-1)
    sin = jnp.concatenate([-s, s, -s, s], axis=-1)
    cos = jnp.concatenate([jnp.ones((n_ctx, LANE), F32), cos], axis=0)
    sin = jnp.concatenate([jnp.zeros((n_ctx, LANE), F32), sin], axis=0)
    return cos, sin


SHIFT_LIMIT = 60.0


def _softmax_shift(qg, kg):
    bound = A_HD * jnp.max(jnp.abs(qg)) * jnp.max(jnp.abs(kg))
    fast = bound <= SHIFT_LIMIT
    shift = jnp.where(fast, jnp.ceil(bound), 0.0)
    return shift, jnp.logical_not(fast).astype(I32)


def _proj_attn_operands(nb, ntot, gain, w_in, q_norm, k_norm, cos, sin):
    nq = A_HEADS * A_HD
    nk = A_KV * A_HD
    w = w_in.astype(BF16)
    qg = (jnp.tile(q_norm, A_HEADS) * (A_HD ** -0.5 * LOG2E)).reshape(1, nq)
    kg = jnp.tile(k_norm, A_KV).reshape(1, nk)
    shift, flag = _softmax_shift(qg, kg)
    neg_shift = (-shift).reshape(1, 1).astype(F32)
    tab = pl.BlockSpec((TILE, LANE), lambda b, t, *_: (t, 0))
    arrays = [gain, w, qg, kg, cos, sin, neg_shift]
    in_specs = [_full_spec(gain), _full_spec(w), _full_spec(qg), _full_spec(kg), tab, tab, _full_spec(neg_shift)]
    out_shape = [
        jax.ShapeDtypeStruct((nb, A_HEADS, ntot, LANE), BF16),
        jax.ShapeDtypeStruct((nb, A_KV, LANE, ntot), BF16),
        jax.ShapeDtypeStruct((nb, A_KV, ntot, LANE), BF16),
    ]
    out_specs = [pl.BlockSpec((1, A_HEADS, TILE, LANE), lambda b, t, *_: (b, 0, t, 0)),
                 pl.BlockSpec((1, A_KV, LANE, TILE), lambda b, t, *_: (b, 0, 0, t)),
                 pl.BlockSpec((1, A_KV, TILE, LANE), lambda b, t, *_: (b, 0, t, 0))]
    return (arrays, in_specs, out_shape, out_specs), flag.reshape(1)


def _attn_kernel(flag_ref, q_ref, kt_ref, vd_ref, o_ref, *, ntot):
    low = _iota((TILE, LANE), 1) < A_HD

    def attend(row0, nk, row_max):
        outs = []
        for hd in range(A_GRP):
            s = _dot(q_ref[0, hd, pl.ds(row0, TILE), :], kt_ref[0, 0, :, :nk])
            if row_max:
                s = s - jnp.max(s, axis=1, keepdims=True)
            r = _dot(jnp.exp2(s).astype(BF16), vd_ref[0, 0, :nk, :])
            outs.append(r / r[:, A_HD:A_HD + 1])
        t0 = jnp.where(low, outs[0], pltpu.roll(outs[1], A_HD, 1))
        t1 = jnp.where(low, outs[2], pltpu.roll(outs[3], A_HD, 1))
        o_ref[0, 0, pl.ds(row0, TILE), :] = jnp.concatenate([t0, t1], axis=1).astype(BF16)

    attend(0, TILE, True)

    def latent_tiles(row_max):
        def body(i, carry):
            attend(pl.multiple_of(i * TILE, TILE), ntot, row_max)
            return carry
        lax.fori_loop(1, ntot // TILE, body, 0)

    @pl.when(flag_ref[0] == 0)
    def _():
        latent_tiles(False)

    @pl.when(flag_ref[0] != 0)
    def _():
        latent_tiles(True)


def _attention(q, kt, vd, flag):
    nb, _, ntot, _ = q.shape
    grid_spec = pltpu.PrefetchScalarGridSpec(
        num_scalar_prefetch=1,
        grid=(nb, A_KV),
        in_specs=[pl.BlockSpec((1, A_GRP, ntot, LANE), lambda b, g, f: (b, g, 0, 0)),
                  pl.BlockSpec((1, 1, LANE, ntot), lambda b, g, f: (b, g, 0, 0)),
                  pl.BlockSpec((1, 1, ntot, LANE), lambda b, g, f: (b, g, 0, 0))],
        out_specs=pl.BlockSpec((1, 1, ntot, A_GRP * A_HD), lambda b, g, f: (b, g, 0, 0)),
    )
    return pl.pallas_call(
        functools.partial(_attn_kernel, ntot=ntot),
        out_shape=jax.ShapeDtypeStruct((nb, A_KV, ntot, A_GRP * A_HD), BF16),
        grid_spec=grid_spec,
        compiler_params=_cparams(("arbitrary", "arbitrary")),
        name="attention",
    )(flag, q, kt, vd)


def _route_tail(x, y, m, gain_ref, wrt_ref, x1_ref, h2_ref, aff_ref, rows):
    x1 = x + m[2:3] * y
    x1_ref[0, rows] = x1
    h2 = _rms_mod(x1, gain_ref[...], m[3:4], m[4:5])
    h2_ref[0, rows] = h2.astype(BF16)
    logits = _dot3_nt(wrt_ref[...], h2)
    e = jnp.exp(logits - jnp.max(logits, axis=0, keepdims=True))
    aff_ref[0, :, rows] = e / jnp.sum(e, axis=0, keepdims=True)


def _dot3_nt(a, b):
    ah, al = _split2(a)
    bh, bl = _split2(b)
    return _dot_nt(ah, bh) + (_dot_nt(ah, bl) + _dot_nt(al, bh))


def _readout_mlstm_kernel(*refs, n_src):
    x = _tile_value(refs[:n_src])
    mod_ref, hf_ref, hb_ref, o_ref, mn_ref, wo_ref, gain_ref, wrt_ref, x1_ref, h2_ref, aff_ref = refs[n_src:]
    mn = mn_ref[...]
    for rows in HALVES:
        hh = hf_ref[0, rows].astype(F32) + hb_ref[0, rows].astype(F32)
        parts = []
        for h in range(M_HEADS):
            seg = hh[:, M_DV * h:M_DV * (h + 1)]
            ms = jnp.mean(seg * seg, axis=-1, keepdims=True)
            parts.append(seg * lax.rsqrt(ms + EPS) * mn[:, M_DV * h:M_DV * (h + 1)])
        z = jnp.concatenate(parts, axis=1) * jax.nn.sigmoid(o_ref[0, rows].astype(F32))
        y = _dot(z.astype(BF16), wo_ref[...])
        _route_tail(x[rows], y, mod_ref[0], gain_ref, wrt_ref, x1_ref, h2_ref, aff_ref, rows)


def _readout_attn_kernel(*refs, n_src):
    x = _tile_value(refs[:n_src])
    mod_ref, oa_ref, wo_ref, gain_ref, wrt_ref, x1_ref, h2_ref, aff_ref = refs[n_src:]
    gw = A_GRP * A_HD
    for rows in HALVES:
        y = _dot(oa_ref[0, 0, rows], wo_ref[0:gw, :])
        for g in range(1, A_KV):
            y = y + _dot(oa_ref[0, g, rows], wo_ref[gw * g:gw * (g + 1), :])
        _route_tail(x[rows], y, mod_ref[0], gain_ref, wrt_ref, x1_ref, h2_ref, aff_ref, rows)


def _readout(xa, mod_i, mixer_out, w_out, gain_ffn, w_router, mlstm_norm=None):
    srcs, src_specs, nb, ntot = _tile_sources(xa)
    nt = ntot // TILE
    wo = w_out.astype(BF16)
    wrt = w_router.T
    tail = [wo, gain_ffn, wrt]
    if mlstm_norm is not None:
        hf, hb, og = mixer_out
        kern = _readout_mlstm_kernel
        mid = [hf, hb, og, mlstm_norm]
        mid_in = [_tok_spec(M_V), _tok_spec(M_V), _tok_spec(M_V), _full_spec(mlstm_norm)]
    else:
        kern = _readout_attn_kernel
        mid = [mixer_out]
        mid_in = [pl.BlockSpec((1, A_KV, TILE, A_GRP * A_HD), lambda b, t: (b, 0, t, 0))]
    return pl.pallas_call(
        functools.partial(kern, n_src=len(srcs)),
        out_shape=(jax.ShapeDtypeStruct((nb, ntot, D), F32), jax.ShapeDtypeStruct((nb, ntot, D), BF16),
                   jax.ShapeDtypeStruct((nb, N_EXP, ntot), F32)),
        grid=(nb, nt),
        in_specs=src_specs + [_mod_spec(nb)] + mid_in + [_full_spec(a) for a in tail],
        out_specs=(_tok_spec(D), _tok_spec(D), pl.BlockSpec((1, N_EXP, TILE), lambda b, t: (b, 0, t))),
        compiler_params=_cparams(("arbitrary", "arbitrary")),
        name="readout_route",
    )(*srcs, mod_i, *mid, *tail)


def _slot_geometry(n_ctx, n_lat):
    cap_ctx = max(1, EC_CAPACITY * n_ctx // N_EXP)
    cap_lat = max(1, EC_CAPACITY * n_lat // N_EXP)
    lat_base = -(-cap_ctx // SUB) * SUB
    slots_max = lat_base + cap_lat + (SUB - 1) * (n_lat // TILE)
    rows_ffn = -(-slots_max // 16) * 16
    return cap_ctx, cap_lat, lat_base, rows_ffn


def _topk_kernel(a_ref, slot_ref, slot_t_ref, aff_t_ref, off_ref, cnt_ref, *, n_ctx, cap_ctx, cap_lat, lat_base):
    bits = lax.bitcast_convert_type(a_ref[0], I32)
    prefix = (_iota((LANE, LANE), 0) <= _iota((LANE, LANE), 1)).astype(BF16)

    def count(mask):
        return jnp.sum(mask.astype(F32), axis=1, keepdims=True)

    def select(x, cap):
        thr = jnp.zeros((N_EXP, 1), I32)
        for bit in range(30, -1, -1):
            cand = thr | (1 << bit)
            thr = jnp.where(count(x >= cand) >= cap, cand, thr)
        gt = x > thr
        eq = x == thr
        need = cap - count(gt)
        run = jnp.zeros((N_EXP, 1), F32)
        blocks = []
        for j in range(x.shape[1] // LANE):
            sl = slice(LANE * j, LANE * (j + 1))
            eqf = eq[:, sl].astype(F32)
            inc = _dot(eqf.astype(BF16), prefix)
            rank = run + inc - eqf
            blocks.append(jnp.logical_or(gt[:, sl], jnp.logical_and(eq[:, sl], rank < need)))
            run = run + inc[:, LANE - 1:LANE]
        return blocks

    blocks = select(bits[:, :n_ctx], cap_ctx) + select(bits[:, n_ctx:], cap_lat)
    per_tile = TILE // LANE
    lane = _iota((N_EXP, LANE), 1)
    off_acc = jnp.zeros((N_EXP, LANE), I32)
    cnt_acc = jnp.zeros((N_EXP, LANE), I32)
    base = jnp.zeros((N_EXP, 1), F32)
    for t in range(len(blocks) // per_tile):
        if t == n_ctx // TILE:
            base = jnp.full((N_EXP, 1), float(lat_base), F32)
        run = jnp.zeros((N_EXP, 1), F32)
        for j in range(per_tile):
            blk = blocks[per_tile * t + j]
            sf = blk.astype(F32)
            inc = _dot(sf.astype(BF16), prefix)
            pos = base + run + inc - sf
            c0 = LANE * (per_tile * t + j)
            slot_blk = jnp.concatenate([jnp.where(blk, pos.astype(I32), -1),
                                        jnp.full((LANE - N_EXP, LANE), -1, I32)], axis=0)
            slot_ref[0, :, c0:c0 + LANE] = slot_blk
            slot_t_ref[0, c0:c0 + LANE, :] = jnp.transpose(slot_blk)
            aff_blk = jnp.concatenate([a_ref[0, :, c0:c0 + LANE], jnp.zeros((LANE - N_EXP, LANE), F32)], axis=0)
            aff_t_ref[0, c0:c0 + LANE, :] = jnp.transpose(aff_blk)
            run = run + inc[:, LANE - 1:LANE]
        n8 = jnp.floor((run + (SUB - 1)) * (1.0 / SUB)) * SUB
        off_acc = jnp.where(lane == t, base.astype(I32), off_acc)
        cnt_acc = jnp.where(lane == t, n8.astype(I32), cnt_acc)
        base = base + n8
    off_ref[0] = off_acc
    cnt_ref[0] = cnt_acc


def _topk(aff_t, n_ctx):
    nb, _, ntot = aff_t.shape
    cap_ctx, cap_lat, lat_base, _ = _slot_geometry(n_ctx, ntot - n_ctx)
    kern = functools.partial(_topk_kernel, n_ctx=n_ctx, cap_ctx=cap_ctx, cap_lat=cap_lat, lat_base=lat_base)
    return pl.pallas_call(
        kern,
        out_shape=(jax.ShapeDtypeStruct((nb, LANE, ntot), I32), jax.ShapeDtypeStruct((nb, ntot, LANE), I32),
                   jax.ShapeDtypeStruct((nb, ntot, LANE), F32), jax.ShapeDtypeStruct((nb, N_EXP, LANE), I32),
                   jax.ShapeDtypeStruct((nb, N_EXP, LANE), I32)),
        grid=(nb,),
        in_specs=[pl.BlockSpec((1, N_EXP, ntot), lambda b: (b, 0, 0))],
        out_specs=(pl.BlockSpec((1, LANE, ntot), lambda b: (b, 0, 0)),
                   pl.BlockSpec((1, ntot, LANE), lambda b: (b, 0, 0)),
                   pl.BlockSpec((1, ntot, LANE), lambda b: (b, 0, 0)),
                   pl.BlockSpec((1, N_EXP, LANE), lambda b: (b, 0, 0)),
                   pl.BlockSpec((1, N_EXP, LANE), lambda b: (b, 0, 0))),
        compiler_params=_cparams(("arbitrary",)),
        name="ec_topk",
    )(aff_t)


def _strip_index(shape, dim):
    i = _iota(shape, dim)
    e = jnp.floor((i.astype(F32) + 0.5) * (1.0 / WIN)).astype(I32)
    return e, i - WIN * e


def _dispatch_kernel(off_s, rounds_s, h_ref, slot_ref, aff_ref, xg_ref, stage_ref, sem, cnt_ref):
    b = pl.program_id(0)
    t = pl.program_id(1)
    nt = pl.num_programs(1)
    base = (b * nt + t) * N_EXP
    rows = N_EXP * WIN
    a = aff_ref[0]
    src = _iota((LANE, LANE), 0)
    dst = _iota((LANE, LANE), 1)
    a3 = None
    for p, piece in enumerate(_split3(a)):
        sel = jnp.logical_and(dst == 3 * src + p, src < N_EXP).astype(BF16)
        term = _dot(piece, sel)
        a3 = term if a3 is None else a3 + term
    rhs = jnp.concatenate([h_ref[0], a3.astype(BF16)], axis=1)
    e_of_row, _ = _strip_index((rows, LANE), 0)
    expand = (e_of_row == _iota((rows, LANE), 1)).astype(BF16)
    _, j_row = _strip_index((rows, 1), 0)
    sl = slot_ref[0]
    e_row = _iota((LANE, 1), 0)
    off_v = jnp.zeros((LANE, 1), I32)
    for e in range(N_EXP):
        off_v = jnp.where(e_row == e, off_s[base + e], off_v)

    @pl.when(jnp.logical_and(b == 0, t == 0))
    def _():
        cnt_ref[0] = 0

    def strip_copies(buf, sample, starts):
        return [pltpu.make_async_copy(stage_ref.at[buf, pl.ds(WIN * e, WIN), :],
                                      xg_ref.at[sample, e, pl.ds(starts[e], WIN), :], sem.at[buf])
                for e in range(N_EXP)]

    def wait_strips(buf):
        for cp in strip_copies(buf, 0, [0] * N_EXP):
            cp.wait()

    def round_body(r, carry):
        n = cnt_ref[0]
        buf = n & 1
        rel = jnp.clip(sl - (off_v + WIN * r), -1, WIN).astype(F32).astype(BF16)
        relx = _dot(expand, rel)
        onehot = (relx == j_row.astype(F32)).astype(BF16)
        bits = lax.bitcast_convert_type(_dot(onehot, rhs), I32)
        packed = jnp.bitwise_or(jnp.bitwise_and(bits[:, :HALF] >> 16, 0xFFFF),
                                jnp.bitwise_and(bits[:, HALF:D], HIGH16))
        stage_ref[buf] = jnp.concatenate([packed, bits[:, D:]], axis=1)

        @pl.when(n > 0)
        def _():
            wait_strips(1 - buf)

        last_start = xg_ref.shape[2] - WIN
        starts = [pl.multiple_of(jnp.minimum(off_s[base + e] + WIN * r, last_start), SUB) for e in range(N_EXP)]
        for cp in strip_copies(buf, b, starts):
            cp.start()
        cnt_ref[0] = n + 1
        return carry

    lax.fori_loop(0, rounds_s[b * nt + t], round_body, 0)

    @pl.when(jnp.logical_and(b == pl.num_programs(0) - 1, t == nt - 1))
    def _():
        wait_strips((cnt_ref[0] - 1) & 1)


def _dispatch(h2, slot_pad, aff_pad, off_flat, rounds_flat, rows_alloc):
    nb, ntot, _ = h2.shape
    nt = ntot // TILE
    grid_spec = pltpu.PrefetchScalarGridSpec(
        num_scalar_prefetch=2,
        grid=(nb, nt),
        in_specs=[pl.BlockSpec((1, TILE, D), lambda b, t, o, r: (b, t, 0)),
                  pl.BlockSpec((1, LANE, TILE), lambda b, t, o, r: (b, 0, t)),
                  pl.BlockSpec((1, TILE, LANE), lambda b, t, o, r: (b, t, 0))],
        out_specs=pl.BlockSpec(memory_space=pl.ANY),
        scratch_shapes=[pltpu.VMEM((2, N_EXP * WIN, XP), I32), pltpu.SemaphoreType.DMA((2,)),
                        pltpu.SMEM((1,), I32)],
    )
    return pl.pallas_call(
        _dispatch_kernel,
        out_shape=jax.ShapeDtypeStruct((nb, N_EXP, rows_alloc, XP), I32),
        grid_spec=grid_spec,
        compiler_params=_cparams(("arbitrary", "arbitrary")),
        name="ec_dispatch",
    )(off_flat, rounds_flat, h2, slot_pad, aff_pad)


def _ffn_kernel(used_s, xg_ref, wg32_ref, wu32_ref, wd32_ref, y_ref, wg_ref, wu_ref, wd_ref, *, rows_short):
    e = pl.program_id(0)
    b = pl.program_id(1)

    @pl.when(b == 0)
    def _():
        wg_ref[0] = wg32_ref[0].astype(BF16)
        wu_ref[0] = wu32_ref[0].astype(BF16)
        wd_ref[0] = wd32_ref[0].astype(BF16)

    rows_all = y_ref.shape[2]

    def run(rows):
        words = xg_ref[0, 0, :rows, :HALF]
        x = jnp.concatenate([lax.bitcast_convert_type(words << 16, F32),
                             lax.bitcast_convert_type(jnp.bitwise_and(words, HIGH16), F32)], axis=1).astype(BF16)
        gl = lax.bitcast_convert_type(xg_ref[0, 0, :rows, HALF:], F32)
        lane = _iota(gl.shape, 1)
        mine = jnp.logical_and(lane >= 3 * e, lane < 3 * e + 3)
        gate = jnp.sum(jnp.where(mine, gl, 0.0), axis=1, keepdims=True)
        a = _dot(x, wg_ref[0])
        u = _dot(x, wu_ref[0])
        hm = (a * jax.nn.sigmoid(a) * u).astype(BF16)
        y_ref[0, 0, :rows] = _dot(hm, wd_ref[0]) * gate
        if rows < rows_all:
            y_ref[0, 0, rows:] = jnp.zeros((rows_all - rows, D), F32)

    used = used_s[b * N_EXP + e]

    @pl.when(used <= rows_short)
    def _():
        run(rows_short)

    @pl.when(used > rows_short)
    def _():
        run(rows_all)


def _expert_ffn(xg, wg, wu, wd, used_flat, rows_ffn, layer=0):
    nb = xg.shape[0]
    rows_short = max(16, rows_ffn - 48)
    if wg.ndim == 4:
        wspec = pl.BlockSpec((None, 1, D, D), lambda e, b, u: (layer, e, 0, 0))
    else:
        wspec = pl.BlockSpec((1, D, D), lambda e, b, u: (e, 0, 0))
    grid_spec = pltpu.PrefetchScalarGridSpec(
        num_scalar_prefetch=1,
        grid=(N_EXP, nb),
        in_specs=[pl.BlockSpec((1, 1, rows_ffn, XP), lambda e, b, u: (b, e, 0, 0)), wspec, wspec, wspec],
        out_specs=pl.BlockSpec((1, 1, rows_ffn, D), lambda e, b, u: (b, e, 0, 0)),
        scratch_shapes=[pltpu.VMEM((1, D, D), BF16)] * 3,
    )
    return pl.pallas_call(
        functools.partial(_ffn_kernel, rows_short=rows_short),
        out_shape=jax.ShapeDtypeStruct((nb, N_EXP, rows_ffn, D), F32),
        grid_spec=grid_spec,
        compiler_params=_cparams(("arbitrary", "arbitrary")),
        name="ec_ffn",
    )(used_flat, xg, wg, wu, wd)


def _combine_kernel(off_s, rounds_s, x_ref, mod_ref, slot_ref, fin_ref, y_ref, *rest, final_ctx_tiles, proj_body,
                    n_proj_in):
    if proj_body is None:
        o_ref, strip_ref, sem = rest
    else:
        next_mod_ref = rest[0]
        proj_in = rest[1:1 + n_proj_in]
        o_ref = rest[1 + n_proj_in]
        proj_out = rest[2 + n_proj_in:-2]
        strip_ref, sem = rest[-2:]
    b = pl.program_id(0)
    t = pl.program_id(1)
    nt = pl.num_programs(1)
    base = (b * nt + t) * N_EXP
    cols = N_EXP * WIN
    e_of_col, _ = _strip_index((LANE, cols), 1)
    expand = (e_of_col == _iota((LANE, cols), 0)).astype(BF16)
    _, j_lane = _strip_index((1, cols), 1)
    j_lane = j_lane.astype(F32)
    sl = slot_ref[0]
    e_lane = _iota((1, LANE), 1)
    last_start = y_ref.shape[2] - WIN

    def strip_start(step_, e, r):
        return jnp.minimum(off_s[step_ * N_EXP + e] + WIN * r, last_start)

    step = b * nt + t
    buf = step & 1

    def strip_copies(step_, sample, r, dst):
        return [pltpu.make_async_copy(
            y_ref.at[sample, e, pl.ds(pl.multiple_of(strip_start(step_, e, r), SUB), WIN), :],
            strip_ref.at[dst, pl.ds(WIN * e, WIN), :], sem.at[dst]) for e in range(N_EXP)]

    @pl.when(step == 0)
    def _():
        for cp in strip_copies(step, b, 0, buf):
            cp.start()

    @pl.when(step + 1 < pl.num_programs(0) * nt)
    def _():
        for cp in strip_copies(step + 1, jnp.where(t == nt - 1, b + 1, b), 0, 1 - buf):
            cp.start()

    def expand_round(r, acc):
        first_v = jnp.zeros((1, LANE), I32)
        start_v = jnp.zeros((1, LANE), I32)
        for e in range(N_EXP):
            first_v = jnp.where(e_lane == e, off_s[base + e] + WIN * r, first_v)
            start_v = jnp.where(e_lane == e, strip_start(step, e, r), start_v)
        nominal = sl - first_v
        in_round = jnp.logical_and(nominal >= 0, nominal < WIN)
        rel = jnp.where(in_round, sl - start_v, -1).astype(F32).astype(BF16)
        relx = _dot(rel, expand)
        onehot = (relx == j_lane).astype(BF16)
        for cp in strip_copies(step, b, r, buf):
            cp.wait()
        return acc + _dot(onehot, strip_ref[buf].astype(BF16))

    def extra_round(r, acc):
        for cp in strip_copies(step, b, r, buf):
            cp.start()
        return expand_round(r, acc)

    acc = expand_round(0, jnp.zeros((TILE, D), F32))
    acc = lax.fori_loop(1, rounds_s[step], extra_round, acc)
    x2 = x_ref[0] + mod_ref[0][5:6] * acc
    if final_ctx_tiles is None:
        o_ref[0] = x2
        if proj_body is not None:
            proj_body(x2, next_mod_ref[0], *proj_in, *proj_out)
    else:
        @pl.when(t >= final_ctx_tiles)
        def _():
            ms = jnp.mean(x2 * x2, axis=-1, keepdims=True)
            o_ref[0] = x2 * lax.rsqrt(ms + EPS) * fin_ref[...]


def _combine(x1, mod_i, slot_t_pad, y, off_flat, rounds_flat, final_gain, final_ctx_tiles=None, next_proj=None):
    nb, ntot, _ = x1.shape
    nt = ntot // TILE
    skip = 0 if final_ctx_tiles is None else final_ctx_tiles
    in_specs = [_tok_spec(D), _mod_spec(nb), _tok_spec(LANE), pl.BlockSpec((1, D), lambda b, t, *_: (0, 0)),
                pl.BlockSpec(memory_space=pl.ANY)]
    args = [x1, mod_i, slot_t_pad, final_gain, y]
    out_shape = [jax.ShapeDtypeStruct((nb, ntot - skip * TILE, D), F32)]
    out_specs = [pl.BlockSpec((1, TILE, D), lambda b, t, *_: (b, jnp.maximum(t - skip, 0), 0))]
    proj_body, n_proj_in = None, 0
    if next_proj is not None:
        proj_body, next_mod, (arrays, specs, p_shape, p_specs) = next_proj
        in_specs += [_mod_spec(nb)] + specs
        args += [next_mod] + arrays
        out_shape += p_shape
        out_specs += p_specs
        n_proj_in = len(arrays)
    grid_spec = pltpu.PrefetchScalarGridSpec(
        num_scalar_prefetch=2,
        grid=(nb, nt),
        in_specs=in_specs,
        out_specs=tuple(out_specs),
        scratch_shapes=[pltpu.VMEM((2, N_EXP * WIN, D), F32), pltpu.SemaphoreType.DMA((2,))],
    )
    outs = pl.pallas_call(
        functools.partial(_combine_kernel, final_ctx_tiles=final_ctx_tiles, proj_body=proj_body,
                          n_proj_in=n_proj_in),
        out_shape=tuple(out_shape),
        grid_spec=grid_spec,
        compiler_params=_cparams(("arbitrary", "arbitrary")),
        name="ec_combine",
    )(off_flat, rounds_flat, *args)
    return outs[0], tuple(outs[1:])


def _moe(x1, h2, aff_t, mod_i, wg, wu, wd, n_ctx, final_gain, last, layer=0, next_proj=None):
    nb, ntot, _ = x1.shape
    nt = ntot // TILE
    _, _, _, rows_ffn = _slot_geometry(n_ctx, ntot - n_ctx)
    rows_alloc = rows_ffn + WIN
    slot_pad, slot_t_pad, aff_pad, off, cnt = _topk(aff_t, n_ctx)
    off_t = jnp.swapaxes(off[:, :, :nt], 1, 2)
    cnt_t = jnp.swapaxes(cnt[:, :, :nt], 1, 2)
    rounds_c = jnp.maximum(1, jnp.max((cnt_t + WIN - 1) // WIN, axis=2))
    fill = jnp.max((rows_alloc - off_t[:, nt - 1, :] + WIN - 1) // WIN, axis=1)
    rounds_d = rounds_c.at[:, nt - 1].max(fill)
    off_flat = off_t.reshape(-1)
    xg = _dispatch(h2, slot_pad, aff_pad, off_flat, rounds_d.reshape(-1), rows_alloc)
    used = (off_t[:, nt - 1, :] + cnt_t[:, nt - 1, :]).reshape(-1)
    y = _expert_ffn(xg, wg, wu, wd, used, rows_ffn, layer)
    return _combine(x1, mod_i, slot_t_pad, y, off_flat, rounds_c.reshape(-1), final_gain,
                    final_ctx_tiles=n_ctx // TILE if last else None, next_proj=next_proj)


def kernel(x, c, ctx, c_ctx, w_mod, b_mod, norm_mix, norm_ffn, mlstm_w_in, mlstm_b_gate, mlstm_norm, mlstm_w_out,
           attn_w_in, attn_q_norm, attn_k_norm, attn_w_out, moe_router, moe_w_gate, moe_w_up, moe_w_down,
           norm_final):
    nb, n_lat, _ = x.shape
    n_ctx = ctx.shape[1]
    depth = w_mod.shape[0]
    assert n_ctx == TILE and n_lat % TILE == 0 and x.shape[2] == D
    xa = (ctx, x)
    rb = -(-(nb + 1) // SUB) * SUB
    cc = jnp.concatenate([c, c_ctx[None, :], jnp.zeros((rb - nb - 1, D), F32)], axis=0)
    mod = _modulation(cc, w_mod, b_mod)
    cos, sin = _rope_tables(n_lat, n_ctx)
    ntot = n_ctx + n_lat

    def proj_operands(i):
        j = i // 2
        gain_mix = norm_mix[i].reshape(1, D)
        if i % 2 == 0:
            return _proj_mlstm_body, _proj_mlstm_operands(nb, ntot, gain_mix, mlstm_w_in[j], mlstm_b_gate[j]), None
        ops, flag = _proj_attn_operands(nb, ntot, gain_mix, attn_w_in[j], attn_q_norm[j], attn_k_norm[j], cos, sin)
        return _proj_attn_body, ops, flag

    _, ops, flag = proj_operands(0)
    proj = _proj_mlstm(xa, mod[0], ops)
    for i in range(depth):
        j = i // 2
        mod_i = mod[i]
        gain_ffn = norm_ffn[i].reshape(1, D)
        if i % 2 == 0:
            q, kt, v, og, gc, gr = proj
            hf, hb = _mlstm_scan(q, kt, v, gc, gr)
            x1, h2, aff_t = _readout(xa, mod_i, (hf, hb, og), mlstm_w_out[j], gain_ffn, moe_router[i],
                                     mlstm_norm=mlstm_norm[j].reshape(1, M_V))
        else:
            q, kt, vd = proj
            oa = _attention(q, kt, vd, flag)
            x1, h2, aff_t = _readout(xa, mod_i, oa, attn_w_out[j], gain_ffn, moe_router[i])
        last = i == depth - 1
        next_proj = None
        if not last:
            body, ops, flag = proj_operands(i + 1)
            next_proj = (body, mod[i + 1], ops)
        xa, proj = _moe(x1, h2, aff_t, mod_i, moe_w_gate, moe_w_up, moe_w_down, n_ctx,
                        norm_final.reshape(1, D), last=last, layer=i, next_proj=next_proj)
    return xa
```

```python
import functools

import jax
import jax.numpy as jnp
from jax import lax
from jax.experimental import pallas as pl
from jax.experimental.pallas import tpu as pltpu

F32 = jnp.float32
BF16 = jnp.bfloat16
I32 = jnp.int32

D = 1024
TILE = 256
HALVES = (slice(0, TILE // 2), slice(TILE // 2, TILE))
EPS = 1e-6
DEPTH = 4

M_HEADS = 4
M_DK = 128
M_DV = 256
M_QK = M_HEADS * M_DK
M_V = M_HEADS * M_DV
M_AUG = M_DV + 128
GATE_CAP = 15.0

A_HEADS = 16
A_KV = 4
A_GRP = 4
A_HD = 64
ROPE_THETA = 10000.0
GRID_W = 64
LOG2E = 1.4426950408889634

N_EXP = 16
EC_CAPACITY = 2
WIN = 48
GATE_LANES = 128
XW = D + GATE_LANES
HALF = D // 2
XP = HALF + GATE_LANES
HIGH16 = -65536

LANE = 128
SUB = 8
VMEM_LIMIT = 56 * 1024 * 1024


def _cparams(sem):
    return pltpu.CompilerParams(dimension_semantics=sem, vmem_limit_bytes=VMEM_LIMIT)


def _dot(a, b):
    return jnp.dot(a, b, preferred_element_type=F32)


def _dot_nt(a, b):
    return lax.dot_general(a, b, (((1,), (1,)), ((), ())), preferred_element_type=F32)


def _split2(x):
    hi = x.astype(BF16)
    lo = (x - hi.astype(F32)).astype(BF16)
    return hi, lo


def _split3(x):
    hi = x.astype(BF16)
    r = x - hi.astype(F32)
    mid = r.astype(BF16)
    lo = (r - mid.astype(F32)).astype(BF16)
    return hi, mid, lo


def _dot3(a, b):
    ah, al = _split2(a)
    bh, bl = _split2(b)
    return _dot(ah, bh) + (_dot(ah, bl) + _dot(al, bh))


def _rms_mod(x, gain, shift, scale):
    ms = jnp.mean(x * x, axis=-1, keepdims=True)
    y = x * lax.rsqrt(ms + EPS) * gain
    return y * (1.0 + scale) + shift


def _iota(shape, dim):
    return lax.broadcasted_iota(I32, shape, dim)


def _mod_kernel(c_ref, w_ref, b_ref, o_ref):
    c = c_ref[...]
    s = c * jax.nn.sigmoid(c)
    o_ref[...] = _dot3(s, w_ref[...]) + b_ref[...]


def _modulation(cc, w_mod, b_mod):
    depth, _, n6 = w_mod.shape
    rb = cc.shape[0]
    nj = n6 // D
    out = pl.pallas_call(
        _mod_kernel,
        out_shape=jax.ShapeDtypeStruct((depth, rb, n6), F32),
        grid=(depth, nj),
        in_specs=[
            pl.BlockSpec((rb, D), lambda i, j: (0, 0)),
            pl.BlockSpec((None, D, D), lambda i, j: (i, 0, j)),
            pl.BlockSpec((None, 1, D), lambda i, j: (i, 0, j)),
        ],
        out_specs=pl.BlockSpec((None, rb, D), lambda i, j: (i, 0, j)),
        compiler_params=_cparams(("arbitrary", "arbitrary")),
        name="adaln_mod",
    )(cc, w_mod, b_mod.reshape(depth, 1, n6))
    return out.reshape(depth, rb, nj, D)


def _gate_act(g, idx):
    g = GATE_CAP * jnp.tanh(g * (1.0 / GATE_CAP))
    logsig = jnp.minimum(g, 0.0) - jnp.log(1.0 + jnp.exp(-jnp.abs(g)))
    is_forget = ((idx >> 2) & 1) == 1
    return jnp.where(is_forget, logsig, g)


def _proj_mlstm_body(x, m, gain_ref, w_ref, wkt_ref, wg_ref, wgt_ref, bc_ref, br_ref,
                     q_ref, kt_ref, v_ref, o_ref, gc_ref, gr_ref):
    h = _rms_mod(x, gain_ref[...], m[0:1], m[1:2])
    hb = h.astype(BF16)
    r = _dot(hb, w_ref[...])
    q_ref[0] = (r[:, :M_QK] * (M_DK ** -0.5)).astype(BF16)
    v_ref[0] = r[:, M_QK:M_QK + M_V].astype(BF16)
    o_ref[0] = r[:, M_QK + M_V:].astype(BF16)
    kt_ref[0] = _dot_nt(wkt_ref[...], hb).astype(BF16)
    gc = _dot(hb, wg_ref[...]) + bc_ref[...]
    gc_ref[0] = _gate_act(gc, _iota(gc.shape, 1))
    gr = _dot_nt(wgt_ref[...], hb) + br_ref[...]
    gr_ref[0] = _gate_act(gr, _iota(gr.shape, 0))


def _proj_mlstm_kernel(*refs, n_src):
    _proj_mlstm_body(_tile_value(refs[:n_src]), refs[n_src][0], *refs[n_src + 1:])


def _tile_sources(xa):
    if isinstance(xa, tuple):
        ctx, lat = xa
        assert ctx.shape[1] == TILE
        specs = [pl.BlockSpec((1, TILE, D), lambda b, t, *_: (b, 0, 0)),
                 pl.BlockSpec((1, TILE, D), lambda b, t, *_: (b, jnp.maximum(t - 1, 0), 0))]
        return [ctx, lat], specs, lat.shape[0], ctx.shape[1] + lat.shape[1]
    return [xa], [_tok_spec(D)], xa.shape[0], xa.shape[1]


def _tile_value(src_refs):
    if len(src_refs) == 1:
        return src_refs[0][0]
    return jnp.where(pl.program_id(1) == 0, src_refs[0][0], src_refs[1][0])


def _tok_spec(width):
    return pl.BlockSpec((1, TILE, width), lambda b, t, *_: (b, t, 0))


def _full_spec(a):
    return pl.BlockSpec(a.shape, lambda b, t, *_: (0,) * a.ndim)


def _mod_spec(nb):
    return pl.BlockSpec((1, 6, D), lambda b, t, *_: (jnp.where(t == 0, nb, b), 0, 0))


def _proj_mlstm_operands(nb, ntot, gain, w_in, b_gate):
    n_g = 4 * M_HEADS
    wq = w_in[:, :M_QK]
    wk = w_in[:, M_QK:2 * M_QK]
    wvo = w_in[:, 2 * M_QK:2 * M_QK + 2 * M_V]
    wg = w_in[:, 2 * M_QK + 2 * M_V:]
    w_main = jnp.concatenate([wq, wvo], axis=1).astype(BF16)
    wkt = wk.T.astype(BF16)
    wg_pad = jnp.pad(wg, ((0, 0), (0, LANE - n_g))).astype(BF16)
    wgt = wg.T.astype(BF16)
    bc = jnp.pad(b_gate, (0, LANE - n_g)).reshape(1, LANE)
    br = b_gate.reshape(n_g, 1)
    arrays = [gain, w_main, wkt, wg_pad, wgt, bc, br]
    out_shape = [
        jax.ShapeDtypeStruct((nb, ntot, M_QK), BF16),
        jax.ShapeDtypeStruct((nb, M_QK, ntot), BF16),
        jax.ShapeDtypeStruct((nb, ntot, M_V), BF16),
        jax.ShapeDtypeStruct((nb, ntot, M_V), BF16),
        jax.ShapeDtypeStruct((nb, ntot, LANE), F32),
        jax.ShapeDtypeStruct((nb, n_g, ntot), F32),
    ]
    out_specs = [_tok_spec(M_QK), pl.BlockSpec((1, M_QK, TILE), lambda b, t, *_: (b, 0, t)), _tok_spec(M_V),
                 _tok_spec(M_V), _tok_spec(LANE), pl.BlockSpec((1, n_g, TILE), lambda b, t, *_: (b, 0, t))]
    return arrays, [_full_spec(a) for a in arrays], out_shape, out_specs


def _proj_mlstm(xa, mod_i, operands):
    srcs, src_specs, nb, ntot = _tile_sources(xa)
    arrays, in_specs, out_shape, out_specs = operands
    return pl.pallas_call(
        functools.partial(_proj_mlstm_kernel, n_src=len(srcs)),
        out_shape=tuple(out_shape),
        grid=(nb, ntot // TILE),
        in_specs=src_specs + [_mod_spec(nb)] + in_specs,
        out_specs=tuple(out_specs),
        compiler_params=_cparams(("arbitrary", "arbitrary")),
        name="proj_mlstm",
    )(*srcs, mod_i, *arrays)


def _mlstm_kernel(qf_ref, qb_ref, kf_ref, kb_ref, vf_ref, vb_ref, gcf_ref, gcb_ref, grf_ref, grb_ref,
                  hf_ref, hb_ref, c_ref, m_ref):
    t = pl.program_id(1)

    @pl.when(t == 0)
    def _():
        c_ref[...] = jnp.zeros(c_ref.shape, F32)
        m_ref[...] = jnp.zeros(m_ref.shape, F32)

    n = TILE
    row = _iota((n, n), 0)
    col = _iota((n, n), 1)
    lower = col <= row
    upper = col >= row
    lower_b = lower.astype(BF16)
    upper_b = upper.astype(BF16)
    ones_col = (_iota((n, M_AUG - M_DV), 1) == 0).astype(BF16)
    dirs = ((qf_ref, kf_ref, vf_ref, gcf_ref, grf_ref, hf_ref, lower, lower_b, upper_b, n - 1),
            (qb_ref, kb_ref, vb_ref, gcb_ref, grb_ref, hb_ref, upper, upper_b, lower_b, 0))
    c_old = [c_ref[i] for i in range(2 * M_HEADS)]
    m_old = [m_ref[i][0:1, 0:1] for i in range(2 * M_HEADS)]
    c_new, m_new_all, h_out = {}, {}, {0: [], 1: []}
    for d, (q_ref, k_ref, v_ref, gc_ref, gr_ref, o_ref, mask, cum_l, cum_r, last) in enumerate(dirs):
        gc = gc_ref[0]
        gr = gr_ref[0]
        bcol = sum(_dot(cum_l, p) for p in _split3(gc))
        brow = sum(_dot(p, cum_r) for p in _split3(gr))
        for h in range(M_HEADS):
            gi = 8 * d + h
            gf = gi + 4
            sidx = 4 * d + h
            b_col = bcol[:, gf:gf + 1]
            b_row = brow[gf:gf + 1, :]
            ig_row = gr[gi:gi + 1, :]
            total = b_row[:, last:last + 1]
            m_st = m_old[sidx]
            key_row = ig_row - b_row
            log_d = jnp.where(mask, b_col + key_row, -jnp.inf)
            m_inter = b_col + m_st
            m_q = jnp.maximum(m_inter, jnp.max(log_d, axis=1, keepdims=True))
            w_intra = jnp.exp(log_d - m_q)
            w_inter = jnp.exp(m_inter - m_q)
            qh = q_ref[0, :, M_DK * h:M_DK * (h + 1)]
            kth = k_ref[0, M_DK * h:M_DK * (h + 1), :]
            v_aug = jnp.concatenate([v_ref[0, :, M_DV * h:M_DV * (h + 1)], ones_col], axis=1)
            c_aug = c_old[sidx]
            s = (_dot(qh, kth) * w_intra).astype(BF16)
            nd = _dot(s, v_aug) + w_inter * _dot(qh, c_aug.astype(BF16))
            den = nd[:, M_DV:M_DV + 1]
            inv = 1.0 / jnp.maximum(jnp.abs(den), jnp.exp(-m_q))
            h_out[d].append((nd[:, :M_DV] * inv).astype(o_ref.dtype))
            log_w = total + key_row
            m_new = jnp.maximum(total + m_st, jnp.max(log_w, axis=1, keepdims=True))
            w_key = jnp.exp(log_w - m_new)
            decay = jnp.exp(total + m_st - m_new)
            kw = (kth.astype(F32) * w_key).astype(BF16)
            c_new[sidx] = decay * c_aug + _dot(kw, v_aug)
            m_new_all[sidx] = m_new
    hf_ref[0] = jnp.concatenate(h_out[0], axis=1)
    hb_ref[0] = jnp.concatenate(h_out[1], axis=1)
    for i in range(2 * M_HEADS):
        c_ref[i] = c_new[i]
        m_ref[i] = jnp.broadcast_to(m_new_all[i], m_ref.shape[1:])


def _mlstm_scan(q, kt, v, gc, gr):
    nb, ntot, _ = q.shape
    nt = ntot // TILE
    fwd = lambda b, t: (b, t, 0)
    bwd = lambda b, t: (b, jnp.where(t == 0, 0, nt - t), 0)
    fwd_t = lambda b, t: (b, 0, t)
    bwd_t = lambda b, t: (b, 0, jnp.where(t == 0, 0, nt - t))
    n_g = gr.shape[1]
    return pl.pallas_call(
        _mlstm_kernel,
        out_shape=(jax.ShapeDtypeStruct((nb, ntot, M_V), BF16), jax.ShapeDtypeStruct((nb, ntot, M_V), BF16)),
        grid=(nb, nt),
        in_specs=[
            pl.BlockSpec((1, TILE, M_QK), fwd), pl.BlockSpec((1, TILE, M_QK), bwd),
            pl.BlockSpec((1, M_QK, TILE), fwd_t), pl.BlockSpec((1, M_QK, TILE), bwd_t),
            pl.BlockSpec((1, TILE, M_V), fwd), pl.BlockSpec((1, TILE, M_V), bwd),
            pl.BlockSpec((1, TILE, LANE), fwd), pl.BlockSpec((1, TILE, LANE), bwd),
            pl.BlockSpec((1, n_g, TILE), fwd_t), pl.BlockSpec((1, n_g, TILE), bwd_t),
        ],
        out_specs=(pl.BlockSpec((1, TILE, M_V), fwd), pl.BlockSpec((1, TILE, M_V), bwd)),
        scratch_shapes=[pltpu.VMEM((2 * M_HEADS, M_DK, M_AUG), F32), pltpu.VMEM((2 * M_HEADS, SUB, LANE), F32)],
        compiler_params=_cparams(("arbitrary", "arbitrary")),
        name="mlstm_scan",
    )(q, q, kt, kt, v, v, gc, gc, gr, gr)


def _head_norm(x, gain):
    w = x.shape[1]
    gsum = ((_iota((w, LANE), 0) >> 6) == _iota((w, LANE), 1)).astype(BF16)
    gexp = ((_iota((LANE, w), 1) >> 6) == _iota((LANE, w), 0)).astype(BF16)
    hi, lo = _split2(x * x)
    ssum = _dot(hi, gsum) + _dot(lo, gsum)
    rh, rl = _split2(lax.rsqrt(ssum * (1.0 / A_HD) + EPS))
    return x * (_dot(rh, gexp) + _dot(rl, gexp)) * gain


def _rope(x, cos, sin_signed):
    first = (_iota((x.shape[0], LANE), 1) & (A_HD - 1)) < A_HD // 2
    tiles = []
    for i in range(x.shape[1] // LANE):
        xt = x[:, LANE * i:LANE * (i + 1)]
        partner = jnp.where(first, pltpu.roll(xt, LANE - A_HD // 2, 1), pltpu.roll(xt, A_HD // 2, 1))
        tiles.append(xt * cos + partner * sin_signed)
    return jnp.concatenate(tiles, axis=1)


def _proj_attn_body(x, m, gain_ref, w_ref, qg_ref, kg_ref, cos_ref, sin_ref, shift_ref, q_ref, kt_ref, vd_ref):
    h = _rms_mod(x, gain_ref[...], m[0:1], m[1:2])
    r = _dot(h.astype(BF16), w_ref[...])
    nq = A_HEADS * A_HD
    nk = A_KV * A_HD
    cos = cos_ref[...]
    sin = sin_ref[...]
    qn = _rope(_head_norm(r[:, :nq], qg_ref[...]), cos, sin)
    kn = _rope(_head_norm(r[:, nq:nq + nk], kg_ref[...]), cos, sin)
    v = r[:, nq + nk:]
    lane = _iota((TILE, LANE), 1)
    low = lane < A_HD
    one_hot = jnp.where(lane == A_HD, 1.0, 0.0)

    def head_tile(a, idx):
        tile = a[:, LANE * (idx // 2):LANE * (idx // 2 + 1)]
        if idx % 2 == 1:
            tile = pltpu.roll(tile, A_HD, 1)
        return jnp.where(low, tile, one_hot).astype(BF16)

    for hd in range(A_HEADS):
        q_ref[0, hd] = head_tile(qn, hd)
    for g in range(A_KV):
        vd_ref[0, g] = head_tile(v, g)
    kt = jnp.transpose(kn)
    extra = jnp.where(_iota((LANE - A_HD, TILE), 0) == 0, shift_ref[...], 0.0).astype(BF16)
    for g in range(A_KV):
        kt_ref[0, g, 0:A_HD, :] = kt[A_HD * g:A_HD * (g + 1), :].astype(BF16)
        kt_ref[0, g, A_HD:LANE, :] = extra


def _rope_tables(n_lat, n_ctx):
    rows = n_lat // GRID_W
    row = jnp.repeat(jnp.arange(rows, dtype=F32), GRID_W)
    col = jnp.tile(jnp.arange(GRID_W, dtype=F32), rows)
    pairs = A_HD // 4
    inv = ROPE_THETA ** (-jnp.arange(pairs, dtype=F32) / pairs)
    ang = jnp.concatenate([row[:, None] * inv, col[:, None] * inv], axis=-1)
    c = jnp.cos(ang)
    s = jnp.sin(ang)
    cos = jnp.concatenate([c, c, c, c], axis=-1)
    sin = jnp.concatenate([-s, s, -s, s], axis=-1)
    cos = jnp.concatenate([jnp.ones((n_ctx, LANE), F32), cos], axis=0)
    sin = jnp.concatenate([jnp.zeros((n_ctx, LANE), F32), sin], axis=0)
    return cos, sin


SHIFT_LIMIT = 60.0


def _softmax_shift(qg, kg):
    bound = A_HD * jnp.max(jnp.abs(qg)) * jnp.max(jnp.abs(kg))
    fast = bound <= SHIFT_LIMIT
    shift = jnp.where(fast, jnp.ceil(bound), 0.0)
    return shift, jnp.logical_not(fast).astype(I32)


def _proj_attn_operands(nb, ntot, gain, w_in, q_norm, k_norm, cos, sin):
    nq = A_HEADS * A_HD
    nk = A_KV * A_HD
    w = w_in.astype(BF16)
    qg = (jnp.tile(q_norm, A_HEADS) * (A_HD ** -0.5 * LOG2E)).reshape(1, nq)
    kg = jnp.tile(k_norm, A_KV).reshape(1, nk)
    shift, flag = _softmax_shift(qg, kg)
    neg_shift = (-shift).reshape(1, 1).astype(F32)
    tab = pl.BlockSpec((TILE, LANE), lambda b, t, *_: (t, 0))
    arrays = [gain, w, qg, kg, cos, sin, neg_shift]
    in_specs = [_full_spec(gain), _full_spec(w), _full_spec(qg), _full_spec(kg), tab, tab, _full_spec(neg_shift)]
    out_shape = [
        jax.ShapeDtypeStruct((nb, A_HEADS, ntot, LANE), BF16),
        jax.ShapeDtypeStruct((nb, A_KV, LANE, ntot), BF16),
        jax.ShapeDtypeStruct((nb, A_KV, ntot, LANE), BF16),
    ]
    out_specs = [pl.BlockSpec((1, A_HEADS, TILE, LANE), lambda b, t, *_: (b, 0, t, 0)),
                 pl.BlockSpec((1, A_KV, LANE, TILE), lambda b, t, *_: (b, 0, 0, t)),
                 pl.BlockSpec((1, A_KV, TILE, LANE), lambda b, t, *_: (b, 0, t, 0))]
    return (arrays, in_specs, out_shape, out_specs), flag.reshape(1)


def _attn_kernel(flag_ref, q_ref, kt_ref, vd_ref, o_ref, *, ntot):
    low = _iota((TILE, LANE), 1) < A_HD

    def attend(row0, nk, row_max):
        outs = []
        for hd in range(A_GRP):
            s = _dot(q_ref[0, hd, pl.ds(row0, TILE), :], kt_ref[0, 0, :, :nk])
            if row_max:
                s = s - jnp.max(s, axis=1, keepdims=True)
            r = _dot(jnp.exp2(s).astype(BF16), vd_ref[0, 0, :nk, :])
            outs.append(r / r[:, A_HD:A_HD + 1])
        t0 = jnp.where(low, outs[0], pltpu.roll(outs[1], A_HD, 1))
        t1 = jnp.where(low, outs[2], pltpu.roll(outs[3], A_HD, 1))
        o_ref[0, 0, pl.ds(row0, TILE), :] = jnp.concatenate([t0, t1], axis=1).astype(BF16)

    attend(0, TILE, True)

    def latent_tiles(row_max):
        def body(i, carry):
            attend(pl.multiple_of(i * TILE, TILE), ntot, row_max)
            return carry
        lax.fori_loop(1, ntot // TILE, body, 0)

    @pl.when(flag_ref[0] == 0)
    def _():
        latent_tiles(False)

    @pl.when(flag_ref[0] != 0)
    def _():
        latent_tiles(True)


def _attention(q, kt, vd, flag):
    nb, _, ntot, _ = q.shape
    grid_spec = pltpu.PrefetchScalarGridSpec(
        num_scalar_prefetch=1,
        grid=(nb, A_KV),
        in_specs=[pl.BlockSpec((1, A_GRP, ntot, LANE), lambda b, g, f: (b, g, 0, 0)),
                  pl.BlockSpec((1, 1, LANE, ntot), lambda b, g, f: (b, g, 0, 0)),
                  pl.BlockSpec((1, 1, ntot, LANE), lambda b, g, f: (b, g, 0, 0))],
        out_specs=pl.BlockSpec((1, 1, ntot, A_GRP * A_HD), lambda b, g, f: (b, g, 0, 0)),
    )
    return pl.pallas_call(
        functools.partial(_attn_kernel, ntot=ntot),
        out_shape=jax.ShapeDtypeStruct((nb, A_KV, ntot, A_GRP * A_HD), BF16),
        grid_spec=grid_spec,
        compiler_params=_cparams(("arbitrary", "arbitrary")),
        name="attention",
    )(flag, q, kt, vd)


def _route_tail(x, y, m, gain_ref, wrt_ref, x1_ref, h2_ref, aff_ref, rows):
    x1 = x + m[2:3] * y
    x1_ref[0, rows] = x1
    h2 = _rms_mod(x1, gain_ref[...], m[3:4], m[4:5])
    h2_ref[0, rows] = h2.astype(BF16)
    logits = _dot3_nt(wrt_ref[...], h2)
    e = jnp.exp(logits - jnp.max(logits, axis=0, keepdims=True))
    aff_ref[0, :, rows] = e / jnp.sum(e, axis=0, keepdims=True)


def _dot3_nt(a, b):
    ah, al = _split2(a)
    bh, bl = _split2(b)
    return _dot_nt(ah, bh) + (_dot_nt(ah, bl) + _dot_nt(al, bh))


def _readout_mlstm_kernel(*refs, n_src):
    x = _tile_value(refs[:n_src])
    mod_ref, hf_ref, hb_ref, o_ref, mn_ref, wo_ref, gain_ref, wrt_ref, x1_ref, h2_ref, aff_ref = refs[n_src:]
    mn = mn_ref[...]
    for rows in HALVES:
        hh = hf_ref[0, rows].astype(F32) + hb_ref[0, rows].astype(F32)
        parts = []
        for h in range(M_HEADS):
            seg = hh[:, M_DV * h:M_DV * (h + 1)]
            ms = jnp.mean(seg * seg, axis=-1, keepdims=True)
            parts.append(seg * lax.rsqrt(ms + EPS) * mn[:, M_DV * h:M_DV * (h + 1)])
        z = jnp.concatenate(parts, axis=1) * jax.nn.sigmoid(o_ref[0, rows].astype(F32))
        y = _dot(z.astype(BF16), wo_ref[...])
        _route_tail(x[rows], y, mod_ref[0], gain_ref, wrt_ref, x1_ref, h2_ref, aff_ref, rows)


def _readout_attn_kernel(*refs, n_src):
    x = _tile_value(refs[:n_src])
    mod_ref, oa_ref, wo_ref, gain_ref, wrt_ref, x1_ref, h2_ref, aff_ref = refs[n_src:]
    gw = A_GRP * A_HD
    for rows in HALVES:
        y = _dot(oa_ref[0, 0, rows], wo_ref[0:gw, :])
        for g in range(1, A_KV):
            y = y + _dot(oa_ref[0, g, rows], wo_ref[gw * g:gw * (g + 1), :])
        _route_tail(x[rows], y, mod_ref[0], gain_ref, wrt_ref, x1_ref, h2_ref, aff_ref, rows)


def _readout(xa, mod_i, mixer_out, w_out, gain_ffn, w_router, mlstm_norm=None):
    srcs, src_specs, nb, ntot = _tile_sources(xa)
    nt = ntot // TILE
    wo = w_out.astype(BF16)
    wrt = w_router.T
    tail = [wo, gain_ffn, wrt]
    if mlstm_norm is not None:
        hf, hb, og = mixer_out
        kern = _readout_mlstm_kernel
        mid = [hf, hb, og, mlstm_norm]
        mid_in = [_tok_spec(M_V), _tok_spec(M_V), _tok_spec(M_V), _full_spec(mlstm_norm)]
    else:
        kern = _readout_attn_kernel
        mid = [mixer_out]
        mid_in = [pl.BlockSpec((1, A_KV, TILE, A_GRP * A_HD), lambda b, t: (b, 0, t, 0))]
    return pl.pallas_call(
        functools.partial(kern, n_src=len(srcs)),
        out_shape=(jax.ShapeDtypeStruct((nb, ntot, D), F32), jax.ShapeDtypeStruct((nb, ntot, D), BF16),
                   jax.ShapeDtypeStruct((nb, N_EXP, ntot), F32)),
        grid=(nb, nt),
        in_specs=src_specs + [_mod_spec(nb)] + mid_in + [_full_spec(a) for a in tail],
        out_specs=(_tok_spec(D), _tok_spec(D), pl.BlockSpec((1, N_EXP, TILE), lambda b, t: (b, 0, t))),
        compiler_params=_cparams(("arbitrary", "arbitrary")),
        name="readout_route",
    )(*srcs, mod_i, *mid, *tail)


def _slot_geometry(n_ctx, n_lat):
    cap_ctx = max(1, EC_CAPACITY * n_ctx // N_EXP)
    cap_lat = max(1, EC_CAPACITY * n_lat // N_EXP)
    lat_base = -(-cap_ctx // SUB) * SUB
    slots_max = lat_base + cap_lat + (SUB - 1) * (n_lat // TILE)
    rows_ffn = -(-slots_max // 16) * 16
    return cap_ctx, cap_lat, lat_base, rows_ffn


def _topk_kernel(a_ref, slot_ref, slot_t_ref, aff_t_ref, off_ref, cnt_ref, *, n_ctx, cap_ctx, cap_lat, lat_base):
    bits = lax.bitcast_convert_type(a_ref[0], I32)
    prefix = (_iota((LANE, LANE), 0) <= _iota((LANE, LANE), 1)).astype(BF16)

    def count(mask):
        return jnp.sum(mask.astype(F32), axis=1, keepdims=True)

    def select(x, cap):
        thr = jnp.zeros((N_EXP, 1), I32)
        for bit in range(30, -1, -1):
            cand = thr | (1 << bit)
            thr = jnp.where(count(x >= cand) >= cap, cand, thr)
        gt = x > thr
        eq = x == thr
        need = cap - count(gt)
        run = jnp.zeros((N_EXP, 1), F32)
        blocks = []
        for j in range(x.shape[1] // LANE):
            sl = slice(LANE * j, LANE * (j + 1))
            eqf = eq[:, sl].astype(F32)
            inc = _dot(eqf.astype(BF16), prefix)
            rank = run + inc - eqf
            blocks.append(jnp.logical_or(gt[:, sl], jnp.logical_and(eq[:, sl], rank < need)))
            run = run + inc[:, LANE - 1:LANE]
        return blocks

    blocks = select(bits[:, :n_ctx], cap_ctx) + select(bits[:, n_ctx:], cap_lat)
    per_tile = TILE // LANE
    lane = _iota((N_EXP, LANE), 1)
    off_acc = jnp.zeros((N_EXP, LANE), I32)
    cnt_acc = jnp.zeros((N_EXP, LANE), I32)
    base = jnp.zeros((N_EXP, 1), F32)
    for t in range(len(blocks) // per_tile):
        if t == n_ctx // TILE:
            base = jnp.full((N_EXP, 1), float(lat_base), F32)
        run = jnp.zeros((N_EXP, 1), F32)
        for j in range(per_tile):
            blk = blocks[per_tile * t + j]
            sf = blk.astype(F32)
            inc = _dot(sf.astype(BF16), prefix)
            pos = base + run + inc - sf
            c0 = LANE * (per_tile * t + j)
            slot_blk = jnp.concatenate([jnp.where(blk, pos.astype(I32), -1),
                                        jnp.full((LANE - N_EXP, LANE), -1, I32)], axis=0)
            slot_ref[0, :, c0:c0 + LANE] = slot_blk
            slot_t_ref[0, c0:c0 + LANE, :] = jnp.transpose(slot_blk)
            aff_blk = jnp.concatenate([a_ref[0, :, c0:c0 + LANE], jnp.zeros((LANE - N_EXP, LANE), F32)], axis=0)
            aff_t_ref[0, c0:c0 + LANE, :] = jnp.transpose(aff_blk)
            run = run + inc[:, LANE - 1:LANE]
        n8 = jnp.floor((run + (SUB - 1)) * (1.0 / SUB)) * SUB
        off_acc = jnp.where(lane == t, base.astype(I32), off_acc)
        cnt_acc = jnp.where(lane == t, n8.astype(I32), cnt_acc)
        base = base + n8
    off_ref[0] = off_acc
    cnt_ref[0] = cnt_acc


def _topk(aff_t, n_ctx):
    nb, _, ntot = aff_t.shape
    cap_ctx, cap_lat, lat_base, _ = _slot_geometry(n_ctx, ntot - n_ctx)
    kern = functools.partial(_topk_kernel, n_ctx=n_ctx, cap_ctx=cap_ctx, cap_lat=cap_lat, lat_base=lat_base)
    return pl.pallas_call(
        kern,
        out_shape=(jax.ShapeDtypeStruct((nb, LANE, ntot), I32), jax.ShapeDtypeStruct((nb, ntot, LANE), I32),
                   jax.ShapeDtypeStruct((nb, ntot, LANE), F32), jax.ShapeDtypeStruct((nb, N_EXP, LANE), I32),
                   jax.ShapeDtypeStruct((nb, N_EXP, LANE), I32)),
        grid=(nb,),
        in_specs=[pl.BlockSpec((1, N_EXP, ntot), lambda b: (b, 0, 0))],
        out_specs=(pl.BlockSpec((1, LANE, ntot), lambda b: (b, 0, 0)),
                   pl.BlockSpec((1, ntot, LANE), lambda b: (b, 0, 0)),
                   pl.BlockSpec((1, ntot, LANE), lambda b: (b, 0, 0)),
                   pl.BlockSpec((1, N_EXP, LANE), lambda b: (b, 0, 0)),
                   pl.BlockSpec((1, N_EXP, LANE), lambda b: (b, 0, 0))),
        compiler_params=_cparams(("arbitrary",)),
        name="ec_topk",
    )(aff_t)


def _strip_index(shape, dim):
    i = _iota(shape, dim)
    e = jnp.floor((i.astype(F32) + 0.5) * (1.0 / WIN)).astype(I32)
    return e, i - WIN * e


def _dispatch_kernel(off_s, rounds_s, h_ref, slot_ref, aff_ref, xg_ref, stage_ref, sem, cnt_ref):
    b = pl.program_id(0)
    t = pl.program_id(1)
    nt = pl.num_programs(1)
    base = (b * nt + t) * N_EXP
    rows = N_EXP * WIN
    a = aff_ref[0]
    src = _iota((LANE, LANE), 0)
    dst = _iota((LANE, LANE), 1)
    a3 = None
    for p, piece in enumerate(_split3(a)):
        sel = jnp.logical_and(dst == 3 * src + p, src < N_EXP).astype(BF16)
        term = _dot(piece, sel)
        a3 = term if a3 is None else a3 + term
    rhs = jnp.concatenate([h_ref[0], a3.astype(BF16)], axis=1)
    e_of_row, _ = _strip_index((rows, LANE), 0)
    expand = (e_of_row == _iota((rows, LANE), 1)).astype(BF16)
    _, j_row = _strip_index((rows, 1), 0)
    sl = slot_ref[0]
    e_row = _iota((LANE, 1), 0)
    off_v = jnp.zeros((LANE, 1), I32)
    for e in range(N_EXP):
        off_v = jnp.where(e_row == e, off_s[base + e], off_v)

    @pl.when(jnp.logical_and(b == 0, t == 0))
    def _():
        cnt_ref[0] = 0

    def strip_copies(buf, sample, starts):
        return [pltpu.make_async_copy(stage_ref.at[buf, pl.ds(WIN * e, WIN), :],
                                      xg_ref.at[sample, e, pl.ds(starts[e], WIN), :], sem.at[buf])
                for e in range(N_EXP)]

    def wait_strips(buf):
        for cp in strip_copies(buf, 0, [0] * N_EXP):
            cp.wait()

    def round_body(r, carry):
        n = cnt_ref[0]
        buf = n & 1
        rel = jnp.clip(sl - (off_v + WIN * r), -1, WIN).astype(F32).astype(BF16)
        relx = _dot(expand, rel)
        onehot = (relx == j_row.astype(F32)).astype(BF16)
        bits = lax.bitcast_convert_type(_dot(onehot, rhs), I32)
        packed = jnp.bitwise_or(jnp.bitwise_and(bits[:, :HALF] >> 16, 0xFFFF),
                                jnp.bitwise_and(bits[:, HALF:D], HIGH16))
        stage_ref[buf] = jnp.concatenate([packed, bits[:, D:]], axis=1)

        @pl.when(n > 0)
        def _():
            wait_strips(1 - buf)

        last_start = xg_ref.shape[2] - WIN
        starts = [pl.multiple_of(jnp.minimum(off_s[base + e] + WIN * r, last_start), SUB) for e in range(N_EXP)]
        for cp in strip_copies(buf, b, starts):
            cp.start()
        cnt_ref[0] = n + 1
        return carry

    lax.fori_loop(0, rounds_s[b * nt + t], round_body, 0)

    @pl.when(jnp.logical_and(b == pl.num_programs(0) - 1, t == nt - 1))
    def _():
        wait_strips((cnt_ref[0] - 1) & 1)


def _dispatch(h2, slot_pad, aff_pad, off_flat, rounds_flat, rows_alloc):
    nb, ntot, _ = h2.shape
    nt = ntot // TILE
    grid_spec = pltpu.PrefetchScalarGridSpec(
        num_scalar_prefetch=2,
        grid=(nb, nt),
        in_specs=[pl.BlockSpec((1, TILE, D), lambda b, t, o, r: (b, t, 0)),
                  pl.BlockSpec((1, LANE, TILE), lambda b, t, o, r: (b, 0, t)),
                  pl.BlockSpec((1, TILE, LANE), lambda b, t, o, r: (b, t, 0))],
        out_specs=pl.BlockSpec(memory_space=pl.ANY),
        scratch_shapes=[pltpu.VMEM((2, N_EXP * WIN, XP), I32), pltpu.SemaphoreType.DMA((2,)),
                        pltpu.SMEM((1,), I32)],
    )
    return pl.pallas_call(
        _dispatch_kernel,
        out_shape=jax.ShapeDtypeStruct((nb, N_EXP, rows_alloc, XP), I32),
        grid_spec=grid_spec,
        compiler_params=_cparams(("arbitrary", "arbitrary")),
        name="ec_dispatch",
    )(off_flat, rounds_flat, h2, slot_pad, aff_pad)


def _ffn_kernel(used_s, xg_ref, wg32_ref, wu32_ref, wd32_ref, y_ref, wg_ref, wu_ref, wd_ref, *, rows_short):
    e = pl.program_id(0)
    b = pl.program_id(1)

    @pl.when(b == 0)
    def _():
        wg_ref[0] = wg32_ref[0].astype(BF16)
        wu_ref[0] = wu32_ref[0].astype(BF16)
        wd_ref[0] = wd32_ref[0].astype(BF16)

    rows_all = y_ref.shape[2]

    def run(rows):
        words = xg_ref[0, 0, :rows, :HALF]
        x = jnp.concatenate([lax.bitcast_convert_type(words << 16, F32),
                             lax.bitcast_convert_type(jnp.bitwise_and(words, HIGH16), F32)], axis=1).astype(BF16)
        gl = lax.bitcast_convert_type(xg_ref[0, 0, :rows, HALF:], F32)
        lane = _iota(gl.shape, 1)
        mine = jnp.logical_and(lane >= 3 * e, lane < 3 * e + 3)
        gate = jnp.sum(jnp.where(mine, gl, 0.0), axis=1, keepdims=True)
        a = _dot(x, wg_ref[0])
        u = _dot(x, wu_ref[0])
        hm = (a * jax.nn.sigmoid(a) * u).astype(BF16)
        y_ref[0, 0, :rows] = _dot(hm, wd_ref[0]) * gate
        if rows < rows_all:
            y_ref[0, 0, rows:] = jnp.zeros((rows_all - rows, D), F32)

    used = used_s[b * N_EXP + e]

    @pl.when(used <= rows_short)
    def _():
        run(rows_short)

    @pl.when(used > rows_short)
    def _():
        run(rows_all)


def _expert_ffn(xg, wg, wu, wd, used_flat, rows_ffn, layer=0):
    nb = xg.shape[0]
    rows_short = max(16, rows_ffn - 3 * 16)
    wspec = pl.BlockSpec((None, 1, D, D), lambda e, b, u: (layer, e, 0, 0))
    grid_spec = pltpu.PrefetchScalarGridSpec(
        num_scalar_prefetch=1,
        grid=(N_EXP, nb),
        in_specs=[pl.BlockSpec((1, 1, rows_ffn, XP), lambda e, b, u: (b, e, 0, 0)), wspec, wspec, wspec],
        out_specs=pl.BlockSpec((1, 1, rows_ffn, D), lambda e, b, u: (b, e, 0, 0)),
        scratch_shapes=[pltpu.VMEM((1, D, D), BF16)] * 3,
    )
    return pl.pallas_call(
        functools.partial(_ffn_kernel, rows_short=rows_short),
        out_shape=jax.ShapeDtypeStruct((nb, N_EXP, rows_ffn, D), F32),
        grid_spec=grid_spec,
        compiler_params=_cparams(("arbitrary", "arbitrary")),
        name="ec_ffn",
    )(used_flat, xg, wg, wu, wd)


def _combine_kernel(off_s, rounds_s, x_ref, mod_ref, slot_ref, fin_ref, y_ref, *rest, final_ctx_tiles, proj_body,
                    n_proj_in):
    if proj_body is None:
        o_ref, strip_ref, sem = rest
    else:
        next_mod_ref = rest[0]
        proj_in = rest[1:1 + n_proj_in]
        o_ref = rest[1 + n_proj_in]
        proj_out = rest[2 + n_proj_in:-2]
        strip_ref, sem = rest[-2:]
    b = pl.program_id(0)
    t = pl.program_id(1)
    nt = pl.num_programs(1)
    base = (b * nt + t) * N_EXP
    cols = N_EXP * WIN
    e_of_col, _ = _strip_index((LANE, cols), 1)
    expand = (e_of_col == _iota((LANE, cols), 0)).astype(BF16)
    _, j_lane = _strip_index((1, cols), 1)
    j_lane = j_lane.astype(F32)
    sl = slot_ref[0]
    e_lane = _iota((1, LANE), 1)
    last_start = y_ref.shape[2] - WIN

    def strip_start(step_, e, r):
        return jnp.minimum(off_s[step_ * N_EXP + e] + WIN * r, last_start)

    step = b * nt + t
    buf = step & 1

    def strip_copies(step_, sample, r, dst):
        return [pltpu.make_async_copy(
            y_ref.at[sample, e, pl.ds(pl.multiple_of(strip_start(step_, e, r), SUB), WIN), :],
            strip_ref.at[dst, pl.ds(WIN * e, WIN), :], sem.at[dst]) for e in range(N_EXP)]

    @pl.when(step == 0)
    def _():
        for cp in strip_copies(step, b, 0, buf):
            cp.start()

    @pl.when(step + 1 < pl.num_programs(0) * nt)
    def _():
        for cp in strip_copies(step + 1, jnp.where(t == nt - 1, b + 1, b), 0, 1 - buf):
            cp.start()

    def expand_round(r, acc):
        first_v = jnp.zeros((1, LANE), I32)
        start_v = jnp.zeros((1, LANE), I32)
        for e in range(N_EXP):
            first_v = jnp.where(e_lane == e, off_s[base + e] + WIN * r, first_v)
            start_v = jnp.where(e_lane == e, strip_start(step, e, r), start_v)
        nominal = sl - first_v
        in_round = jnp.logical_and(nominal >= 0, nominal < WIN)
        rel = jnp.where(in_round, sl - start_v, -1).astype(F32).astype(BF16)
        relx = _dot(rel, expand)
        onehot = (relx == j_lane).astype(BF16)
        for cp in strip_copies(step, b, r, buf):
            cp.wait()
        return acc + _dot(onehot, strip_ref[buf].astype(BF16))

    def extra_round(r, acc):
        for cp in strip_copies(step, b, r, buf):
            cp.start()
        return expand_round(r, acc)

    acc = expand_round(0, jnp.zeros((TILE, D), F32))
    acc = lax.fori_loop(1, rounds_s[step], extra_round, acc)
    x2 = x_ref[0] + mod_ref[0][5:6] * acc
    if final_ctx_tiles is None:
        o_ref[0] = x2
        if proj_body is not None:
            proj_body(x2, next_mod_ref[0], *proj_in, *proj_out)
    else:
        @pl.when(t >= final_ctx_tiles)
        def _():
            ms = jnp.mean(x2 * x2, axis=-1, keepdims=True)
            o_ref[0] = x2 * lax.rsqrt(ms + EPS) * fin_ref[...]


def _combine(x1, mod_i, slot_t_pad, y, off_flat, rounds_flat, final_gain, final_ctx_tiles=None, next_proj=None):
    nb, ntot, _ = x1.shape
    nt = ntot // TILE
    skip = 0 if final_ctx_tiles is None else final_ctx_tiles
    in_specs = [_tok_spec(D), _mod_spec(nb), _tok_spec(LANE), pl.BlockSpec((1, D), lambda b, t, *_: (0, 0)),
                pl.BlockSpec(memory_space=pl.ANY)]
    args = [x1, mod_i, slot_t_pad, final_gain, y]
    out_shape = [jax.ShapeDtypeStruct((nb, ntot - skip * TILE, D), F32)]
    out_specs = [pl.BlockSpec((1, TILE, D), lambda b, t, *_: (b, jnp.maximum(t - skip, 0), 0))]
    proj_body, n_proj_in = None, 0
    if next_proj is not None:
        proj_body, next_mod, (arrays, specs, p_shape, p_specs) = next_proj
        in_specs += [_mod_spec(nb)] + specs
        args += [next_mod] + arrays
        out_shape += p_shape
        out_specs += p_specs
        n_proj_in = len(arrays)
    grid_spec = pltpu.PrefetchScalarGridSpec(
        num_scalar_prefetch=2,
        grid=(nb, nt),
        in_specs=in_specs,
        out_specs=tuple(out_specs),
        scratch_shapes=[pltpu.VMEM((2, N_EXP * WIN, D), F32), pltpu.SemaphoreType.DMA((2,))],
    )
    outs = pl.pallas_call(
        functools.partial(_combine_kernel, final_ctx_tiles=final_ctx_tiles, proj_body=proj_body,
                          n_proj_in=n_proj_in),
        out_shape=tuple(out_shape),
        grid_spec=grid_spec,
        compiler_params=_cparams(("arbitrary", "arbitrary")),
        name="ec_combine",
    )(off_flat, rounds_flat, *args)
    return outs[0], tuple(outs[1:])


def _moe(x1, h2, aff_t, mod_i, wg, wu, wd, n_ctx, final_gain, last, layer=0, next_proj=None):
    nb, ntot, _ = x1.shape
    nt = ntot // TILE
    _, _, _, rows_ffn = _slot_geometry(n_ctx, ntot - n_ctx)
    rows_alloc = rows_ffn + WIN
    slot_pad, slot_t_pad, aff_pad, off, cnt = _topk(aff_t, n_ctx)
    off_t = jnp.swapaxes(off[:, :, :nt], 1, 2)
    cnt_t = jnp.swapaxes(cnt[:, :, :nt], 1, 2)
    rounds_c = jnp.maximum(1, jnp.max((cnt_t + WIN - 1) // WIN, axis=2))
    fill = jnp.max((rows_alloc - off_t[:, nt - 1, :] + WIN - 1) // WIN, axis=1)
    rounds_d = rounds_c.at[:, nt - 1].max(fill)
    off_flat = off_t.reshape(-1)
    xg = _dispatch(h2, slot_pad, aff_pad, off_flat, rounds_d.reshape(-1), rows_alloc)
    used = (off_t[:, nt - 1, :] + cnt_t[:, nt - 1, :]).reshape(-1)
    y = _expert_ffn(xg, wg, wu, wd, used, rows_ffn, layer)
    return _combine(x1, mod_i, slot_t_pad, y, off_flat, rounds_c.reshape(-1), final_gain,
                    final_ctx_tiles=n_ctx // TILE if last else None, next_proj=next_proj)


def kernel(x, c, ctx, c_ctx, w_mod, b_mod, norm_mix, norm_ffn, mlstm_w_in, mlstm_b_gate, mlstm_norm, mlstm_w_out,
           attn_w_in, attn_q_norm, attn_k_norm, attn_w_out, moe_router, moe_w_gate, moe_w_up, moe_w_down,
           norm_final):
    nb, n_lat, _ = x.shape
    n_ctx = ctx.shape[1]
    depth = w_mod.shape[0]
    assert n_ctx == TILE and n_lat % TILE == 0 and x.shape[2] == D
    xa = (ctx, x)
    rb = -(-(nb + 1) // SUB) * SUB
    cc = jnp.concatenate([c, c_ctx[None, :], jnp.zeros((rb - nb - 1, D), F32)], axis=0)
    mod = _modulation(cc, w_mod, b_mod)
    cos, sin = _rope_tables(n_lat, n_ctx)
    ntot = n_ctx + n_lat

    def proj_operands(i):
        j = i // 2
        gain_mix = norm_mix[i].reshape(1, D)
        if i % 2 == 0:
            return _proj_mlstm_body, _proj_mlstm_operands(nb, ntot, gain_mix, mlstm_w_in[j], mlstm_b_gate[j]), None
        ops, flag = _proj_attn_operands(nb, ntot, gain_mix, attn_w_in[j], attn_q_norm[j], attn_k_norm[j], cos, sin)
        return _proj_attn_body, ops, flag

    _, ops, flag = proj_operands(0)
    proj = _proj_mlstm(xa, mod[0], ops)
    for i in range(depth):
        j = i // 2
        mod_i = mod[i]
        gain_ffn = norm_ffn[i].reshape(1, D)
        if i % 2 == 0:
            q, kt, v, og, gc, gr = proj
            hf, hb = _mlstm_scan(q, kt, v, gc, gr)
            x1, h2, aff_t = _readout(xa, mod_i, (hf, hb, og), mlstm_w_out[j], gain_ffn, moe_router[i],
                                     mlstm_norm=mlstm_norm[j].reshape(1, M_V))
        else:
            q, kt, vd = proj
            oa = _attention(q, kt, vd, flag)
            x1, h2, aff_t = _readout(xa, mod_i, oa, attn_w_out[j], gain_ffn, moe_router[i])
        last = i == depth - 1
        next_proj = None
        if not last:
            body, ops, flag = proj_operands(i + 1)
            next_proj = (body, mod[i + 1], ops)
        xa, proj = _moe(x1, h2, aff_t, mod_i, moe_w_gate, moe_w_up, moe_w_down, n_ctx,
                        norm_final.reshape(1, D), last=last, layer=i, next_proj=next_proj)
    return xa
```

```python
import functools

import jax
import jax.numpy as jnp
from jax import lax
from jax.experimental import pallas as pl
from jax.experimental.pallas import tpu as pltpu

F32 = jnp.float32
BF16 = jnp.bfloat16
I32 = jnp.int32

D = 1024
TILE = 256
HALVES = (slice(0, TILE // 2), slice(TILE // 2, TILE))
EPS = 1e-6
DEPTH = 4

M_HEADS = 4
M_DK = 128
M_DV = 256
M_QK = M_HEADS * M_DK
M_V = M_HEADS * M_DV
M_AUG = M_DV + 128
GATE_CAP = 15.0

A_HEADS = 16
A_KV = 4
A_GRP = 4
A_HD = 64
ROPE_THETA = 10000.0
GRID_W = 64
LOG2E = 1.4426950408889634

N_EXP = 16
EC_CAPACITY = 2
WIN = 48
GATE_LANES = 128
XW = D + GATE_LANES
HALF = D // 2
XP = HALF + GATE_LANES
HIGH16 = -65536

LANE = 128
SUB = 8
VMEM_LIMIT = 56 * 1024 * 1024


def _cparams(sem):
    return pltpu.CompilerParams(dimension_semantics=sem, vmem_limit_bytes=VMEM_LIMIT)


def _dot(a, b):
    return jnp.dot(a, b, preferred_element_type=F32)


def _dot_nt(a, b):
    return lax.dot_general(a, b, (((1,), (1,)), ((), ())), preferred_element_type=F32)


def _split2(x):
    hi = x.astype(BF16)
    lo = (x - hi.astype(F32)).astype(BF16)
    return hi, lo


def _split3(x):
    hi = x.astype(BF16)
    r = x - hi.astype(F32)
    mid = r.astype(BF16)
    lo = (r - mid.astype(F32)).astype(BF16)
    return hi, mid, lo


def _dot3(a, b):
    ah, al = _split2(a)
    bh, bl = _split2(b)
    return _dot(ah, bh) + (_dot(ah, bl) + _dot(al, bh))


def _rms_mod(x, gain, shift, scale):
    ms = jnp.mean(x * x, axis=-1, keepdims=True)
    y = x * lax.rsqrt(ms + EPS) * gain
    return y * (1.0 + scale) + shift


def _iota(shape, dim):
    return lax.broadcasted_iota(I32, shape, dim)


def _mod_kernel(c_ref, w_ref, b_ref, o_ref):
    c = c_ref[...]
    s = c * jax.nn.sigmoid(c)
    o_ref[...] = _dot3(s, w_ref[...]) + b_ref[...]


def _modulation(cc, w_mod, b_mod):
    depth, _, n6 = w_mod.shape
    rb = cc.shape[0]
    nj = n6 // D
    out = pl.pallas_call(
        _mod_kernel,
        out_shape=jax.ShapeDtypeStruct((depth, rb, n6), F32),
        grid=(depth, nj),
        in_specs=[
            pl.BlockSpec((rb, D), lambda i, j: (0, 0)),
            pl.BlockSpec((None, D, D), lambda i, j: (i, 0, j)),
            pl.BlockSpec((None, 1, D), lambda i, j: (i, 0, j)),
        ],
        out_specs=pl.BlockSpec((None, rb, D), lambda i, j: (i, 0, j)),
        compiler_params=_cparams(("arbitrary", "arbitrary")),
        name="adaln_mod",
    )(cc, w_mod, b_mod.reshape(depth, 1, n6))
    return out.reshape(depth, rb, nj, D)


def _gate_act(g, idx):
    g = GATE_CAP * jnp.tanh(g * (1.0 / GATE_CAP))
    logsig = jnp.minimum(g, 0.0) - jnp.log(1.0 + jnp.exp(-jnp.abs(g)))
    is_forget = ((idx >> 2) & 1) == 1
    return jnp.where(is_forget, logsig, g)


def _proj_mlstm_body(x, m, gain_ref, w_ref, wkt_ref, wg_ref, wgt_ref, bc_ref, br_ref,
                     q_ref, kt_ref, v_ref, o_ref, gc_ref, gr_ref):
    h = _rms_mod(x, gain_ref[...], m[0:1], m[1:2])
    hb = h.astype(BF16)
    r = _dot(hb, w_ref[...])
    q_ref[0] = (r[:, :M_QK] * (M_DK ** -0.5)).astype(BF16)
    v_ref[0] = r[:, M_QK:M_QK + M_V].astype(BF16)
    o_ref[0] = r[:, M_QK + M_V:].astype(BF16)
    kt_ref[0] = _dot_nt(wkt_ref[...], hb).astype(BF16)
    gc = _dot(hb, wg_ref[...]) + bc_ref[...]
    gc_ref[0] = _gate_act(gc, _iota(gc.shape, 1))
    gr = _dot_nt(wgt_ref[...], hb) + br_ref[...]
    gr_ref[0] = _gate_act(gr, _iota(gr.shape, 0))


def _proj_mlstm_kernel(*refs, n_src):
    _proj_mlstm_body(_tile_value(refs[:n_src]), refs[n_src][0], *refs[n_src + 1:])


def _tile_sources(xa):
    if isinstance(xa, tuple):
        ctx, lat = xa
        assert ctx.shape[1] == TILE
        specs = [pl.BlockSpec((1, TILE, D), lambda b, t, *_: (b, 0, 0)),
                 pl.BlockSpec((1, TILE, D), lambda b, t, *_: (b, jnp.maximum(t - 1, 0), 0))]
        return [ctx, lat], specs, lat.shape[0], ctx.shape[1] + lat.shape[1]
    return [xa], [_tok_spec(D)], xa.shape[0], xa.shape[1]


def _tile_value(src_refs):
    if len(src_refs) == 1:
        return src_refs[0][0]
    return jnp.where(pl.program_id(1) == 0, src_refs[0][0], src_refs[1][0])


def _tok_spec(width):
    return pl.BlockSpec((1, TILE, width), lambda b, t, *_: (b, t, 0))


def _full_spec(a):
    return pl.BlockSpec(a.shape, lambda b, t, *_: (0,) * a.ndim)


def _mod_spec(nb):
    return pl.BlockSpec((1, 6, D), lambda b, t, *_: (jnp.where(t == 0, nb, b), 0, 0))


def _proj_mlstm_operands(nb, ntot, gain, w_in, b_gate):
    n_g = 4 * M_HEADS
    wq = w_in[:, :M_QK]
    wk = w_in[:, M_QK:2 * M_QK]
    wvo = w_in[:, 2 * M_QK:2 * M_QK + 2 * M_V]
    wg = w_in[:, 2 * M_QK + 2 * M_V:]
    w_main = jnp.concatenate([wq, wvo], axis=1).astype(BF16)
    wkt = wk.T.astype(BF16)
    wg_pad = jnp.pad(wg, ((0, 0), (0, LANE - n_g))).astype(BF16)
    wgt = wg.T.astype(BF16)
    bc = jnp.pad(b_gate, (0, LANE - n_g)).reshape(1, LANE)
    br = b_gate.reshape(n_g, 1)
    arrays = [gain, w_main, wkt, wg_pad, wgt, bc, br]
    out_shape = [
        jax.ShapeDtypeStruct((nb, ntot, M_QK), BF16),
        jax.ShapeDtypeStruct((nb, M_QK, ntot), BF16),
        jax.ShapeDtypeStruct((nb, ntot, M_V), BF16),
        jax.ShapeDtypeStruct((nb, ntot, M_V), BF16),
        jax.ShapeDtypeStruct((nb, ntot, LANE), F32),
        jax.ShapeDtypeStruct((nb, n_g, ntot), F32),
    ]
    out_specs = [_tok_spec(M_QK), pl.BlockSpec((1, M_QK, TILE), lambda b, t, *_: (b, 0, t)), _tok_spec(M_V),
                 _tok_spec(M_V), _tok_spec(LANE), pl.BlockSpec((1, n_g, TILE), lambda b, t, *_: (b, 0, t))]
    return arrays, [_full_spec(a) for a in arrays], out_shape, out_specs


def _proj_mlstm(xa, mod_i, operands):
    srcs, src_specs, nb, ntot = _tile_sources(xa)
    arrays, in_specs, out_shape, out_specs = operands
    return pl.pallas_call(
        functools.partial(_proj_mlstm_kernel, n_src=len(srcs)),
        out_shape=tuple(out_shape),
        grid=(nb, ntot // TILE),
        in_specs=src_specs + [_mod_spec(nb)] + in_specs,
        out_specs=tuple(out_specs),
        compiler_params=_cparams(("arbitrary", "arbitrary")),
        name="proj_mlstm",
    )(*srcs, mod_i, *arrays)


def _mlstm_kernel(qf_ref, qb_ref, kf_ref, kb_ref, vf_ref, vb_ref, gcf_ref, gcb_ref, grf_ref, grb_ref,
                  hf_ref, hb_ref, c_ref, m_ref):
    t = pl.program_id(1)

    @pl.when(t == 0)
    def _():
        c_ref[...] = jnp.zeros(c_ref.shape, F32)
        m_ref[...] = jnp.zeros(m_ref.shape, F32)

    n = TILE
    row = _iota((n, n), 0)
    col = _iota((n, n), 1)
    lower = col <= row
    upper = col >= row
    lower_b = lower.astype(BF16)
    upper_b = upper.astype(BF16)
    ones_col = (_iota((n, M_AUG - M_DV), 1) == 0).astype(BF16)
    dirs = ((qf_ref, kf_ref, vf_ref, gcf_ref, grf_ref, hf_ref, lower, lower_b, upper_b, n - 1),
            (qb_ref, kb_ref, vb_ref, gcb_ref, grb_ref, hb_ref, upper, upper_b, lower_b, 0))
    c_old = [c_ref[i] for i in range(2 * M_HEADS)]
    m_old = [m_ref[i][0:1, 0:1] for i in range(2 * M_HEADS)]
    c_new, m_new_all, h_out = {}, {}, {0: [], 1: []}
    for d, (q_ref, k_ref, v_ref, gc_ref, gr_ref, o_ref, mask, cum_l, cum_r, last) in enumerate(dirs):
        gc = gc_ref[0]
        gr = gr_ref[0]
        bcol = sum(_dot(cum_l, p) for p in _split3(gc))
        brow = sum(_dot(p, cum_r) for p in _split3(gr))
        for h in range(M_HEADS):
            gi = 8 * d + h
            gf = gi + 4
            sidx = 4 * d + h
            b_col = bcol[:, gf:gf + 1]
            b_row = brow[gf:gf + 1, :]
            ig_row = gr[gi:gi + 1, :]
            total = b_row[:, last:last + 1]
            m_st = m_old[sidx]
            key_row = ig_row - b_row
            log_d = jnp.where(mask, b_col + key_row, -jnp.inf)
            m_inter = b_col + m_st
            m_q = jnp.maximum(m_inter, jnp.max(log_d, axis=1, keepdims=True))
            w_intra = jnp.exp(log_d - m_q)
            w_inter = jnp.exp(m_inter - m_q)
            qh = q_ref[0, :, M_DK * h:M_DK * (h + 1)]
            kth = k_ref[0, M_DK * h:M_DK * (h + 1), :]
            v_aug = jnp.concatenate([v_ref[0, :, M_DV * h:M_DV * (h + 1)], ones_col], axis=1)
            c_aug = c_old[sidx]
            s = (_dot(qh, kth) * w_intra).astype(BF16)
            nd = _dot(s, v_aug) + w_inter * _dot(qh, c_aug.astype(BF16))
            den = nd[:, M_DV:M_DV + 1]
            inv = 1.0 / jnp.maximum(jnp.abs(den), jnp.exp(-m_q))
            h_out[d].append((nd[:, :M_DV] * inv).astype(o_ref.dtype))
            log_w = total + key_row
            m_new = jnp.maximum(total + m_st, jnp.max(log_w, axis=1, keepdims=True))
            w_key = jnp.exp(log_w - m_new)
            decay = jnp.exp(total + m_st - m_new)
            kw = (kth.astype(F32) * w_key).astype(BF16)
            c_new[sidx] = decay * c_aug + _dot(kw, v_aug)
            m_new_all[sidx] = m_new
    hf_ref[0] = jnp.concatenate(h_out[0], axis=1)
    hb_ref[0] = jnp.concatenate(h_out[1], axis=1)
    for i in range(2 * M_HEADS):
        c_ref[i] = c_new[i]
        m_ref[i] = jnp.broadcast_to(m_new_all[i], m_ref.shape[1:])


def _mlstm_scan(q, kt, v, gc, gr):
    nb, ntot, _ = q.shape
    nt = ntot // TILE
    fwd = lambda b, t: (b, t, 0)
    bwd = lambda b, t: (b, jnp.where(t == 0, 0, nt - t), 0)
    fwd_t = lambda b, t: (b, 0, t)
    bwd_t = lambda b, t: (b, 0, jnp.where(t == 0, 0, nt - t))
    n_g = gr.shape[1]
    return pl.pallas_call(
        _mlstm_kernel,
        out_shape=(jax.ShapeDtypeStruct((nb, ntot, M_V), BF16), jax.ShapeDtypeStruct((nb, ntot, M_V), BF16)),
        grid=(nb, nt),
        in_specs=[
            pl.BlockSpec((1, TILE, M_QK), fwd), pl.BlockSpec((1, TILE, M_QK), bwd),
            pl.BlockSpec((1, M_QK, TILE), fwd_t), pl.BlockSpec((1, M_QK, TILE), bwd_t),
            pl.BlockSpec((1, TILE, M_V), fwd), pl.BlockSpec((1, TILE, M_V), bwd),
            pl.BlockSpec((1, TILE, LANE), fwd), pl.BlockSpec((1, TILE, LANE), bwd),
            pl.BlockSpec((1, n_g, TILE), fwd_t), pl.BlockSpec((1, n_g, TILE), bwd_t),
        ],
        out_specs=(pl.BlockSpec((1, TILE, M_V), fwd), pl.BlockSpec((1, TILE, M_V), bwd)),
        scratch_shapes=[pltpu.VMEM((2 * M_HEADS, M_DK, M_AUG), F32), pltpu.VMEM((2 * M_HEADS, SUB, LANE), F32)],
        compiler_params=_cparams(("arbitrary", "arbitrary")),
        name="mlstm_scan",
    )(q, q, kt, kt, v, v, gc, gc, gr, gr)


def _head_norm(x, gain):
    w = x.shape[1]
    gsum = ((_iota((w, LANE), 0) >> 6) == _iota((w, LANE), 1)).astype(BF16)
    gexp = ((_iota((LANE, w), 1) >> 6) == _iota((LANE, w), 0)).astype(BF16)
    hi, lo = _split2(x * x)
    ssum = _dot(hi, gsum) + _dot(lo, gsum)
    rh, rl = _split2(lax.rsqrt(ssum * (1.0 / A_HD) + EPS))
    return x * (_dot(rh, gexp) + _dot(rl, gexp)) * gain


def _rope(x, cos, sin_signed):
    first = (_iota((x.shape[0], LANE), 1) & (A_HD - 1)) < A_HD // 2
    tiles = []
    for i in range(x.shape[1] // LANE):
        xt = x[:, LANE * i:LANE * (i + 1)]
        partner = jnp.where(first, pltpu.roll(xt, LANE - A_HD // 2, 1), pltpu.roll(xt, A_HD // 2, 1))
        tiles.append(xt * cos + partner * sin_signed)
    return jnp.concatenate(tiles, axis=1)


def _proj_attn_body(x, m, gain_ref, w_ref, qg_ref, kg_ref, cos_ref, sin_ref, shift_ref, q_ref, kt_ref, vd_ref):
    h = _rms_mod(x, gain_ref[...], m[0:1], m[1:2])
    r = _dot(h.astype(BF16), w_ref[...])
    nq = A_HEADS * A_HD
    nk = A_KV * A_HD
    cos = cos_ref[...]
    sin = sin_ref[...]
    qn = _rope(_head_norm(r[:, :nq], qg_ref[...]), cos, sin)
    kn = _rope(_head_norm(r[:, nq:nq + nk], kg_ref[...]), cos, sin)
    v = r[:, nq + nk:]
    lane = _iota((TILE, LANE), 1)
    low = lane < A_HD
    one_hot = jnp.where(lane == A_HD, 1.0, 0.0)

    def head_tile(a, idx):
        tile = a[:, LANE * (idx // 2):LANE * (idx // 2 + 1)]
        if idx % 2 == 1:
            tile = pltpu.roll(tile, A_HD, 1)
        return jnp.where(low, tile, one_hot).astype(BF16)

    for hd in range(A_HEADS):
        q_ref[0, hd] = head_tile(qn, hd)
    for g in range(A_KV):
        vd_ref[0, g] = head_tile(v, g)
    kt = jnp.transpose(kn)
    extra = jnp.where(_iota((LANE - A_HD, TILE), 0) == 0, shift_ref[...], 0.0).astype(BF16)
    for g in range(A_KV):
        kt_ref[0, g, 0:A_HD, :] = kt[A_HD * g:A_HD * (g + 1), :].astype(BF16)
        kt_ref[0, g, A_HD:LANE, :] = extra


def _rope_tables(n_lat, n_ctx):
    rows = n_lat // GRID_W
    row = jnp.repeat(jnp.arange(rows, dtype=F32), GRID_W)
    col = jnp.tile(jnp.arange(GRID_W, dtype=F32), rows)
    pairs = A_HD // 4
    inv = ROPE_THETA ** (-jnp.arange(pairs, dtype=F32) / pairs)
    ang = jnp.concatenate([row[:, None] * inv, col[:, None] * inv], axis=-1)
    c = jnp.cos(ang)
    s = jnp.sin(ang)
    cos = jnp.concatenate([c, c, c, c], axis=-1)
    sin = jnp.concatenate([-s, s, -s, s], axis=-1)
    cos = jnp.concatenate([jnp.ones((n_ctx, LANE), F32), cos], axis=0)
    sin = jnp.concatenate([jnp.zeros((n_ctx, LANE), F32), sin], axis=0)
    return cos, sin


SHIFT_LIMIT = 60.0


def _softmax_shift(qg, kg):
    bound = A_HD * jnp.max(jnp.abs(qg)) * jnp.max(jnp.abs(kg))
    fast = bound <= SHIFT_LIMIT
    shift = jnp.where(fast, jnp.ceil(bound), 0.0)
    return shift, jnp.logical_not(fast).astype(I32)


def _proj_attn_operands(nb, ntot, gain, w_in, q_norm, k_norm, cos, sin):
    nq = A_HEADS * A_HD
    nk = A_KV * A_HD
    w = w_in.astype(BF16)
    qg = (jnp.tile(q_norm, A_HEADS) * (A_HD ** -0.5 * LOG2E)).reshape(1, nq)
    kg = jnp.tile(k_norm, A_KV).reshape(1, nk)
    shift, flag = _softmax_shift(qg, kg)
    neg_shift = (-shift).reshape(1, 1).astype(F32)
    tab = pl.BlockSpec((TILE, LANE), lambda b, t, *_: (t, 0))
    arrays = [gain, w, qg, kg, cos, sin, neg_shift]
    in_specs = [_full_spec(gain), _full_spec(w), _full_spec(qg), _full_spec(kg), tab, tab, _full_spec(neg_shift)]
    out_shape = [
        jax.ShapeDtypeStruct((nb, A_HEADS, ntot, LANE), BF16),
        jax.ShapeDtypeStruct((nb, A_KV, LANE, ntot), BF16),
        jax.ShapeDtypeStruct((nb, A_KV, ntot, LANE), BF16),
    ]
    out_specs = [pl.BlockSpec((1, A_HEADS, TILE, LANE), lambda b, t, *_: (b, 0, t, 0)),
                 pl.BlockSpec((1, A_KV, LANE, TILE), lambda b, t, *_: (b, 0, 0, t)),
                 pl.BlockSpec((1, A_KV, TILE, LANE), lambda b, t, *_: (b, 0, t, 0))]
    return (arrays, in_specs, out_shape, out_specs), flag.reshape(1)


def _attn_kernel(flag_ref, q_ref, kt_ref, vd_ref, o_ref, *, ntot):
    low = _iota((TILE, LANE), 1) < A_HD

    def attend(row0, nk, row_max):
        outs = []
        for hd in range(A_GRP):
            s = _dot(q_ref[0, hd, pl.ds(row0, TILE), :], kt_ref[0, 0, :, :nk])
            if row_max:
                s = s - jnp.max(s, axis=1, keepdims=True)
            r = _dot(jnp.exp2(s).astype(BF16), vd_ref[0, 0, :nk, :])
            outs.append(r / r[:, A_HD:A_HD + 1])
        t0 = jnp.where(low, outs[0], pltpu.roll(outs[1], A_HD, 1))
        t1 = jnp.where(low, outs[2], pltpu.roll(outs[3], A_HD, 1))
        o_ref[0, 0, pl.ds(row0, TILE), :] = jnp.concatenate([t0, t1], axis=1).astype(BF16)

    attend(0, TILE, True)

    def latent_tiles(row_max):
        n_lat = ntot // TILE - 1

        def body(i, carry):
            attend(pl.multiple_of((2 * i + 1) * TILE, TILE), ntot, row_max)
            attend(pl.multiple_of((2 * i + 2) * TILE, TILE), ntot, row_max)
            return carry

        lax.fori_loop(0, n_lat // 2, body, 0)
        if n_lat % 2:
            attend(n_lat * TILE, ntot, row_max)

    @pl.when(flag_ref[0] == 0)
    def _():
        latent_tiles(False)

    @pl.when(flag_ref[0] != 0)
    def _():
        latent_tiles(True)


def _attention(q, kt, vd, flag):
    nb, _, ntot, _ = q.shape
    grid_spec = pltpu.PrefetchScalarGridSpec(
        num_scalar_prefetch=1,
        grid=(nb, A_KV),
        in_specs=[pl.BlockSpec((1, A_GRP, ntot, LANE), lambda b, g, f: (b, g, 0, 0)),
                  pl.BlockSpec((1, 1, LANE, ntot), lambda b, g, f: (b, g, 0, 0)),
                  pl.BlockSpec((1, 1, ntot, LANE), lambda b, g, f: (b, g, 0, 0))],
        out_specs=pl.BlockSpec((1, 1, ntot, A_GRP * A_HD), lambda b, g, f: (b, g, 0, 0)),
    )
    return pl.pallas_call(
        functools.partial(_attn_kernel, ntot=ntot),
        out_shape=jax.ShapeDtypeStruct((nb, A_KV, ntot, A_GRP * A_HD), BF16),
        grid_spec=grid_spec,
        compiler_params=_cparams(("arbitrary", "arbitrary")),
        name="attention",
    )(flag, q, kt, vd)


def _route_tail(x, y, m, gain_ref, wrt_ref, x1_ref, h2_ref, aff_ref, rows):
    x1 = x + m[2:3] * y
    x1_ref[0, rows] = x1
    h2 = _rms_mod(x1, gain_ref[...], m[3:4], m[4:5])
    h2_ref[0, rows] = h2.astype(BF16)
    logits = _dot3_nt(wrt_ref[...], h2)
    e = jnp.exp(logits - jnp.max(logits, axis=0, keepdims=True))
    aff_ref[0, :, rows] = e / jnp.sum(e, axis=0, keepdims=True)


def _dot3_nt(a, b):
    ah, al = _split2(a)
    bh, bl = _split2(b)
    return _dot_nt(ah, bh) + (_dot_nt(ah, bl) + _dot_nt(al, bh))


def _readout_mlstm_kernel(*refs, n_src):
    x = _tile_value(refs[:n_src])
    mod_ref, hf_ref, hb_ref, o_ref, mn_ref, wo_ref, gain_ref, wrt_ref, x1_ref, h2_ref, aff_ref = refs[n_src:]
    mn = mn_ref[...]
    for rows in HALVES:
        hh = hf_ref[0, rows].astype(F32) + hb_ref[0, rows].astype(F32)
        parts = []
        for h in range(M_HEADS):
            seg = hh[:, M_DV * h:M_DV * (h + 1)]
            ms = jnp.mean(seg * seg, axis=-1, keepdims=True)
            parts.append(seg * lax.rsqrt(ms + EPS) * mn[:, M_DV * h:M_DV * (h + 1)])
        z = jnp.concatenate(parts, axis=1) * jax.nn.sigmoid(o_ref[0, rows].astype(F32))
        y = _dot(z.astype(BF16), wo_ref[...])
        _route_tail(x[rows], y, mod_ref[0], gain_ref, wrt_ref, x1_ref, h2_ref, aff_ref, rows)


def _readout_attn_kernel(*refs, n_src):
    x = _tile_value(refs[:n_src])
    mod_ref, oa_ref, wo_ref, gain_ref, wrt_ref, x1_ref, h2_ref, aff_ref = refs[n_src:]
    gw = A_GRP * A_HD
    for rows in HALVES:
        y = _dot(oa_ref[0, 0, rows], wo_ref[0:gw, :])
        for g in range(1, A_KV):
            y = y + _dot(oa_ref[0, g, rows], wo_ref[gw * g:gw * (g + 1), :])
        _route_tail(x[rows], y, mod_ref[0], gain_ref, wrt_ref, x1_ref, h2_ref, aff_ref, rows)


def _readout(xa, mod_i, mixer_out, w_out, gain_ffn, w_router, mlstm_norm=None):
    srcs, src_specs, nb, ntot = _tile_sources(xa)
    nt = ntot // TILE
    wo = w_out.astype(BF16)
    wrt = w_router.T
    tail = [wo, gain_ffn, wrt]
    if mlstm_norm is not None:
        hf, hb, og = mixer_out
        kern = _readout_mlstm_kernel
        mid = [hf, hb, og, mlstm_norm]
        mid_in = [_tok_spec(M_V), _tok_spec(M_V), _tok_spec(M_V), _full_spec(mlstm_norm)]
    else:
        kern = _readout_attn_kernel
        mid = [mixer_out]
        mid_in = [pl.BlockSpec((1, A_KV, TILE, A_GRP * A_HD), lambda b, t: (b, 0, t, 0))]
    return pl.pallas_call(
        functools.partial(kern, n_src=len(srcs)),
        out_shape=(jax.ShapeDtypeStruct((nb, ntot, D), F32), jax.ShapeDtypeStruct((nb, ntot, D), BF16),
                   jax.ShapeDtypeStruct((nb, N_EXP, ntot), F32)),
        grid=(nb, nt),
        in_specs=src_specs + [_mod_spec(nb)] + mid_in + [_full_spec(a) for a in tail],
        out_specs=(_tok_spec(D), _tok_spec(D), pl.BlockSpec((1, N_EXP, TILE), lambda b, t: (b, 0, t))),
        compiler_params=_cparams(("arbitrary", "arbitrary")),
        name="readout_route",
    )(*srcs, mod_i, *mid, *tail)


def _slot_geometry(n_ctx, n_lat):
    cap_ctx = max(1, EC_CAPACITY * n_ctx // N_EXP)
    cap_lat = max(1, EC_CAPACITY * n_lat // N_EXP)
    lat_base = -(-cap_ctx // SUB) * SUB
    slots_max = lat_base + cap_lat + (SUB - 1) * (n_lat // TILE)
    rows_ffn = -(-slots_max // 16) * 16
    return cap_ctx, cap_lat, lat_base, rows_ffn


def _topk_kernel(a_ref, slot_ref, slot_t_ref, aff_t_ref, off_ref, cnt_ref, *, n_ctx, cap_ctx, cap_lat, lat_base):
    bits = lax.bitcast_convert_type(a_ref[0], I32)
    prefix = (_iota((LANE, LANE), 0) <= _iota((LANE, LANE), 1)).astype(BF16)

    def count(mask):
        return jnp.sum(mask.astype(F32), axis=1, keepdims=True)

    def select(x, cap):
        thr = jnp.zeros((N_EXP, 1), I32)
        for bit in range(30, -1, -1):
            cand = thr | (1 << bit)
            thr = jnp.where(count(x >= cand) >= cap, cand, thr)
        gt = x > thr
        eq = x == thr
        need = cap - count(gt)
        run = jnp.zeros((N_EXP, 1), F32)
        blocks = []
        for j in range(x.shape[1] // LANE):
            sl = slice(LANE * j, LANE * (j + 1))
            eqf = eq[:, sl].astype(F32)
            inc = _dot(eqf.astype(BF16), prefix)
            rank = run + inc - eqf
            blocks.append(jnp.logical_or(gt[:, sl], jnp.logical_and(eq[:, sl], rank < need)))
            run = run + inc[:, LANE - 1:LANE]
        return blocks

    blocks = select(bits[:, :n_ctx], cap_ctx) + select(bits[:, n_ctx:], cap_lat)
    per_tile = TILE // LANE
    lane = _iota((N_EXP, LANE), 1)
    off_acc = jnp.zeros((N_EXP, LANE), I32)
    cnt_acc = jnp.zeros((N_EXP, LANE), I32)
    base = jnp.zeros((N_EXP, 1), F32)
    for t in range(len(blocks) // per_tile):
        if t == n_ctx // TILE:
            base = jnp.full((N_EXP, 1), float(lat_base), F32)
        run = jnp.zeros((N_EXP, 1), F32)
        for j in range(per_tile):
            blk = blocks[per_tile * t + j]
            sf = blk.astype(F32)
            inc = _dot(sf.astype(BF16), prefix)
            pos = base + run + inc - sf
            c0 = LANE * (per_tile * t + j)
            slot_blk = jnp.concatenate([jnp.where(blk, pos.astype(I32), -1),
                                        jnp.full((LANE - N_EXP, LANE), -1, I32)], axis=0)
            slot_ref[0, :, c0:c0 + LANE] = slot_blk
            slot_t_ref[0, c0:c0 + LANE, :] = jnp.transpose(slot_blk)
            aff_blk = jnp.concatenate([a_ref[0, :, c0:c0 + LANE], jnp.zeros((LANE - N_EXP, LANE), F32)], axis=0)
            aff_t_ref[0, c0:c0 + LANE, :] = jnp.transpose(aff_blk)
            run = run + inc[:, LANE - 1:LANE]
        n8 = jnp.floor((run + (SUB - 1)) * (1.0 / SUB)) * SUB
        off_acc = jnp.where(lane == t, base.astype(I32), off_acc)
        cnt_acc = jnp.where(lane == t, n8.astype(I32), cnt_acc)
        base = base + n8
    off_ref[0] = off_acc
    cnt_ref[0] = cnt_acc


def _topk(aff_t, n_ctx):
    nb, _, ntot = aff_t.shape
    cap_ctx, cap_lat, lat_base, _ = _slot_geometry(n_ctx, ntot - n_ctx)
    kern = functools.partial(_topk_kernel, n_ctx=n_ctx, cap_ctx=cap_ctx, cap_lat=cap_lat, lat_base=lat_base)
    return pl.pallas_call(
        kern,
        out_shape=(jax.ShapeDtypeStruct((nb, LANE, ntot), I32), jax.ShapeDtypeStruct((nb, ntot, LANE), I32),
                   jax.ShapeDtypeStruct((nb, ntot, LANE), F32), jax.ShapeDtypeStruct((nb, N_EXP, LANE), I32),
                   jax.ShapeDtypeStruct((nb, N_EXP, LANE), I32)),
        grid=(nb,),
        in_specs=[pl.BlockSpec((1, N_EXP, ntot), lambda b: (b, 0, 0))],
        out_specs=(pl.BlockSpec((1, LANE, ntot), lambda b: (b, 0, 0)),
                   pl.BlockSpec((1, ntot, LANE), lambda b: (b, 0, 0)),
                   pl.BlockSpec((1, ntot, LANE), lambda b: (b, 0, 0)),
                   pl.BlockSpec((1, N_EXP, LANE), lambda b: (b, 0, 0)),
                   pl.BlockSpec((1, N_EXP, LANE), lambda b: (b, 0, 0))),
        compiler_params=_cparams(("arbitrary",)),
        name="ec_topk",
    )(aff_t)


def _strip_index(shape, dim):
    i = _iota(shape, dim)
    e = jnp.floor((i.astype(F32) + 0.5) * (1.0 / WIN)).astype(I32)
    return e, i - WIN * e


def _dispatch_kernel(off_s, rounds_s, h_ref, slot_ref, aff_ref, xg_ref, stage_ref, sem, cnt_ref):
    b = pl.program_id(0)
    t = pl.program_id(1)
    nt = pl.num_programs(1)
    base = (b * nt + t) * N_EXP
    rows = N_EXP * WIN
    a = aff_ref[0]
    src = _iota((LANE, LANE), 0)
    dst = _iota((LANE, LANE), 1)
    a3 = None
    for p, piece in enumerate(_split3(a)):
        sel = jnp.logical_and(dst == 3 * src + p, src < N_EXP).astype(BF16)
        term = _dot(piece, sel)
        a3 = term if a3 is None else a3 + term
    rhs = jnp.concatenate([h_ref[0], a3.astype(BF16)], axis=1)
    e_of_row, _ = _strip_index((rows, LANE), 0)
    expand = (e_of_row == _iota((rows, LANE), 1)).astype(BF16)
    _, j_row = _strip_index((rows, 1), 0)
    sl = slot_ref[0]
    e_row = _iota((LANE, 1), 0)
    off_v = jnp.zeros((LANE, 1), I32)
    for e in range(N_EXP):
        off_v = jnp.where(e_row == e, off_s[base + e], off_v)

    @pl.when(jnp.logical_and(b == 0, t == 0))
    def _():
        cnt_ref[0] = 0

    def strip_copies(buf, sample, starts):
        return [pltpu.make_async_copy(stage_ref.at[buf, pl.ds(WIN * e, WIN), :],
                                      xg_ref.at[sample, e, pl.ds(starts[e], WIN), :], sem.at[buf])
                for e in range(N_EXP)]

    def wait_strips(buf):
        for cp in strip_copies(buf, 0, [0] * N_EXP):
            cp.wait()

    def round_body(r, carry):
        n = cnt_ref[0]
        buf = n & 1
        rel = jnp.clip(sl - (off_v + WIN * r), -1, WIN).astype(F32).astype(BF16)
        relx = _dot(expand, rel)
        onehot = (relx == j_row.astype(F32)).astype(BF16)
        bits = lax.bitcast_convert_type(_dot(onehot, rhs), I32)
        packed = jnp.bitwise_or(jnp.bitwise_and(bits[:, :HALF] >> 16, 0xFFFF),
                                jnp.bitwise_and(bits[:, HALF:D], HIGH16))
        stage_ref[buf] = jnp.concatenate([packed, bits[:, D:]], axis=1)

        @pl.when(n > 0)
        def _():
            wait_strips(1 - buf)

        last_start = xg_ref.shape[2] - WIN
        starts = [pl.multiple_of(jnp.minimum(off_s[base + e] + WIN * r, last_start), SUB) for e in range(N_EXP)]
        for cp in strip_copies(buf, b, starts):
            cp.start()
        cnt_ref[0] = n + 1
        return carry

    lax.fori_loop(0, rounds_s[b * nt + t], round_body, 0)

    @pl.when(jnp.logical_and(b == pl.num_programs(0) - 1, t == nt - 1))
    def _():
        wait_strips((cnt_ref[0] - 1) & 1)


def _dispatch(h2, slot_pad, aff_pad, off_flat, rounds_flat, rows_alloc):
    nb, ntot, _ = h2.shape
    nt = ntot // TILE
    grid_spec = pltpu.PrefetchScalarGridSpec(
        num_scalar_prefetch=2,
        grid=(nb, nt),
        in_specs=[pl.BlockSpec((1, TILE, D), lambda b, t, o, r: (b, t, 0)),
                  pl.BlockSpec((1, LANE, TILE), lambda b, t, o, r: (b, 0, t)),
                  pl.BlockSpec((1, TILE, LANE), lambda b, t, o, r: (b, t, 0))],
        out_specs=pl.BlockSpec(memory_space=pl.ANY),
        scratch_shapes=[pltpu.VMEM((2, N_EXP * WIN, XP), I32), pltpu.SemaphoreType.DMA((2,)),
                        pltpu.SMEM((1,), I32)],
    )
    return pl.pallas_call(
        _dispatch_kernel,
        out_shape=jax.ShapeDtypeStruct((nb, N_EXP, rows_alloc, XP), I32),
        grid_spec=grid_spec,
        compiler_params=_cparams(("arbitrary", "arbitrary")),
        name="ec_dispatch",
    )(off_flat, rounds_flat, h2, slot_pad, aff_pad)


def _ffn_kernel(used_s, xg_ref, wg32_ref, wu32_ref, wd32_ref, y_ref, wg_ref, wu_ref, wd_ref, *, rows_short):
    e = pl.program_id(0)
    b = pl.program_id(1)

    @pl.when(b == 0)
    def _():
        wg_ref[0] = wg32_ref[0].astype(BF16)
        wu_ref[0] = wu32_ref[0].astype(BF16)
        wd_ref[0] = wd32_ref[0].astype(BF16)

    rows_all = y_ref.shape[2]

    def run(rows):
        words = xg_ref[0, 0, :rows, :HALF]
        x = jnp.concatenate([lax.bitcast_convert_type(words << 16, F32),
                             lax.bitcast_convert_type(jnp.bitwise_and(words, HIGH16), F32)], axis=1).astype(BF16)
        gl = lax.bitcast_convert_type(xg_ref[0, 0, :rows, HALF:], F32)
        lane = _iota(gl.shape, 1)
        mine = jnp.logical_and(lane >= 3 * e, lane < 3 * e + 3)
        gate = jnp.sum(jnp.where(mine, gl, 0.0), axis=1, keepdims=True)
        a = _dot(x, wg_ref[0])
        u = _dot(x, wu_ref[0])
        hm = (a * jax.nn.sigmoid(a) * u).astype(BF16)
        y_ref[0, 0, :rows] = _dot(hm, wd_ref[0]) * gate
        if rows < rows_all:
            y_ref[0, 0, rows:] = jnp.zeros((rows_all - rows, D), F32)

    used = used_s[b * N_EXP + e]

    @pl.when(used <= rows_short)
    def _():
        run(rows_short)

    @pl.when(used > rows_short)
    def _():
        run(rows_all)


def _expert_ffn(xg, wg, wu, wd, used_flat, rows_ffn, layer=0):
    nb = xg.shape[0]
    rows_short = max(16, rows_ffn - 3 * 16)
    wspec = pl.BlockSpec((None, 1, D, D), lambda e, b, u: (layer, e, 0, 0))
    grid_spec = pltpu.PrefetchScalarGridSpec(
        num_scalar_prefetch=1,
        grid=(N_EXP, nb),
        in_specs=[pl.BlockSpec((1, 1, rows_ffn, XP), lambda e, b, u: (b, e, 0, 0)), wspec, wspec, wspec],
        out_specs=pl.BlockSpec((1, 1, rows_ffn, D), lambda e, b, u: (b, e, 0, 0)),
        scratch_shapes=[pltpu.VMEM((1, D, D), BF16)] * 3,
    )
    return pl.pallas_call(
        functools.partial(_ffn_kernel, rows_short=rows_short),
        out_shape=jax.ShapeDtypeStruct((nb, N_EXP, rows_ffn, D), F32),
        grid_spec=grid_spec,
        compiler_params=_cparams(("arbitrary", "arbitrary")),
        name="ec_ffn",
    )(used_flat, xg, wg, wu, wd)


def _combine_kernel(off_s, rounds_s, x_ref, mod_ref, slot_ref, fin_ref, y_ref, *rest, final_ctx_tiles, proj_body,
                    n_proj_in):
    if proj_body is None:
        o_ref, strip_ref, sem = rest
    else:
        next_mod_ref = rest[0]
        proj_in = rest[1:1 + n_proj_in]
        o_ref = rest[1 + n_proj_in]
        proj_out = rest[2 + n_proj_in:-2]
        strip_ref, sem = rest[-2:]
    b = pl.program_id(0)
    t = pl.program_id(1)
    nt = pl.num_programs(1)
    base = (b * nt + t) * N_EXP
    cols = N_EXP * WIN
    e_of_col, _ = _strip_index((LANE, cols), 1)
    expand = (e_of_col == _iota((LANE, cols), 0)).astype(BF16)
    _, j_lane = _strip_index((1, cols), 1)
    j_lane = j_lane.astype(F32)
    sl = slot_ref[0]
    e_lane = _iota((1, LANE), 1)
    last_start = y_ref.shape[2] - WIN

    def strip_start(step_, e, r):
        return jnp.minimum(off_s[step_ * N_EXP + e] + WIN * r, last_start)

    step = b * nt + t
    buf = step & 1

    def strip_copies(step_, sample, r, dst):
        return [pltpu.make_async_copy(
            y_ref.at[sample, e, pl.ds(pl.multiple_of(strip_start(step_, e, r), SUB), WIN), :],
            strip_ref.at[dst, pl.ds(WIN * e, WIN), :], sem.at[dst]) for e in range(N_EXP)]

    @pl.when(step == 0)
    def _():
        for cp in strip_copies(step, b, 0, buf):
            cp.start()

    @pl.when(step + 1 < pl.num_programs(0) * nt)
    def _():
        for cp in strip_copies(step + 1, jnp.where(t == nt - 1, b + 1, b), 0, 1 - buf):
            cp.start()

    def expand_round(r, acc):
        first_v = jnp.zeros((1, LANE), I32)
        start_v = jnp.zeros((1, LANE), I32)
        for e in range(N_EXP):
            first_v = jnp.where(e_lane == e, off_s[base + e] + WIN * r, first_v)
            start_v = jnp.where(e_lane == e, strip_start(step, e, r), start_v)
        nominal = sl - first_v
        in_round = jnp.logical_and(nominal >= 0, nominal < WIN)
        rel = jnp.where(in_round, sl - start_v, -1).astype(F32).astype(BF16)
        relx = _dot(rel, expand)
        onehot = (relx == j_lane).astype(BF16)
        for cp in strip_copies(step, b, r, buf):
            cp.wait()
        return acc + _dot(onehot, strip_ref[buf].astype(BF16))

    def extra_round(r, acc):
        for cp in strip_copies(step, b, r, buf):
            cp.start()
        return expand_round(r, acc)

    acc = expand_round(0, jnp.zeros((TILE, D), F32))
    acc = lax.fori_loop(1, rounds_s[step], extra_round, acc)
    x2 = x_ref[0] + mod_ref[0][5:6] * acc
    if final_ctx_tiles is None:
        o_ref[0] = x2
        if proj_body is not None:
            proj_body(x2, next_mod_ref[0], *proj_in, *proj_out)
    else:
        @pl.when(t >= final_ctx_tiles)
        def _():
            ms = jnp.mean(x2 * x2, axis=-1, keepdims=True)
            o_ref[0] = x2 * lax.rsqrt(ms + EPS) * fin_ref[...]


def _combine(x1, mod_i, slot_t_pad, y, off_flat, rounds_flat, final_gain, final_ctx_tiles=None, next_proj=None):
    nb, ntot, _ = x1.shape
    nt = ntot // TILE
    skip = 0 if final_ctx_tiles is None else final_ctx_tiles
    in_specs = [_tok_spec(D), _mod_spec(nb), _tok_spec(LANE), pl.BlockSpec((1, D), lambda b, t, *_: (0, 0)),
                pl.BlockSpec(memory_space=pl.ANY)]
    args = [x1, mod_i, slot_t_pad, final_gain, y]
    out_shape = [jax.ShapeDtypeStruct((nb, ntot - skip * TILE, D), F32)]
    out_specs = [pl.BlockSpec((1, TILE, D), lambda b, t, *_: (b, jnp.maximum(t - skip, 0), 0))]
    proj_body, n_proj_in = None, 0
    if next_proj is not None:
        proj_body, next_mod, (arrays, specs, p_shape, p_specs) = next_proj
        in_specs += [_mod_spec(nb)] + specs
        args += [next_mod] + arrays
        out_shape += p_shape
        out_specs += p_specs
        n_proj_in = len(arrays)
    grid_spec = pltpu.PrefetchScalarGridSpec(
        num_scalar_prefetch=2,
        grid=(nb, nt),
        in_specs=in_specs,
        out_specs=tuple(out_specs),
        scratch_shapes=[pltpu.VMEM((2, N_EXP * WIN, D), F32), pltpu.SemaphoreType.DMA((2,))],
    )
    outs = pl.pallas_call(
        functools.partial(_combine_kernel, final_ctx_tiles=final_ctx_tiles, proj_body=proj_body,
                          n_proj_in=n_proj_in),
        out_shape=tuple(out_shape),
        grid_spec=grid_spec,
        compiler_params=_cparams(("arbitrary", "arbitrary")),
        name="ec_combine",
    )(off_flat, rounds_flat, *args)
    return outs[0], tuple(outs[1:])


def _moe(x1, h2, aff_t, mod_i, wg, wu, wd, n_ctx, final_gain, last, layer=0, next_proj=None):
    nb, ntot, _ = x1.shape
    nt = ntot // TILE
    _, _, _, rows_ffn = _slot_geometry(n_ctx, ntot - n_ctx)
    rows_alloc = rows_ffn + WIN
    slot_pad, slot_t_pad, aff_pad, off, cnt = _topk(aff_t, n_ctx)
    off_t = jnp.swapaxes(off[:, :, :nt], 1, 2)
    cnt_t = jnp.swapaxes(cnt[:, :, :nt], 1, 2)
    rounds_c = jnp.maximum(1, jnp.max((cnt_t + WIN - 1) // WIN, axis=2))
    fill = jnp.max((rows_alloc - off_t[:, nt - 1, :] + WIN - 1) // WIN, axis=1)
    rounds_d = rounds_c.at[:, nt - 1].max(fill)
    off_flat = off_t.reshape(-1)
    xg = _dispatch(h2, slot_pad, aff_pad, off_flat, rounds_d.reshape(-1), rows_alloc)
    used = (off_t[:, nt - 1, :] + cnt_t[:, nt - 1, :]).reshape(-1)
    y = _expert_ffn(xg, wg, wu, wd, used, rows_ffn, layer)
    return _combine(x1, mod_i, slot_t_pad, y, off_flat, rounds_c.reshape(-1), final_gain,
                    final_ctx_tiles=n_ctx // TILE if last else None, next_proj=next_proj)


def kernel(x, c, ctx, c_ctx, w_mod, b_mod, norm_mix, norm_ffn, mlstm_w_in, mlstm_b_gate, mlstm_norm, mlstm_w_out,
           attn_w_in, attn_q_norm, attn_k_norm, attn_w_out, moe_router, moe_w_gate, moe_w_up, moe_w_down,
           norm_final):
    nb, n_lat, _ = x.shape
    n_ctx = ctx.shape[1]
    depth = w_mod.shape[0]
    assert n_ctx == TILE and n_lat % TILE == 0 and x.shape[2] == D
    xa = (ctx, x)
    rb = -(-(nb + 1) // SUB) * SUB
    cc = jnp.concatenate([c, c_ctx[None, :], jnp.zeros((rb - nb - 1, D), F32)], axis=0)
    mod = _modulation(cc, w_mod, b_mod)
    cos, sin = _rope_tables(n_lat, n_ctx)
    ntot = n_ctx + n_lat

    def proj_operands(i):
        j = i // 2
        gain_mix = norm_mix[i].reshape(1, D)
        if i % 2 == 0:
            return _proj_mlstm_body, _proj_mlstm_operands(nb, ntot, gain_mix, mlstm_w_in[j], mlstm_b_gate[j]), None
        ops, flag = _proj_attn_operands(nb, ntot, gain_mix, attn_w_in[j], attn_q_norm[j], attn_k_norm[j], cos, sin)
        return _proj_attn_body, ops, flag

    _, ops, flag = proj_operands(0)
    proj = _proj_mlstm(xa, mod[0], ops)
    for i in range(depth):
        j = i // 2
        mod_i = mod[i]
        gain_ffn = norm_ffn[i].reshape(1, D)
        if i % 2 == 0:
            q, kt, v, og, gc, gr = proj
            hf, hb = _mlstm_scan(q, kt, v, gc, gr)
            x1, h2, aff_t = _readout(xa, mod_i, (hf, hb, og), mlstm_w_out[j], gain_ffn, moe_router[i],
                                     mlstm_norm=mlstm_norm[j].reshape(1, M_V))
        else:
            q, kt, vd = proj
            oa = _attention(q, kt, vd, flag)
            x1, h2, aff_t = _readout(xa, mod_i, oa, attn_w_out[j], gain_ffn, moe_router[i])
        last = i == depth - 1
        next_proj = None
        if not last:
            body, ops, flag = proj_operands(i + 1)
            next_proj = (body, mod[i + 1], ops)
        xa, proj = _moe(x1, h2, aff_t, mod_i, moe_w_gate, moe_w_up, moe_w_down, n_ctx,
                        norm_final.reshape(1, D), last=last, layer=i, next_proj=next_proj)
    return xa
```
